```python
import math
import jax, jax.numpy as jnp
from jax import lax
import numpy as np

D_MODEL = 4096
BATCH = 4
SEQ = 2048
DEPTH = 2
DEC_BATCH = 8
DEC_SEQ = 8
PAST_LEN = 16384
PAGE_SIZE = 128

HEAD_DIM = 128
N_A_LAYERS = DEPTH // 2
N_B_LAYERS = DEPTH - N_A_LAYERS
FOX_HEADS = D_MODEL // HEAD_DIM
PATTERNS = ((128, 1), (512, 4), (2048, 16))
N_GROUPS = len(PATTERNS)
GROUP_HEADS = D_MODEL // (2 * HEAD_DIM)
B_Q_HEADS = N_GROUPS * GROUP_HEADS
B_KV_HEADS = GROUP_HEADS // 2
GQA_REP = GROUP_HEADS // B_KV_HEADS
MAX_WINDOW = max(w for w, _ in PATTERNS)
N_BUCKETS = 32
BUCKET_MAX_DIST = MAX_WINDOW
D_FF = 11008
CONV_W = 3
Q_BLOCK = 128
NORM_EPS = 1e-6
ATTN_SCALE = HEAD_DIM ** -0.5
NEG_INF = -1e30

kernel_name = "yoco_fox_dilated_convffn_step"


def rmsnorm(x, g):
    xf = x.astype(jnp.float32)
    y = xf * lax.rsqrt(jnp.mean(xf * xf, axis=-1, keepdims=True) + NORM_EPS)
    return (y * g.astype(jnp.float32)).astype(x.dtype)


def t5_bucket(dist):
    n = np.asarray(dist, dtype=np.int64)
    exact = N_BUCKETS // 2
    large = exact + (np.log(np.maximum(n, 1) / exact) / np.log(BUCKET_MAX_DIST / exact) * (N_BUCKETS - exact)).astype(np.int64)
    return np.where(n < exact, n, np.minimum(large, N_BUCKETS - 1)).astype(np.int32)


def fox_project(xn, w_in, b_f):
    b, t, _ = xn.shape
    hd = FOX_HEADS * HEAD_DIM
    proj = xn @ w_in
    q = proj[..., :hd].reshape(b, t, FOX_HEADS, HEAD_DIM)
    k = proj[..., hd:2 * hd].reshape(b, t, FOX_HEADS, HEAD_DIM)
    v = proj[..., 2 * hd:3 * hd].reshape(b, t, FOX_HEADS, HEAD_DIM)
    logf = jax.nn.log_sigmoid(proj[..., 3 * hd:].astype(jnp.float32) + b_f.astype(jnp.float32))
    return q, k, v, logf


def fox_prompt_attention(q, k, v, logf):
    b, s, h, dh = q.shape
    nb = s // Q_BLOCK
    c = jnp.cumsum(logf, axis=1)
    c_k = c.transpose(0, 2, 1)
    q_blocks = q.reshape(b, nb, Q_BLOCK, h, dh).swapaxes(0, 1)
    c_blocks = c.reshape(b, nb, Q_BLOCK, h).swapaxes(0, 1)
    starts = jnp.arange(nb, dtype=jnp.int32) * Q_BLOCK
    k_pos = jnp.arange(s, dtype=jnp.int32)

    def block(args):
        q_i, c_i, s0 = args
        q_pos = s0 + jnp.arange(Q_BLOCK, dtype=jnp.int32)
        logits = jnp.einsum('bqhd,bkhd->bhqk', q_i, k, preferred_element_type=jnp.float32) * ATTN_SCALE
        logits = logits + c_i.transpose(0, 2, 1)[..., None] - c_k[:, :, None, :]
        logits = jnp.where(k_pos[None, :] <= q_pos[:, None], logits, -jnp.inf)
        p = jax.nn.softmax(logits, axis=-1)
        return jnp.einsum('bhqk,bkhd->bqhd', p.astype(v.dtype), v)

    o = lax.map(block, (q_blocks, c_blocks, starts))
    return o.swapaxes(0, 1).reshape(b, s, h * dh)


def online_softmax_step(carry, logits, v_blk):
    m, l, acc = carry
    m_new = jnp.maximum(m, jnp.max(logits, axis=-1))
    alpha = jnp.exp(m - m_new)
    p = jnp.exp(logits - m_new[..., None])
    l = l * alpha + jnp.sum(p, axis=-1)
    acc = acc * alpha[..., None] + jnp.einsum('bhts,bshd->bhtd', p.astype(v_blk.dtype), v_blk, preferred_element_type=jnp.float32)
    return (m_new, l, acc)


def fox_sample_attention(q, k_new, v_new, logf_new, k_pool, v_pool, lf_pool, layer, page_table):
    b, t, h, dh = q.shape
    n_pages = page_table.shape[1]
    past = n_pages * PAGE_SIZE
    lf_past = lf_pool[layer, page_table].astype(jnp.float32).reshape(b, past, h)
    c = jnp.cumsum(jnp.concatenate([lf_past, logf_new], axis=1), axis=1)
    c_q = c[:, past:].transpose(0, 2, 1)
    c_k = c[:, :past].reshape(b, n_pages, PAGE_SIZE, h).transpose(1, 0, 3, 2)

    def step(carry, xs):
        pages, ck = xs
        k_blk = k_pool[layer, pages]
        v_blk = v_pool[layer, pages]
        logits = jnp.einsum('bthd,bshd->bhts', q, k_blk, preferred_element_type=jnp.float32) * ATTN_SCALE
        logits = logits + c_q[..., None] - ck[:, :, None, :]
        return online_softmax_step(carry, logits, v_blk), None

    init = (jnp.full((b, h, t), NEG_INF, jnp.float32), jnp.zeros((b, h, t), jnp.float32), jnp.zeros((b, h, t, dh), jnp.float32))
    carry, _ = lax.scan(step, init, (page_table.T, c_k))
    logits = jnp.einsum('bthd,bshd->bhts', q, k_new, preferred_element_type=jnp.float32) * ATTN_SCALE
    logits = logits + c_q[..., None] - c_q[:, :, None, :]
    causal = jnp.tril(jnp.ones((t, t), dtype=bool))
    logits = jnp.where(causal, logits, NEG_INF)
    m, l, acc = online_softmax_step(carry, logits, v_new)
    o = (acc / l[..., None]).transpose(0, 2, 1, 3).reshape(b, t, h * dh)
    return o.astype(q.dtype)


def shared_kv(x, g_kv, w_kv_b):
    b, t, _ = x.shape
    kvw = B_KV_HEADS * HEAD_DIM
    h = rmsnorm(x, g_kv) @ w_kv_b
    return h[..., :kvw].reshape(b, t, B_KV_HEADS, HEAD_DIM), h[..., kvw:].reshape(b, t, B_KV_HEADS, HEAD_DIM)


def dilated_attention(q, q_pos, k, v, key_offset, rel_bias):
    b, t = q.shape[:2]
    n_rows = k.shape[1]
    outs, lses = [], []
    for g, (window, dil) in enumerate(PATTERNS):
        dist = dil * np.arange(window // dil + 1)
        a_pos = q_pos[:, None] - jnp.asarray(dist, jnp.int32)[None, :]
        valid = a_pos >= 0
        rows = jnp.clip(a_pos - key_offset, 0, n_rows - 1)
        k_g = jnp.take(k, rows, axis=1)
        v_g = jnp.take(v, rows, axis=1)
        q_g = q[:, :, g * GROUP_HEADS:(g + 1) * GROUP_HEADS].reshape(b, t, B_KV_HEADS, GQA_REP, HEAD_DIM)
        bias = rel_bias[t5_bucket(dist)][:, g * GROUP_HEADS:(g + 1) * GROUP_HEADS]
        bias = bias.astype(jnp.float32).T.reshape(B_KV_HEADS, GQA_REP, dist.shape[0])
        logits = jnp.einsum('btkrd,btjkd->btkrj', q_g, k_g, preferred_element_type=jnp.float32) * ATTN_SCALE + bias
        logits = jnp.where(valid[None, :, None, None, :], logits, -jnp.inf)
        m = jnp.max(logits, axis=-1, keepdims=True)
        p = jnp.exp(logits - m)
        den = jnp.sum(p, axis=-1, keepdims=True)
        outs.append(jnp.einsum('btkrj,btjkd->btkrd', (p / den).astype(v.dtype), v_g, preferred_element_type=jnp.float32))
        lses.append((m + jnp.log(den))[..., 0])
    w = jax.nn.softmax(jnp.stack(lses), axis=0)
    o = jnp.sum(w[..., None] * jnp.stack(outs), axis=0)
    return o.reshape(b, t, GROUP_HEADS * HEAD_DIM).astype(q.dtype)


def dilated_prompt_attention(q, k, v, rel_bias):
    b, s = q.shape[:2]
    nb = s // Q_BLOCK
    q_blocks = q.reshape(b, nb, Q_BLOCK, B_Q_HEADS, HEAD_DIM).swapaxes(0, 1)
    starts = jnp.arange(nb, dtype=jnp.int32) * Q_BLOCK

    def block(args):
        q_i, s0 = args
        return dilated_attention(q_i, s0 + jnp.arange(Q_BLOCK, dtype=jnp.int32), k, v, 0, rel_bias)

    o = lax.map(block, (q_blocks, starts))
    return o.swapaxes(0, 1).reshape(b, s, GROUP_HEADS * HEAD_DIM)


def conv_ffn(x, prev, g, w_up, w_gate, w_conv, b_conv, w_down):
    t = x.shape[1]
    xn = rmsnorm(x, g)
    u = xn @ w_up
    gate = xn @ w_gate
    u_pad = jnp.concatenate([prev.astype(u.dtype), u], axis=1)
    u_conv = b_conv + u_pad[:, 0:t] * w_conv[0]
    for i in range(1, CONV_W):
        u_conv = u_conv + u_pad[:, i:i + t] * w_conv[i]
    h = jax.nn.gelu(u_conv, approximate=False) * gate
    return x + h @ w_down, u_pad[:, t:]


def prompt_forward(x, w_in_a, b_f_a, w_o_a, g_attn, g_kv, w_kv_b, w_q_b, w_o_b, rel_bias,
                   g_ffn, w_up, w_gate, w_conv, b_conv, w_down, g_final):
    b, s, _ = x.shape
    fox_k, fox_v, fox_lf, conv_state = [], [], [], []
    k_b = v_b = None
    zeros_prev = jnp.zeros((b, CONV_W - 1, D_FF), x.dtype)
    for l in range(DEPTH):
        xn = rmsnorm(x, g_attn[l])
        if l < N_A_LAYERS:
            q, k, v, logf = fox_project(xn, w_in_a[l], b_f_a[l])
            x = x + fox_prompt_attention(q, k, v, logf) @ w_o_a[l]
            fox_k.append(k.reshape(b, s // PAGE_SIZE, PAGE_SIZE, FOX_HEADS, HEAD_DIM))
            fox_v.append(v.reshape(b, s // PAGE_SIZE, PAGE_SIZE, FOX_HEADS, HEAD_DIM))
            fox_lf.append(logf.reshape(b, s // PAGE_SIZE, PAGE_SIZE, FOX_HEADS))
        else:
            if l == N_A_LAYERS:
                k_b, v_b = shared_kv(x, g_kv, w_kv_b)
            q = (xn @ w_q_b[l - N_A_LAYERS]).reshape(b, s, B_Q_HEADS, HEAD_DIM)
            x = x + dilated_prompt_attention(q, k_b, v_b, rel_bias) @ w_o_b[l - N_A_LAYERS]
        x, st = conv_ffn(x, zeros_prev, g_ffn[l], w_up[l], w_gate[l], w_conv[l], b_conv[l], w_down[l])
        conv_state.append(st)
    n_buf = min(MAX_WINDOW, s)
    return (rmsnorm(x, g_final), jnp.stack(fox_k), jnp.stack(fox_v), jnp.stack(fox_lf),
            k_b[:, s - n_buf:], v_b[:, s - n_buf:], jnp.stack(conv_state))


def sample_forward(x, cache_fox_k, cache_fox_v, cache_fox_logf, cache_win_k, cache_win_v, state_ffn_conv, page_table,
                   w_in_a, b_f_a, w_o_a, g_attn, g_kv, w_kv_b, w_q_b, w_o_b, rel_bias,
                   g_ffn, w_up, w_gate, w_conv, b_conv, w_down, g_final):
    b, t, _ = x.shape
    past = page_table.shape[1] * PAGE_SIZE
    n_buf = cache_win_k.shape[1]
    q_pos = past + jnp.arange(t, dtype=jnp.int32)
    fox_k, fox_v, fox_lf, conv_state = [], [], [], []
    keys = values = None
    for l in range(DEPTH):
        xn = rmsnorm(x, g_attn[l])
        if l < N_A_LAYERS:
            q, k, v, logf = fox_project(xn, w_in_a[l], b_f_a[l])
            a = fox_sample_attention(q, k, v, logf, cache_fox_k, cache_fox_v, cache_fox_logf, l, page_table)
            x = x + a @ w_o_a[l]
            fox_k.append(k)
            fox_v.append(v)
            fox_lf.append(logf)
        else:
            if l == N_A_LAYERS:
                k_new, v_new = shared_kv(x, g_kv, w_kv_b)
                keys = jnp.concatenate([cache_win_k.astype(k_new.dtype), k_new], axis=1)
                values = jnp.concatenate([cache_win_v.astype(v_new.dtype), v_new], axis=1)
            q = (xn @ w_q_b[l - N_A_LAYERS]).reshape(b, t, B_Q_HEADS, HEAD_DIM)
            a = dilated_attention(q, q_pos, keys, values, past - n_buf, rel_bias)
            x = x + a @ w_o_b[l - N_A_LAYERS]
        x, st = conv_ffn(x, state_ffn_conv[l], g_ffn[l], w_up[l], w_gate[l], w_conv[l], b_conv[l], w_down[l])
        conv_state.append(st)
    return (rmsnorm(x, g_final), jnp.stack(fox_k), jnp.stack(fox_v), jnp.stack(fox_lf),
            keys[:, t:], values[:, t:], jnp.stack(conv_state))


def setup_inputs(seed: int = 0) -> dict:
    key = jax.random.key(seed)
    ks = iter(jax.random.split(key, 40))

    def nrm(shape, scale):
        return jax.random.normal(next(ks), shape, jnp.float32) * scale

    n_pages = PAST_LEN // PAGE_SIZE
    n_pool = (5 * DEC_BATCH * n_pages + 3) // 4
    n_buf = min(MAX_WINDOW, PAST_LEN)
    hd_a = FOX_HEADS * HEAD_DIM
    x_prompt = nrm((BATCH, SEQ, D_MODEL), 1.0)
    x_sample = nrm((DEC_BATCH, DEC_SEQ, D_MODEL), 1.0)
    cache_fox_k = nrm((N_A_LAYERS, n_pool, PAGE_SIZE, FOX_HEADS, HEAD_DIM), 1.0)
    cache_fox_v = nrm((N_A_LAYERS, n_pool, PAGE_SIZE, FOX_HEADS, HEAD_DIM), 1.0)
    cache_fox_logf = jax.nn.log_sigmoid(1.0 + nrm((N_A_LAYERS, n_pool, PAGE_SIZE, FOX_HEADS), 1.0))
    cache_win_k = nrm((DEC_BATCH, n_buf, B_KV_HEADS, HEAD_DIM), 1.0)
    cache_win_v = nrm((DEC_BATCH, n_buf, B_KV_HEADS, HEAD_DIM), 1.0)
    state_ffn_conv = nrm((DEPTH, DEC_BATCH, CONV_W - 1, D_FF), 1.0)
    page_table = jax.random.permutation(next(ks), n_pool)[:DEC_BATCH * n_pages].reshape(DEC_BATCH, n_pages).astype(jnp.int32)
    w_in_a = nrm((N_A_LAYERS, D_MODEL, 3 * hd_a + FOX_HEADS), D_MODEL ** -0.5)
    b_f_a = 1.0 + nrm((N_A_LAYERS, FOX_HEADS), 0.1)
    w_o_a = nrm((N_A_LAYERS, hd_a, D_MODEL), hd_a ** -0.5)
    g_attn = 1.0 + nrm((DEPTH, D_MODEL), 0.05)
    g_kv = 1.0 + nrm((D_MODEL,), 0.05)
    w_kv_b = nrm((D_MODEL, 2 * B_KV_HEADS * HEAD_DIM), D_MODEL ** -0.5)
    w_q_b = nrm((N_B_LAYERS, D_MODEL, B_Q_HEADS * HEAD_DIM), D_MODEL ** -0.5)
    w_o_b = nrm((N_B_LAYERS, GROUP_HEADS * HEAD_DIM, D_MODEL), (GROUP_HEADS * HEAD_DIM) ** -0.5)
    rel_bias = nrm((N_BUCKETS, B_Q_HEADS), 0.5)
    g_ffn = 1.0 + nrm((DEPTH, D_MODEL), 0.05)
    w_up = nrm((DEPTH, D_MODEL, D_FF), D_MODEL ** -0.5)
    w_gate = nrm((DEPTH, D_MODEL, D_FF), D_MODEL ** -0.5)
    w_conv = nrm((DEPTH, CONV_W, D_FF), CONV_W ** -0.5)
    b_conv = nrm((DEPTH, D_FF), 0.01)
    w_down = nrm((DEPTH, D_FF, D_MODEL), D_FF ** -0.5)
    g_final = 1.0 + nrm((D_MODEL,), 0.05)
    return {"x_prompt": x_prompt, "x_sample": x_sample, "cache_fox_k": cache_fox_k, "cache_fox_v": cache_fox_v,
            "cache_fox_logf": cache_fox_logf, "cache_win_k": cache_win_k, "cache_win_v": cache_win_v,
            "state_ffn_conv": state_ffn_conv, "page_table": page_table, "w_in_a": w_in_a, "b_f_a": b_f_a,
            "w_o_a": w_o_a, "g_attn": g_attn, "g_kv": g_kv, "w_kv_b": w_kv_b, "w_q_b": w_q_b, "w_o_b": w_o_b,
            "rel_bias": rel_bias, "g_ffn": g_ffn, "w_up": w_up, "w_gate": w_gate, "w_conv": w_conv,
            "b_conv": b_conv, "w_down": w_down, "g_final": g_final}


def reference(x_prompt, x_sample, cache_fox_k, cache_fox_v, cache_fox_logf, cache_win_k, cache_win_v, state_ffn_conv,
              page_table, w_in_a, b_f_a, w_o_a, g_attn, g_kv, w_kv_b, w_q_b, w_o_b, rel_bias,
              g_ffn, w_up, w_gate, w_conv, b_conv, w_down, g_final):
    (y_prompt, fox_k_prompt, fox_v_prompt, fox_logf_prompt, win_k_prompt, win_v_prompt, conv_prompt) = prompt_forward(
        x_prompt, w_in_a, b_f_a, w_o_a, g_attn, g_kv, w_kv_b, w_q_b, w_o_b, rel_bias,
        g_ffn, w_up, w_gate, w_conv, b_conv, w_down, g_final)
    (y_sample, fox_k_sample, fox_v_sample, fox_logf_sample, win_k_sample, win_v_sample, conv_sample) = sample_forward(
        x_sample, cache_fox_k, cache_fox_v, cache_fox_logf, cache_win_k, cache_win_v, state_ffn_conv, page_table,
        w_in_a, b_f_a, w_o_a, g_attn, g_kv, w_kv_b, w_q_b, w_o_b, rel_bias,
        g_ffn, w_up, w_gate, w_conv, b_conv, w_down, g_final)
    return (y_prompt, y_sample, fox_k_prompt, fox_v_prompt, fox_logf_prompt, fox_k_sample, fox_v_sample, fox_logf_sample,
            win_k_prompt, win_v_prompt, win_k_sample, win_v_sample, conv_prompt, conv_sample)
```

```python
import functools
import math

import numpy as np
import jax
import jax.numpy as jnp
from jax import lax
from jax.experimental import pallas as pl
from jax.experimental.pallas import tpu as pltpu

F32 = jnp.float32
BF16 = jnp.bfloat16

HEAD_DIM = 128
PAGE_SIZE = 128
PATTERNS = ((128, 1), (512, 4), (2048, 16))
N_GROUPS = len(PATTERNS)
N_BUCKETS = 32
BUCKET_MAX_DIST = max(w for w, _ in PATTERNS)
CONV_W = 3
Q_BLOCK = 128
NORM_EPS = 1e-6
ATTN_SCALE = HEAD_DIM ** -0.5
NEG = -1e30
SQRT_HALF = 0.7071067811865476

V7X_VMEM_LIMIT_BYTES = 56 * 1024 * 1024


def _params(*semantics):
    return pltpu.CompilerParams(dimension_semantics=semantics, vmem_limit_bytes=V7X_VMEM_LIMIT_BYTES)


def _rmsnorm_kernel(x_ref, g_ref, *o_refs):
    x = x_ref[...]
    y = x * lax.rsqrt(jnp.mean(x * x, axis=-1, keepdims=True) + NORM_EPS)
    for j, o_ref in enumerate(o_refs):
        o_ref[...] = (y * g_ref[j:j + 1, :]).astype(o_ref.dtype)


def rmsnorm(x, gains, out_dtype):
    m, d = x.shape
    n = gains.shape[0]
    tm = min(256, m)
    return pl.pallas_call(
        _rmsnorm_kernel,
        grid=(m // tm,),
        in_specs=[pl.BlockSpec((tm, d), lambda i: (i, 0)), pl.BlockSpec((n, d), lambda i: (0, 0))],
        out_specs=[pl.BlockSpec((tm, d), lambda i: (i, 0))] * n,
        out_shape=[jax.ShapeDtypeStruct((m, d), out_dtype)] * n,
        compiler_params=_params("parallel"),
        name="rmsnorm",
    )(x, gains)


def _matmul_kernel(*refs, has_res, nk, out_scale):
    x_ref, w_ref = refs[0], refs[1]
    res_ref = refs[2] if has_res else None
    o_ref = refs[2 + has_res]

    def finish(acc):
        if out_scale != 1.0:
            acc = acc * out_scale
        if has_res:
            acc = acc + res_ref[...]
        o_ref[...] = acc.astype(o_ref.dtype)

    part = jnp.dot(x_ref[...], w_ref[...], preferred_element_type=F32)
    if nk == 1:
        finish(part)
        return
    acc_ref = refs[3 + has_res]
    k = pl.program_id(2)

    @pl.when(k == 0)
    def _():
        acc_ref[...] = part

    @pl.when(jnp.logical_and(k > 0, k < nk - 1))
    def _():
        acc_ref[...] += part

    @pl.when(k == nk - 1)
    def _():
        finish(acc_ref[...] + part)


def matmul(x, w, *, res=None, out_dtype=F32, out_scale=1.0, bm=1024, bn=1024, bk=None):
    m, kdim = x.shape
    n = w.shape[1]
    bm, bn = min(bm, m), min(bn, n)
    bk = kdim if bk is None else bk
    nk = kdim // bk
    assert m % bm == 0 and n % bn == 0 and kdim % bk == 0 and nk >= 1
    in_specs = [pl.BlockSpec((bm, bk), lambda i, j, k: (i, k)), pl.BlockSpec((bk, bn), lambda i, j, k: (k, j))]
    args = [x, w]
    if res is not None:
        in_specs.append(pl.BlockSpec((bm, bn), lambda i, j, k: (i, j)))
        args.append(res)
    return pl.pallas_call(
        functools.partial(_matmul_kernel, has_res=res is not None, nk=nk, out_scale=out_scale),
        grid=(m // bm, n // bn, nk),
        in_specs=in_specs,
        out_specs=pl.BlockSpec((bm, bn), lambda i, j, k: (i, j)),
        out_shape=jax.ShapeDtypeStruct((m, n), out_dtype),
        scratch_shapes=[pltpu.VMEM((bm, bn), F32)] if nk > 1 else [],
        compiler_params=_params("parallel", "parallel", "arbitrary"),
        name="matmul",
    )(*args)


def _gate_kernel(x_ref, w_ref, b_ref, o_ref):
    nh = o_ref.shape[1]
    z = jnp.dot(x_ref[...], w_ref[...], preferred_element_type=F32)[:, :nh] + b_ref[...]
    o_ref[...] = -(jnp.maximum(-z, 0.0) + jnp.log1p(jnp.exp(-jnp.abs(z))))


def forget_gate(xn, w_pad, b_f):
    m, d = xn.shape
    nh = b_f.shape[1]
    tm = min(512, m)
    return pl.pallas_call(
        _gate_kernel,
        grid=(m // tm,),
        in_specs=[pl.BlockSpec((tm, d), lambda i: (i, 0)), pl.BlockSpec(w_pad.shape, lambda i: (0, 0)),
                  pl.BlockSpec((1, nh), lambda i: (0, 0))],
        out_specs=pl.BlockSpec((tm, nh), lambda i: (i, 0)),
        out_shape=jax.ShapeDtypeStruct((m, nh), F32),
        compiler_params=_params("parallel"),
        name="forget_gate",
    )(xn, w_pad, b_f)


def _split_dot(tri, x):
    hi = x.astype(BF16)
    r1 = x - hi.astype(F32)
    mid = r1.astype(BF16)
    lo = (r1 - mid.astype(F32)).astype(BF16)
    out = jnp.dot(tri, lo, preferred_element_type=F32)
    out = out + jnp.dot(tri, mid, preferred_element_type=F32)
    return out + jnp.dot(tri, hi, preferred_element_type=F32)


def _cumsum_kernel(x_ref, o_ref, *, chunk):
    s = x_ref.shape[0]
    row = lax.broadcasted_iota(jnp.int32, (chunk, chunk), 0)
    col = lax.broadcasted_iota(jnp.int32, (chunk, chunk), 1)
    tri = jnp.where(col <= row, 1.0, 0.0).astype(BF16)
    carry = jnp.zeros((1, x_ref.shape[1]), F32)
    for i in range(s // chunk):
        y = _split_dot(tri, x_ref[i * chunk:(i + 1) * chunk, :]) + carry
        o_ref[i * chunk:(i + 1) * chunk, :] = y
        carry = y[chunk - 1:chunk, :]


def _cumsum_small_kernel(x_ref, o_ref):
    x = x_ref[...]
    t = x.shape[0]
    row = lax.broadcasted_iota(jnp.int32, x.shape, 0)
    for i in range(t):
        o_ref[i:i + 1, :] = jnp.sum(jnp.where(row <= i, x, 0.0), axis=0, keepdims=True)


def cumsum_rows(x):
    b, s, nh = x.shape
    body = _cumsum_small_kernel if s < 128 else functools.partial(_cumsum_kernel, chunk=min(256, s))
    return pl.pallas_call(
        body,
        grid=(b,),
        in_specs=[pl.BlockSpec((None, s, nh), lambda i: (i, 0, 0))],
        out_specs=pl.BlockSpec((None, s, nh), lambda i: (i, 0, 0)),
        out_shape=jax.ShapeDtypeStruct((b, s, nh), F32),
        compiler_params=_params("parallel"),
        name="cumsum_rows",
    )(x)


def _page_suffix_kernel(pt_ref, lf_ref, o_ref, carry_ref):
    j = pl.program_id(1)
    n = lf_ref.shape[0]

    @pl.when(j == 0)
    def _():
        carry_ref[...] = jnp.zeros_like(carry_ref)

    row = lax.broadcasted_iota(jnp.int32, (n, n), 0)
    col = lax.broadcasted_iota(jnp.int32, (n, n), 1)
    upper = jnp.where(col > row, 1.0, 0.0).astype(BF16)
    x = lf_ref[...]
    o_ref[...] = _split_dot(upper, x) + carry_ref[...]
    carry_ref[...] += jnp.sum(x, axis=0, keepdims=True)


def page_suffix_sums(page_table_flat, lf_pool, b, n_pages):
    _, n, nh = lf_pool.shape
    grid_spec = pltpu.PrefetchScalarGridSpec(
        num_scalar_prefetch=1,
        grid=(b, n_pages),
        in_specs=[pl.BlockSpec((None, n, nh), lambda i, j, pt: (pt[i * n_pages + n_pages - 1 - j], 0, 0))],
        out_specs=pl.BlockSpec((None, None, n, nh), lambda i, j, pt: (i, n_pages - 1 - j, 0, 0)),
        scratch_shapes=[pltpu.VMEM((1, nh), F32)],
    )
    return pl.pallas_call(
        _page_suffix_kernel,
        grid_spec=grid_spec,
        out_shape=jax.ShapeDtypeStruct((b, n_pages, n, nh), F32),
        compiler_params=_params("parallel", "arbitrary"),
        name="page_suffix_sums",
    )(page_table_flat, lf_pool)


def _fox_prompt_kernel(q_ref, k_ref, v_ref, c_ref, ct_ref, o_ref):
    tq = q_ref.shape[0]
    h = pl.program_id(1)
    i = pl.program_id(2)
    q = q_ref[...]
    lane = lax.broadcasted_iota(jnp.int32, c_ref.shape, 1)
    cq = jnp.sum(jnp.where(lane == h, c_ref[...], 0.0), axis=-1, keepdims=True)

    def step(j, carry, masked):
        m, l, acc = carry
        ks = pl.multiple_of(j * tq, tq)
        kb = k_ref[pl.ds(ks, tq), :].astype(BF16)
        vb = v_ref[pl.ds(ks, tq), :].astype(BF16)
        s = lax.dot_general(q, kb, (((1,), (1,)), ((), ())), preferred_element_type=F32)
        s = s + (cq - ct_ref[pl.ds(j, 1), :])
        if masked:
            row = lax.broadcasted_iota(jnp.int32, s.shape, 0)
            col = lax.broadcasted_iota(jnp.int32, s.shape, 1)
            s = jnp.where(col <= row, s, NEG)
        m_new = jnp.maximum(m, jnp.max(s, axis=-1, keepdims=True))
        alpha = jnp.exp(m - m_new)
        p = jnp.exp(s - m_new)
        l = l * alpha + jnp.sum(p, axis=-1, keepdims=True)
        acc = acc * alpha + jnp.dot(p.astype(BF16), vb, preferred_element_type=F32)
        return m_new, l, acc

    init = (jnp.full((tq, 1), NEG, F32), jnp.zeros((tq, 1), F32), jnp.zeros((tq, HEAD_DIM), F32))
    carry = lax.fori_loop(0, i, lambda j, c: step(j, c, False), init)
    _, l, acc = step(i, carry, True)
    o_ref[...] = (acc / l).astype(o_ref.dtype)


def fox_prompt_attention(q, k, v, c, b, s, n_heads, tq=512):
    nq = s // tq
    ct = c.reshape(b, s, n_heads).transpose(0, 2, 1).reshape(b * n_heads, nq, tq)
    return pl.pallas_call(
        _fox_prompt_kernel,
        grid=(b, n_heads, nq),
        in_specs=[
            pl.BlockSpec((tq, HEAD_DIM), lambda bi, h, i: (bi * nq + i, h)),
            pl.BlockSpec((s, HEAD_DIM), lambda bi, h, i: (bi, h)),
            pl.BlockSpec((s, HEAD_DIM), lambda bi, h, i: (bi, h)),
            pl.BlockSpec((tq, n_heads), lambda bi, h, i: (bi * nq + i, 0)),
            pl.BlockSpec((None, nq, tq), lambda bi, h, i: (bi * n_heads + h, 0, 0)),
        ],
        out_specs=pl.BlockSpec((tq, HEAD_DIM), lambda bi, h, i: (bi * nq + i, h)),
        out_shape=jax.ShapeDtypeStruct(q.shape, BF16),
        compiler_params=_params("parallel", "parallel", "arbitrary"),
        name="fox_prompt_attention",
    )(q, k, v, c, ct)


HEADS_PER_TILE = 8


def _fox_sample_kernel(pt_ref, q_ref, cn_ref, cnt_ref, dt_ref, kn_ref, vn_ref, *refs, n_heads, n_pages):
    ng = n_heads // HEADS_PER_TILE
    k_refs, v_refs = refs[:ng], refs[ng:2 * ng]
    o_ref, m_ref, l_ref, acc_ref = refs[2 * ng:]
    j = pl.program_id(1)
    t = q_ref.shape[1]
    rows = k_refs[0].shape[0]

    @pl.when(j == 0)
    def _():
        m_ref[...] = jnp.full_like(m_ref, NEG)
        l_ref[...] = jnp.zeros_like(l_ref)
        acc_ref[...] = jnp.zeros_like(acc_ref)

    def update(h, s, vb):
        m_old = m_ref[h]
        m_new = jnp.maximum(m_old, jnp.max(s, axis=-1, keepdims=True))
        alpha = jnp.exp(m_old - m_new)
        p = jnp.exp(s - m_new)
        l_ref[h] = l_ref[h] * alpha + jnp.sum(p, axis=-1, keepdims=True)
        acc_ref[h] = acc_ref[h] * alpha + jnp.dot(p.astype(vb.dtype), vb, preferred_element_type=F32)
        m_ref[h] = m_new

    for h in range(n_heads):
        g, hs = divmod(h, HEADS_PER_TILE)
        kflat = k_refs[g].reshape(rows * HEADS_PER_TILE, HEAD_DIM)
        vflat = v_refs[g].reshape(rows * HEADS_PER_TILE, HEAD_DIM)
        kh = kflat[pl.ds(hs, rows, stride=HEADS_PER_TILE), :].astype(BF16)
        vh = vflat[pl.ds(hs, rows, stride=HEADS_PER_TILE), :].astype(BF16)
        s = lax.dot_general(q_ref[h], kh, (((1,), (1,)), ((), ())), preferred_element_type=F32)
        s = s + cn_ref[h] + dt_ref[h:h + 1, :]
        update(h, s, vh)

    @pl.when(j == n_pages - 1)
    def _():
        row = lax.broadcasted_iota(jnp.int32, (t, t), 0)
        col = lax.broadcasted_iota(jnp.int32, (t, t), 1)
        for h in range(n_heads):
            sl = slice(h * HEAD_DIM, (h + 1) * HEAD_DIM)
            kh = kn_ref[:, sl].astype(BF16)
            s = lax.dot_general(q_ref[h], kh, (((1,), (1,)), ((), ())), preferred_element_type=F32)
            s = s + cn_ref[h][:, :t] - cnt_ref[h]
            s = jnp.where(col <= row, s, NEG)
            update(h, s, vn_ref[:, sl])
            o_ref[:, sl] = acc_ref[h] / l_ref[h]


def fox_sample_attention(q, c_new, d_past, k_new, v_new, k_pool, v_pool, page_table_flat, b, t, n_heads, n_pages):
    ng = n_heads // HEADS_PER_TILE
    n_pool = k_pool.shape[0]
    q4 = q.reshape(b, t, n_heads, HEAD_DIM).transpose(0, 2, 1, 3)
    cn = jnp.broadcast_to(c_new.transpose(0, 2, 1)[..., None], (b, n_heads, t, PAGE_SIZE))
    cnt = c_new.transpose(0, 2, 1)[:, :, None, :]
    dt = d_past.transpose(0, 1, 3, 2)
    kp = k_pool.reshape(n_pool, PAGE_SIZE, ng, HEADS_PER_TILE, HEAD_DIM)
    vp = v_pool.reshape(n_pool, PAGE_SIZE, ng, HEADS_PER_TILE, HEAD_DIM)

    def page_spec(g):
        return pl.BlockSpec((None, PAGE_SIZE, None, HEADS_PER_TILE, HEAD_DIM),
                            lambda i, j, pt: (pt[i * n_pages + j], 0, g, 0, 0))

    d = n_heads * HEAD_DIM
    grid_spec = pltpu.PrefetchScalarGridSpec(
        num_scalar_prefetch=1,
        grid=(b, n_pages),
        in_specs=[
            pl.BlockSpec((None, n_heads, t, HEAD_DIM), lambda i, j, pt: (i, 0, 0, 0)),
            pl.BlockSpec((None, n_heads, t, PAGE_SIZE), lambda i, j, pt: (i, 0, 0, 0)),
            pl.BlockSpec((None, n_heads, 1, t), lambda i, j, pt: (i, 0, 0, 0)),
            pl.BlockSpec((None, None, n_heads, PAGE_SIZE), lambda i, j, pt: (i, j, 0, 0)),
            pl.BlockSpec((t, d), lambda i, j, pt: (i, 0)),
            pl.BlockSpec((t, d), lambda i, j, pt: (i, 0)),
        ] + [page_spec(g) for g in range(ng)] * 2,
        out_specs=pl.BlockSpec((t, d), lambda i, j, pt: (i, 0)),
        scratch_shapes=[pltpu.VMEM((n_heads, t, 1), F32), pltpu.VMEM((n_heads, t, 1), F32),
                        pltpu.VMEM((n_heads, t, HEAD_DIM), F32)],
    )
    return pl.pallas_call(
        functools.partial(_fox_sample_kernel, n_heads=n_heads, n_pages=n_pages),
        grid_spec=grid_spec,
        out_shape=jax.ShapeDtypeStruct((b * t, d), F32),
        compiler_params=_params("parallel", "arbitrary"),
        name="fox_sample_attention",
    )(page_table_flat, q4, cn, cnt, dt, k_new, v_new, *([kp] * ng), *([vp] * ng))


def _t5_bucket(dist):
    n = np.asarray(dist, dtype=np.int64)
    exact = N_BUCKETS // 2
    large = exact + (np.log(np.maximum(n, 1) / exact) / np.log(BUCKET_MAX_DIST / exact) * (N_BUCKETS - exact)).astype(np.int64)
    return np.where(n < exact, n, np.minimum(large, N_BUCKETS - 1)).astype(np.int32)


def _band_bias(rel_bias, group_heads):
    a = np.arange(Q_BLOCK)[:, None]
    bcol = np.arange(2 * Q_BLOCK)[None, :]
    j = a - bcol + Q_BLOCK
    out = []
    for g, (window, dil) in enumerate(PATTERNS):
        assert window // dil == Q_BLOCK
        valid = (j >= 0) & (j <= window // dil)
        bucket = _t5_bucket(dil * np.clip(j, 0, window // dil))
        rb = rel_bias[:, g * group_heads:(g + 1) * group_heads].astype(F32)
        bias = jnp.transpose(rb[bucket], (2, 0, 1))
        out.append(jnp.where(valid[None], bias, NEG))
    return jnp.stack(out)


def _dilated_prompt_kernel(*refs, s):
    n_q = N_GROUPS * 2
    q_refs = refs[:n_q]
    k_ref, v_ref, bias_ref, o_ref, m_ref, l_ref, acc_ref = refs[n_q:]
    qb_rows = Q_BLOCK

    def block(g, q_idx, k_idx, first, merge):
        qs = [q_refs[2 * g + r][q_idx, :].astype(BF16) for r in range(2)]
        q2 = jnp.concatenate(qs, axis=0)
        kk = k_ref[k_idx, :].astype(BF16)
        vv = v_ref[k_idx, :].astype(BF16)
        sc = lax.dot_general(q2, kk, (((1,), (1,)), ((), ())), preferred_element_type=F32)
        bias = bias_ref[g].reshape(2 * qb_rows, 2 * qb_rows)
        sc = sc + (bias[:, qb_rows:] if first else bias)
        m_blk = jnp.max(sc, axis=-1, keepdims=True)
        p = jnp.exp(sc - m_blk)
        l_blk = jnp.sum(p, axis=-1, keepdims=True)
        o_blk = jnp.dot(p.astype(BF16), vv, preferred_element_type=F32)
        for r in range(2):
            rs = slice(r * qb_rows, (r + 1) * qb_rows)
            m_b = jnp.broadcast_to(m_blk[rs], (qb_rows, HEAD_DIM))
            l_b = jnp.broadcast_to(l_blk[rs], (qb_rows, HEAD_DIM))
            if merge:
                m_old = m_ref[r, q_idx, :]
                m_new = jnp.maximum(m_old, m_b)
                a_old = jnp.exp(m_old - m_new)
                a_blk = jnp.exp(m_b - m_new)
                l_ref[r, q_idx, :] = l_ref[r, q_idx, :] * a_old + l_b * a_blk
                acc_ref[r, q_idx, :] = acc_ref[r, q_idx, :] * a_old + o_blk[rs] * a_blk
                m_ref[r, q_idx, :] = m_new
            else:
                m_ref[r, q_idx, :] = m_b
                l_ref[r, q_idx, :] = l_b
                acc_ref[r, q_idx, :] = o_blk[rs]

    for g, (_, dil) in enumerate(PATTERNS):
        n_blocks = s // (dil * qb_rows)
        merge = g > 0

        def run_class(c, g=g, dil=dil, n_blocks=n_blocks, merge=merge):
            def idx(start, size):
                if dil == 1:
                    return pl.ds(start, size)
                return pl.ds(start, size, stride=dil)

            block(g, idx(c, qb_rows), idx(c, qb_rows), True, merge)

            def later(qb, carry):
                q0 = c + dil * qb_rows * qb
                block(g, idx(q0, qb_rows), idx(q0 - dil * qb_rows, 2 * qb_rows), False, merge)
                return carry

            if n_blocks > 1:
                lax.fori_loop(1, n_blocks, later, 0)

        if dil == 1:
            run_class(0)
        else:
            lax.fori_loop(0, dil, lambda c, carry: (run_class(c), carry)[1], 0)

    for r in range(2):
        o_ref[:, r * HEAD_DIM:(r + 1) * HEAD_DIM] = (acc_ref[r] / l_ref[r]).astype(o_ref.dtype)


def dilated_prompt_attention(q, k, v, rel_bias, b, s, n_kv, group_heads):
    assert group_heads == 2 * n_kv
    bias = _band_bias(rel_bias, group_heads).reshape(N_GROUPS, n_kv, 2, Q_BLOCK, 2 * Q_BLOCK)

    def q_spec(g, r):
        return pl.BlockSpec((s, HEAD_DIM), lambda bi, kv: (bi, g * group_heads + 2 * kv + r))

    return pl.pallas_call(
        functools.partial(_dilated_prompt_kernel, s=s),
        grid=(b, n_kv),
        in_specs=[q_spec(g, r) for g in range(N_GROUPS) for r in range(2)] + [
            pl.BlockSpec((s, HEAD_DIM), lambda bi, kv: (bi, kv)),
            pl.BlockSpec((s, HEAD_DIM), lambda bi, kv: (bi, kv)),
            pl.BlockSpec((N_GROUPS, None, 2, Q_BLOCK, 2 * Q_BLOCK), lambda bi, kv: (0, kv, 0, 0, 0)),
        ],
        out_specs=pl.BlockSpec((s, 2 * HEAD_DIM), lambda bi, kv: (bi, kv)),
        out_shape=jax.ShapeDtypeStruct((b * s, group_heads * HEAD_DIM), BF16),
        scratch_shapes=[pltpu.VMEM((2, s, HEAD_DIM), F32)] * 3,
        compiler_params=_params("parallel", "parallel"),
        name="dilated_prompt_attention",
    )(*([q] * (N_GROUPS * 2)), k, v, bias)


def _sample_bias(rel_bias, group_heads, n_kv, t, n_buf):
    dist = n_buf + np.arange(t)[:, None] - np.arange(n_buf + t)[None, :]
    out = []
    for g, (window, dil) in enumerate(PATTERNS):
        valid = (dist >= 0) & (dist % dil == 0) & (dist <= window)
        bucket = _t5_bucket(np.clip(dist, 0, window))
        rb = rel_bias[:, g * group_heads:(g + 1) * group_heads].astype(F32)
        bias = jnp.transpose(rb[bucket], (2, 0, 1))
        out.append(jnp.where(valid[None], bias, NEG).reshape(n_kv, 2, t, n_buf + t))
    return jnp.stack(out, axis=1).reshape(n_kv, N_GROUPS * 2 * t, n_buf + t)


def _dilated_sample_kernel(q_ref, kc_ref, vc_ref, kn_ref, vn_ref, bc_ref, bn_ref, o_ref, *, n_kv, group_heads):
    t = q_ref.shape[0]
    n_buf = kc_ref.shape[0]
    kflat = kc_ref.reshape(n_buf * n_kv, HEAD_DIM)
    vflat = vc_ref.reshape(n_buf * n_kv, HEAD_DIM)
    for kv in range(n_kv):
        heads = [g * group_heads + 2 * kv + r for g in range(N_GROUPS) for r in range(2)]
        q6 = jnp.concatenate([q_ref[:, h * HEAD_DIM:(h + 1) * HEAD_DIM] for h in heads], axis=0).astype(BF16)
        kc = kflat[pl.ds(kv, n_buf, stride=n_kv), :].astype(BF16)
        vc = vflat[pl.ds(kv, n_buf, stride=n_kv), :].astype(BF16)
        sl = slice(kv * HEAD_DIM, (kv + 1) * HEAD_DIM)
        kn = kn_ref[:, sl].astype(BF16)
        vn = vn_ref[:, sl]
        dn = (((1,), (1,)), ((), ()))
        s_c = lax.dot_general(q6, kc, dn, preferred_element_type=F32) + bc_ref[kv]
        s_n = lax.dot_general(q6, kn, dn, preferred_element_type=F32) + bn_ref[kv]
        m_row = jnp.maximum(jnp.max(s_c, axis=-1, keepdims=True), jnp.max(s_n, axis=-1, keepdims=True))
        m = jnp.max(m_row.reshape(N_GROUPS, 2 * t, 1), axis=0)
        m_all = jnp.concatenate([m] * N_GROUPS, axis=0)
        p_c = jnp.exp(s_c - m_all)
        p_n = jnp.exp(s_n - m_all)
        l_row = jnp.sum(p_c, axis=-1, keepdims=True) + jnp.sum(p_n, axis=-1, keepdims=True)
        o_row = jnp.dot(p_c.astype(BF16), vc, preferred_element_type=F32) + jnp.dot(p_n, vn, preferred_element_type=F32)
        l = jnp.sum(l_row.reshape(N_GROUPS, 2 * t, 1), axis=0)
        o = jnp.sum(o_row.reshape(N_GROUPS, 2 * t, HEAD_DIM), axis=0) / l
        for r in range(2):
            o_ref[:, (2 * kv + r) * HEAD_DIM:(2 * kv + r + 1) * HEAD_DIM] = o[r * t:(r + 1) * t]


def dilated_sample_attention(q, k_new, v_new, cache_k, cache_v, rel_bias, b, t, n_kv, group_heads):
    n_buf = cache_k.shape[1]
    bias = _sample_bias(rel_bias, group_heads, n_kv, t, n_buf)
    bias_c, bias_n = bias[:, :, :n_buf], bias[:, :, n_buf:]
    dq = q.shape[1]
    dkv = n_kv * HEAD_DIM
    return pl.pallas_call(
        functools.partial(_dilated_sample_kernel, n_kv=n_kv, group_heads=group_heads),
        grid=(b,),
        in_specs=[
            pl.BlockSpec((t, dq), lambda i: (i, 0)),
            pl.BlockSpec((None, n_buf, n_kv, HEAD_DIM), lambda i: (i, 0, 0, 0)),
            pl.BlockSpec((None, n_buf, n_kv, HEAD_DIM), lambda i: (i, 0, 0, 0)),
            pl.BlockSpec((t, dkv), lambda i: (i, 0)),
            pl.BlockSpec((t, dkv), lambda i: (i, 0)),
            pl.BlockSpec(bias_c.shape, lambda i: (0, 0, 0)),
            pl.BlockSpec(bias_n.shape, lambda i: (0, 0, 0)),
        ],
        out_specs=pl.BlockSpec((t, group_heads * HEAD_DIM), lambda i: (i, 0)),
        out_shape=jax.ShapeDtypeStruct((b * t, group_heads * HEAD_DIM), F32),
        compiler_params=_params("parallel"),
        name="dilated_sample_attention",
    )(q, cache_k, cache_v, k_new, v_new, bias_c, bias_n)


CONV_PAD_ROWS = 8


def _upgate_kernel(x_ref, wu_ref, wg_ref, prev_ref, wc_ref, bc_ref, h_ref, st_ref, pad_ref, *, n_seq, t):
    x = x_ref[...]
    u = jnp.dot(x, wu_ref[...], preferred_element_type=F32)
    gate = jnp.dot(x, wg_ref[...], preferred_element_type=F32)
    lo = CONV_PAD_ROWS - (CONV_W - 1)
    pieces = []
    for sq in range(n_seq):
        rs = slice(sq * t, (sq + 1) * t)
        pad_ref[pl.ds(CONV_PAD_ROWS, t), :] = u[rs]
        pad_ref[pl.ds(lo, CONV_W - 1), :] = prev_ref[sq]
        uc = bc_ref[...] + pad_ref[pl.ds(lo, t), :] * wc_ref[0:1, :]
        for i in range(1, CONV_W):
            uc = uc + pad_ref[pl.ds(lo + i, t), :] * wc_ref[i:i + 1, :]
        pieces.append(0.5 * uc * (1.0 + lax.erf(uc * SQRT_HALF)) * gate[rs])
        st_ref[sq] = pad_ref[pl.ds(lo + t, CONV_W - 1), :]
    hh = pieces[0] if n_seq == 1 else jnp.concatenate(pieces, axis=0)
    h_ref[...] = hh.astype(h_ref.dtype)


def conv_ffn_hidden(xn, w_up, w_gate, prev, w_conv, b_conv, n_seq, t, seq_per_tile, tn=256):
    m, d = xn.shape
    f = w_up.shape[1]
    tm = seq_per_tile * t
    return pl.pallas_call(
        functools.partial(_upgate_kernel, n_seq=seq_per_tile, t=t),
        grid=(n_seq // seq_per_tile, f // tn),
        in_specs=[
            pl.BlockSpec((tm, d), lambda i, j: (i, 0)),
            pl.BlockSpec((d, tn), lambda i, j: (0, j)),
            pl.BlockSpec((d, tn), lambda i, j: (0, j)),
            pl.BlockSpec((seq_per_tile, CONV_W - 1, tn), lambda i, j: (i, 0, j)),
            pl.BlockSpec((CONV_W, tn), lambda i, j: (0, j)),
            pl.BlockSpec((1, tn), lambda i, j: (0, j)),
        ],
        out_specs=[
            pl.BlockSpec((tm, tn), lambda i, j: (i, j)),
            pl.BlockSpec((seq_per_tile, CONV_W - 1, tn), lambda i, j: (i, 0, j)),
        ],
        out_shape=[jax.ShapeDtypeStruct((m, f), BF16), jax.ShapeDtypeStruct((n_seq, CONV_W - 1, f), F32)],
        scratch_shapes=[pltpu.VMEM((t + CONV_PAD_ROWS, tn), F32)],
        compiler_params=_params("parallel", "parallel"),
        name="conv_ffn_hidden",
    )(xn, w_up, w_gate, prev, w_conv, b_conv)


def _forward(x, n_seq, t, weights, sample_state):
    (w_q_a, w_k_a, w_v_a, w_f_a, b_f_a, w_o_a, g_attn, g_kv, w_kv_b, w_q_b, w_o_b, rel_bias,
     g_ffn, w_up, w_gate, w_conv, b_conv, w_down, g_final) = weights
    m, d_model = x.shape
    n_heads = w_q_a.shape[1] // HEAD_DIM
    n_kv = w_kv_b.shape[1] // (2 * HEAD_DIM)
    group_heads = w_q_b.shape[1] // (N_GROUPS * HEAD_DIM)
    d_ff = w_up.shape[2]
    is_prompt = sample_state is None
    big = dict(bm=1024, bn=1024) if is_prompt else dict(bm=m, bn=1024)
    big_res = dict(bm=1024, bn=512) if is_prompt else dict(bm=m, bn=1024)
    out = {}

    def ffn(x, layer, prev):
        (xn,) = rmsnorm(x, g_ffn[layer][None], BF16)
        spt = 1 if is_prompt else n_seq
        h, st = conv_ffn_hidden(xn, w_up[layer], w_gate[layer], prev, w_conv[layer], b_conv[layer][None],
                                n_seq, t, spt)
        return matmul(h, w_down[layer], res=x, bk=d_ff // 2, **big_res), st

    (xn,) = rmsnorm(x, g_attn[0][None], BF16)
    q = matmul(xn, w_q_a, out_dtype=BF16, out_scale=ATTN_SCALE, **big)
    k = matmul(xn, w_k_a, **big)
    v = matmul(xn, w_v_a, **big)
    logf = forget_gate(xn, w_f_a, b_f_a)
    c = cumsum_rows(logf.reshape(n_seq, t, n_heads))
    if is_prompt:
        a = fox_prompt_attention(q, k, v, c.reshape(m, n_heads), n_seq, t, n_heads)
        prev0 = prev1 = jnp.zeros((n_seq, CONV_W - 1, d_ff), F32)
    else:
        pool_k, pool_v, pool_lf, win_k, win_v, conv_state, page_table = sample_state
        n_pages = page_table.shape[1]
        pt_flat = page_table.reshape(-1)
        d_past = page_suffix_sums(pt_flat, pool_lf, n_seq, n_pages)
        a = fox_sample_attention(q, c, d_past, k, v, pool_k, pool_v, pt_flat, n_seq, t, n_heads, n_pages)
        a = a.astype(BF16)
        prev0, prev1 = conv_state[0], conv_state[1]
    out["fox_k"], out["fox_v"], out["fox_lf"] = k, v, logf
    x = matmul(a, w_o_a, res=x, **big_res)
    x, st0 = ffn(x, 0, prev0)

    xn, xkv = rmsnorm(x, jnp.stack([g_attn[1], g_kv]), BF16)
    kv = matmul(xkv, w_kv_b, **big)
    k_b, v_b = kv[:, :n_kv * HEAD_DIM], kv[:, n_kv * HEAD_DIM:]
    q = matmul(xn, w_q_b, out_scale=ATTN_SCALE, **big)
    if is_prompt:
        a = dilated_prompt_attention(q, k_b, v_b, rel_bias, n_seq, t, n_kv, group_heads)
    else:
        a = dilated_sample_attention(q, k_b, v_b, win_k, win_v, rel_bias, n_seq, t, n_kv, group_heads)
        a = a.astype(BF16)
    out["win_k"], out["win_v"] = k_b, v_b
    x = matmul(a, w_o_b, res=x, **big_res)
    x, st1 = ffn(x, 1, prev1)
    (y,) = rmsnorm(x, g_final[None], F32)
    out["y"] = y
    out["conv"] = jnp.stack([st0, st1])
    return out


def kernel(x_prompt, x_sample, cache_fox_k, cache_fox_v, cache_fox_logf, cache_win_k, cache_win_v, state_ffn_conv,
           page_table, w_in_a, b_f_a, w_o_a, g_attn, g_kv, w_kv_b, w_q_b, w_o_b, rel_bias,
           g_ffn, w_up, w_gate, w_conv, b_conv, w_down, g_final):
    bp, sp, d_model = x_prompt.shape
    bs, ts, _ = x_sample.shape
    n_heads = b_f_a.shape[1]
    hd = n_heads * HEAD_DIM
    n_kv = cache_win_k.shape[2]
    assert w_in_a.shape[0] == 1 and w_q_b.shape[0] == 1 and g_attn.shape[0] == 2

    w_in = w_in_a[0]
    w_f = jnp.pad(w_in[:, 3 * hd:], ((0, 0), (0, HEAD_DIM - n_heads))).astype(BF16)
    weights = (
        w_in[:, :hd].astype(BF16), w_in[:, hd:2 * hd].astype(BF16), w_in[:, 2 * hd:3 * hd].astype(BF16), w_f,
        b_f_a[0][None].astype(F32), w_o_a[0].astype(BF16), g_attn, g_kv, w_kv_b.astype(BF16), w_q_b[0].astype(BF16),
        w_o_b[0].astype(BF16), rel_bias, g_ffn, w_up.astype(BF16), w_gate.astype(BF16), w_conv, b_conv,
        w_down.astype(BF16), g_final)

    p = _forward(x_prompt.reshape(bp * sp, d_model), bp, sp, weights, None)
    sample_state = (cache_fox_k[0], cache_fox_v[0], cache_fox_logf[0], cache_win_k, cache_win_v, state_ffn_conv,
                    page_table)
    s = _forward(x_sample.reshape(bs * ts, d_model), bs, ts, weights, sample_state)

    n_pg = sp // PAGE_SIZE
    n_buf = cache_win_k.shape[1]
    win_k_new = s["win_k"].reshape(bs, ts, n_kv, HEAD_DIM)
    win_v_new = s["win_v"].reshape(bs, ts, n_kv, HEAD_DIM)
    return (
        p["y"].reshape(bp, sp, d_model),
        s["y"].reshape(bs, ts, d_model),
        p["fox_k"].reshape(1, bp, n_pg, PAGE_SIZE, n_heads, HEAD_DIM),
        p["fox_v"].reshape(1, bp, n_pg, PAGE_SIZE, n_heads, HEAD_DIM),
        p["fox_lf"].reshape(1, bp, n_pg, PAGE_SIZE, n_heads),
        s["fox_k"].reshape(1, bs, ts, n_heads, HEAD_DIM),
        s["fox_v"].reshape(1, bs, ts, n_heads, HEAD_DIM),
        s["fox_lf"].reshape(1, bs, ts, n_heads),
        p["win_k"].reshape(bp, sp, n_kv, HEAD_DIM)[:, sp - min(BUCKET_MAX_DIST, sp):],
        p["win_v"].reshape(bp, sp, n_kv, HEAD_DIM)[:, sp - min(BUCKET_MAX_DIST, sp):],
        jnp.concatenate([cache_win_k[:, ts:], win_k_new], axis=1)[:, -n_buf:],
        jnp.concatenate([cache_win_v[:, ts:], win_v_new], axis=1)[:, -n_buf:],
        p["conv"],
        s["conv"],
    )
```

```python
import functools
import math

import numpy as np
import jax
import jax.numpy as jnp
from jax import lax
from jax.experimental import pallas as pl
from jax.experimental.pallas import tpu as pltpu

F32 = jnp.float32
BF16 = jnp.bfloat16

HEAD_DIM = 128
PAGE_SIZE = 128
PATTERNS = ((128, 1), (512, 4), (2048, 16))
N_GROUPS = len(PATTERNS)
N_BUCKETS = 32
BUCKET_MAX_DIST = max(w for w, _ in PATTERNS)
CONV_W = 3
Q_BLOCK = 128
NORM_EPS = 1e-6
ATTN_SCALE = HEAD_DIM ** -0.5
NEG = -1e30
SQRT_HALF = 0.7071067811865476

V7X_VMEM_LIMIT_BYTES = 56 * 1024 * 1024


def _params(*semantics):
    return pltpu.CompilerParams(dimension_semantics=semantics, vmem_limit_bytes=V7X_VMEM_LIMIT_BYTES)


def _rmsnorm_kernel(x_ref, g_ref, *o_refs):
    x = x_ref[...]
    y = x * lax.rsqrt(jnp.mean(x * x, axis=-1, keepdims=True) + NORM_EPS)
    for j, o_ref in enumerate(o_refs):
        o_ref[...] = (y * g_ref[j:j + 1, :]).astype(o_ref.dtype)


def rmsnorm(x, gains, out_dtype):
    m, d = x.shape
    n = gains.shape[0]
    tm = min(256, m)
    return pl.pallas_call(
        _rmsnorm_kernel,
        grid=(m // tm,),
        in_specs=[pl.BlockSpec((tm, d), lambda i: (i, 0)), pl.BlockSpec((n, d), lambda i: (0, 0))],
        out_specs=[pl.BlockSpec((tm, d), lambda i: (i, 0))] * n,
        out_shape=[jax.ShapeDtypeStruct((m, d), out_dtype)] * n,
        compiler_params=_params("parallel"),
        name="rmsnorm",
    )(x, gains)


def _matmul_kernel(*refs, has_res, nk, out_scale):
    x_ref, w_ref = refs[0], refs[1]
    res_ref = refs[2] if has_res else None
    o_ref = refs[2 + has_res]

    def finish(acc):
        if out_scale != 1.0:
            acc = acc * out_scale
        if has_res:
            acc = acc + res_ref[...]
        o_ref[...] = acc.astype(o_ref.dtype)

    part = jnp.dot(x_ref[...], w_ref[...], preferred_element_type=F32)
    if nk == 1:
        finish(part)
        return
    acc_ref = refs[3 + has_res]
    k = pl.program_id(2)

    @pl.when(k == 0)
    def _():
        acc_ref[...] = part

    @pl.when(jnp.logical_and(k > 0, k < nk - 1))
    def _():
        acc_ref[...] += part

    @pl.when(k == nk - 1)
    def _():
        finish(acc_ref[...] + part)


def matmul(x, w, *, res=None, out_dtype=F32, out_scale=1.0, bm=1024, bn=1024, bk=None):
    m, kdim = x.shape
    n = w.shape[1]
    bm, bn = min(bm, m), min(bn, n)
    bk = kdim if bk is None else bk
    nk = kdim // bk
    assert m % bm == 0 and n % bn == 0 and kdim % bk == 0 and nk >= 1
    in_specs = [pl.BlockSpec((bm, bk), lambda i, j, k: (i, k)), pl.BlockSpec((bk, bn), lambda i, j, k: (k, j))]
    args = [x, w]
    if res is not None:
        in_specs.append(pl.BlockSpec((bm, bn), lambda i, j, k: (i, j)))
        args.append(res)
    return pl.pallas_call(
        functools.partial(_matmul_kernel, has_res=res is not None, nk=nk, out_scale=out_scale),
        grid=(m // bm, n // bn, nk),
        in_specs=in_specs,
        out_specs=pl.BlockSpec((bm, bn), lambda i, j, k: (i, j)),
        out_shape=jax.ShapeDtypeStruct((m, n), out_dtype),
        scratch_shapes=[pltpu.VMEM((bm, bn), F32)] if nk > 1 else [],
        compiler_params=_params("parallel", "parallel", "arbitrary"),
        name="matmul",
    )(*args)


def _gate_kernel(x_ref, w_ref, b_ref, o_ref):
    nh = o_ref.shape[1]
    z = jnp.dot(x_ref[...], w_ref[...], preferred_element_type=F32)[:, :nh] + b_ref[...]
    o_ref[...] = -(jnp.maximum(-z, 0.0) + jnp.log1p(jnp.exp(-jnp.abs(z))))


def forget_gate(xn, w_pad, b_f):
    m, d = xn.shape
    nh = b_f.shape[1]
    tm = min(512, m)
    return pl.pallas_call(
        _gate_kernel,
        grid=(m // tm,),
        in_specs=[pl.BlockSpec((tm, d), lambda i: (i, 0)), pl.BlockSpec(w_pad.shape, lambda i: (0, 0)),
                  pl.BlockSpec((1, nh), lambda i: (0, 0))],
        out_specs=pl.BlockSpec((tm, nh), lambda i: (i, 0)),
        out_shape=jax.ShapeDtypeStruct((m, nh), F32),
        compiler_params=_params("parallel"),
        name="forget_gate",
    )(xn, w_pad, b_f)


def _split_dot(tri, x):
    hi = x.astype(BF16)
    r1 = x - hi.astype(F32)
    mid = r1.astype(BF16)
    lo = (r1 - mid.astype(F32)).astype(BF16)
    out = jnp.dot(tri, lo, preferred_element_type=F32)
    out = out + jnp.dot(tri, mid, preferred_element_type=F32)
    return out + jnp.dot(tri, hi, preferred_element_type=F32)


def _cumsum_kernel(x_ref, o_ref, *, chunk):
    s = x_ref.shape[0]
    row = lax.broadcasted_iota(jnp.int32, (chunk, chunk), 0)
    col = lax.broadcasted_iota(jnp.int32, (chunk, chunk), 1)
    tri = jnp.where(col <= row, 1.0, 0.0).astype(BF16)
    carry = jnp.zeros((1, x_ref.shape[1]), F32)
    for i in range(s // chunk):
        y = _split_dot(tri, x_ref[i * chunk:(i + 1) * chunk, :]) + carry
        o_ref[i * chunk:(i + 1) * chunk, :] = y
        carry = y[chunk - 1:chunk, :]


def _cumsum_small_kernel(x_ref, o_ref):
    x = x_ref[...]
    t = x.shape[0]
    row = lax.broadcasted_iota(jnp.int32, x.shape, 0)
    for i in range(t):
        o_ref[i:i + 1, :] = jnp.sum(jnp.where(row <= i, x, 0.0), axis=0, keepdims=True)


def cumsum_rows(x):
    b, s, nh = x.shape
    body = _cumsum_small_kernel if s < 128 else functools.partial(_cumsum_kernel, chunk=min(256, s))
    return pl.pallas_call(
        body,
        grid=(b,),
        in_specs=[pl.BlockSpec((None, s, nh), lambda i: (i, 0, 0))],
        out_specs=pl.BlockSpec((None, s, nh), lambda i: (i, 0, 0)),
        out_shape=jax.ShapeDtypeStruct((b, s, nh), F32),
        compiler_params=_params("parallel"),
        name="cumsum_rows",
    )(x)


SUFFIX_PAGES_PER_STEP = 16


def _page_suffix_kernel(pt_ref, *refs, pps):
    lf_refs, o_ref, carry_ref = refs[:pps], refs[pps], refs[pps + 1]
    j = pl.program_id(1)
    n = lf_refs[0].shape[0]

    @pl.when(j == 0)
    def _():
        carry_ref[...] = jnp.zeros_like(carry_ref)

    row = lax.broadcasted_iota(jnp.int32, (n, n), 0)
    col = lax.broadcasted_iota(jnp.int32, (n, n), 1)
    upper = jnp.where(col > row, 1.0, 0.0).astype(BF16)
    carry = carry_ref[...]
    for pp in reversed(range(pps)):
        x = lf_refs[pp][...]
        o_ref[pp] = _split_dot(upper, x) + carry
        carry = carry + jnp.sum(x, axis=0, keepdims=True)
    carry_ref[...] = carry


def page_suffix_sums(page_table_flat, lf_pool, b, n_pages):
    _, n, nh = lf_pool.shape
    pps = SUFFIX_PAGES_PER_STEP
    assert n_pages % pps == 0
    n_steps = n_pages // pps

    def page_spec(pp):
        return pl.BlockSpec((None, n, nh), lambda i, j, pt: (pt[i * n_pages + n_pages - (j + 1) * pps + pp], 0, 0))

    grid_spec = pltpu.PrefetchScalarGridSpec(
        num_scalar_prefetch=1,
        grid=(b, n_steps),
        in_specs=[page_spec(pp) for pp in range(pps)],
        out_specs=pl.BlockSpec((None, pps, n, nh), lambda i, j, pt: (i, n_steps - 1 - j, 0, 0)),
        scratch_shapes=[pltpu.VMEM((1, nh), F32)],
    )
    return pl.pallas_call(
        functools.partial(_page_suffix_kernel, pps=pps),
        grid_spec=grid_spec,
        out_shape=jax.ShapeDtypeStruct((b, n_pages, n, nh), F32),
        compiler_params=_params("parallel", "arbitrary"),
        name="page_suffix_sums",
    )(page_table_flat, *([lf_pool] * pps))


def _fox_prompt_kernel(q_ref, k_ref, v_ref, c_ref, ct_ref, o_ref):
    tq = q_ref.shape[0]
    h = pl.program_id(1)
    i = pl.program_id(2)
    q = q_ref[...]
    lane = lax.broadcasted_iota(jnp.int32, c_ref.shape, 1)
    cq = jnp.sum(jnp.where(lane == h, c_ref[...], 0.0), axis=-1, keepdims=True)

    def step(j, carry, masked):
        m, l, acc = carry
        ks = pl.multiple_of(j * tq, tq)
        kb = k_ref[pl.ds(ks, tq), :].astype(BF16)
        vb = v_ref[pl.ds(ks, tq), :].astype(BF16)
        s = lax.dot_general(q, kb, (((1,), (1,)), ((), ())), preferred_element_type=F32)
        s = s + (cq - ct_ref[pl.ds(j, 1), :])
        if masked:
            row = lax.broadcasted_iota(jnp.int32, s.shape, 0)
            col = lax.broadcasted_iota(jnp.int32, s.shape, 1)
            s = jnp.where(col <= row, s, NEG)
        m_new = jnp.maximum(m, jnp.max(s, axis=-1, keepdims=True))
        alpha = jnp.exp(m - m_new)
        p = jnp.exp(s - m_new)
        l = l * alpha + jnp.sum(p, axis=-1, keepdims=True)
        acc = acc * alpha + jnp.dot(p.astype(BF16), vb, preferred_element_type=F32)
        return m_new, l, acc

    init = (jnp.full((tq, 1), NEG, F32), jnp.zeros((tq, 1), F32), jnp.zeros((tq, HEAD_DIM), F32))
    carry = lax.fori_loop(0, i, lambda j, c: step(j, c, False), init)
    _, l, acc = step(i, carry, True)
    o_ref[...] = (acc / l).astype(o_ref.dtype)


def fox_prompt_attention(q, k, v, c, b, s, n_heads, tq=512):
    nq = s // tq
    ct = c.reshape(b, s, n_heads).transpose(0, 2, 1).reshape(b * n_heads, nq, tq)
    return pl.pallas_call(
        _fox_prompt_kernel,
        grid=(b, n_heads, nq),
        in_specs=[
            pl.BlockSpec((tq, HEAD_DIM), lambda bi, h, i: (bi * nq + i, h)),
            pl.BlockSpec((s, HEAD_DIM), lambda bi, h, i: (bi, h)),
            pl.BlockSpec((s, HEAD_DIM), lambda bi, h, i: (bi, h)),
            pl.BlockSpec((tq, n_heads), lambda bi, h, i: (bi * nq + i, 0)),
            pl.BlockSpec((None, nq, tq), lambda bi, h, i: (bi * n_heads + h, 0, 0)),
        ],
        out_specs=pl.BlockSpec((tq, HEAD_DIM), lambda bi, h, i: (bi * nq + i, h)),
        out_shape=jax.ShapeDtypeStruct(q.shape, BF16),
        compiler_params=_params("parallel", "parallel", "arbitrary"),
        name="fox_prompt_attention",
    )(q, k, v, c, ct)


HEADS_PER_TILE = 8


FOX_PAGES_PER_STEP = 2


def _tile_rows(ref, sub):
    rows, n_sub, dh = ref.shape
    return ref.reshape(rows * n_sub, dh)[pl.ds(sub, rows, stride=n_sub), :]


def _fox_sample_kernel(pt_ref, q_ref, cn_ref, cnt_ref, dt_ref, kn_ref, vn_ref, *refs, n_heads, n_steps, pps):
    ng = n_heads // HEADS_PER_TILE
    nb = ng * pps
    k_refs, v_refs = refs[:nb], refs[nb:2 * nb]
    o_ref, m_ref, l_ref, acc_ref = refs[2 * nb:]
    j = pl.program_id(1)
    t = q_ref.shape[1]
    dn = (((1,), (1,)), ((), ()))

    @pl.when(j == 0)
    def _():
        m_ref[...] = jnp.full_like(m_ref, NEG)
        l_ref[...] = jnp.zeros_like(l_ref)
        acc_ref[...] = jnp.zeros_like(acc_ref)

    def head_rows(page_refs, h):
        g, hs = divmod(h, HEADS_PER_TILE)
        tiles = [_tile_rows(page_refs[pp * ng + g], hs).astype(BF16) for pp in range(pps)]
        return tiles[0] if pps == 1 else jnp.concatenate(tiles, axis=0)

    def online_update(s, pv):
        m_old = m_ref[...]
        m_new = jnp.maximum(m_old, jnp.max(s, axis=-1, keepdims=True))
        alpha = jnp.exp(m_old - m_new)
        p = jnp.exp(s - m_new)
        l_ref[...] = l_ref[...] * alpha + jnp.sum(p, axis=-1, keepdims=True)
        acc_ref[...] = acc_ref[...] * alpha + pv(p)
        m_ref[...] = m_new

    parts = []
    for h in range(n_heads):
        sh = lax.dot_general(q_ref[h], head_rows(k_refs, h), dn, preferred_element_type=F32)
        d_row = [dt_ref[pp, h:h + 1, :] for pp in range(pps)]
        parts.append(sh + (d_row[0] if pps == 1 else jnp.concatenate(d_row, axis=1)))
    cn = cn_ref[...]
    s = jnp.concatenate(parts, axis=0) + (cn if pps == 1 else jnp.concatenate([cn] * pps, axis=1))

    def pv_pages(p):
        return jnp.concatenate(
            [jnp.dot(p[h * t:(h + 1) * t].astype(BF16), head_rows(v_refs, h), preferred_element_type=F32)
             for h in range(n_heads)], axis=0)

    online_update(s, pv_pages)

    @pl.when(j == n_steps - 1)
    def _():
        parts = []
        for h in range(n_heads):
            kh = kn_ref[:, h * HEAD_DIM:(h + 1) * HEAD_DIM].astype(BF16)
            parts.append(lax.dot_general(q_ref[h], kh, dn, preferred_element_type=F32))
        s2 = jnp.concatenate(parts, axis=0) + cn_ref[:, :t] - cnt_ref[...]
        row = lax.broadcasted_iota(jnp.int32, s2.shape, 0)
        col = lax.broadcasted_iota(jnp.int32, s2.shape, 1)
        s2 = jnp.where(col <= lax.rem(row, t), s2, NEG)

        def pv_new(p):
            return jnp.concatenate(
                [jnp.dot(p[h * t:(h + 1) * t], vn_ref[:, h * HEAD_DIM:(h + 1) * HEAD_DIM], preferred_element_type=F32)
                 for h in range(n_heads)], axis=0)

        online_update(s2, pv_new)
        o_ref[...] = acc_ref[...] / l_ref[...]


def fox_sample_attention(q, c_new, d_past, k_new, v_new, k_pool, v_pool, page_table_flat, b, t, n_heads, n_pages):
    ng = n_heads // HEADS_PER_TILE
    pps = FOX_PAGES_PER_STEP
    assert n_pages % pps == 0
    n_steps = n_pages // pps
    n_pool = k_pool.shape[0]
    ht = n_heads * t
    q4 = q.reshape(b, t, n_heads, HEAD_DIM).transpose(0, 2, 1, 3)
    c_ht = c_new.transpose(0, 2, 1).reshape(b, ht, 1)
    cn = jnp.broadcast_to(c_ht, (b, ht, PAGE_SIZE))
    cnt = jnp.broadcast_to(c_new.transpose(0, 2, 1)[:, :, None, :], (b, n_heads, t, t)).reshape(b, ht, t)
    dt = d_past.transpose(0, 1, 3, 2)
    kp = k_pool.reshape(n_pool, PAGE_SIZE, ng, HEADS_PER_TILE, HEAD_DIM)
    vp = v_pool.reshape(n_pool, PAGE_SIZE, ng, HEADS_PER_TILE, HEAD_DIM)

    def page_spec(pp, g):
        return pl.BlockSpec((None, PAGE_SIZE, None, HEADS_PER_TILE, HEAD_DIM),
                            lambda i, j, pt: (pt[i * n_pages + j * pps + pp], 0, g, 0, 0))

    page_specs = [page_spec(pp, g) for pp in range(pps) for g in range(ng)]
    d = n_heads * HEAD_DIM
    grid_spec = pltpu.PrefetchScalarGridSpec(
        num_scalar_prefetch=1,
        grid=(b, n_steps),
        in_specs=[
            pl.BlockSpec((None, n_heads, t, HEAD_DIM), lambda i, j, pt: (i, 0, 0, 0)),
            pl.BlockSpec((None, ht, PAGE_SIZE), lambda i, j, pt: (i, 0, 0)),
            pl.BlockSpec((None, ht, t), lambda i, j, pt: (i, 0, 0)),
            pl.BlockSpec((None, pps, n_heads, PAGE_SIZE), lambda i, j, pt: (i, j, 0, 0)),
            pl.BlockSpec((t, d), lambda i, j, pt: (i, 0)),
            pl.BlockSpec((t, d), lambda i, j, pt: (i, 0)),
        ] + page_specs * 2,
        out_specs=pl.BlockSpec((None, ht, HEAD_DIM), lambda i, j, pt: (i, 0, 0)),
        scratch_shapes=[pltpu.VMEM((ht, 1), F32), pltpu.VMEM((ht, 1), F32), pltpu.VMEM((ht, HEAD_DIM), F32)],
    )
    o = pl.pallas_call(
        functools.partial(_fox_sample_kernel, n_heads=n_heads, n_steps=n_steps, pps=pps),
        grid_spec=grid_spec,
        out_shape=jax.ShapeDtypeStruct((b, ht, HEAD_DIM), F32),
        compiler_params=_params("parallel", "arbitrary"),
        name="fox_sample_attention",
    )(page_table_flat, q4, cn, cnt, dt, k_new, v_new, *([kp] * (ng * pps)), *([vp] * (ng * pps)))
    return o.reshape(b, n_heads, t, HEAD_DIM).transpose(0, 2, 1, 3).reshape(b * t, d)


def _t5_bucket(dist):
    n = np.asarray(dist, dtype=np.int64)
    exact = N_BUCKETS // 2
    large = exact + (np.log(np.maximum(n, 1) / exact) / np.log(BUCKET_MAX_DIST / exact) * (N_BUCKETS - exact)).astype(np.int64)
    return np.where(n < exact, n, np.minimum(large, N_BUCKETS - 1)).astype(np.int32)


def _band_bias(rel_bias, group_heads):
    a = np.arange(Q_BLOCK)[:, None]
    bcol = np.arange(2 * Q_BLOCK)[None, :]
    j = a - bcol + Q_BLOCK
    out = []
    for g, (window, dil) in enumerate(PATTERNS):
        assert window // dil == Q_BLOCK
        valid = (j >= 0) & (j <= window // dil)
        bucket = _t5_bucket(dil * np.clip(j, 0, window // dil))
        bias = _select_buckets(rel_bias[:, g * group_heads:(g + 1) * group_heads], bucket)
        out.append(jnp.where(valid[None], bias, NEG))
    return jnp.stack(out)


def _select_buckets(rb, bucket):
    onehot = (jnp.asarray(bucket.reshape(-1))[None, :] == jnp.arange(N_BUCKETS)[:, None]).astype(F32)
    sel = jnp.einsum("bh,bn->hn", rb.astype(F32), onehot, precision=lax.Precision.HIGHEST)
    return sel.reshape((rb.shape[1],) + bucket.shape)


DILATED_BLOCKS_PER_STEP = 4


def _dilated_prompt_kernel(*refs, s):
    n_q = N_GROUPS * 2
    q_refs = refs[:n_q]
    k_ref, v_ref, bias_ref, o_ref, m_ref, l_ref, acc_ref = refs[n_q:]
    qb_rows = Q_BLOCK
    dn = (((1,), (1,)), ((), ()))

    def blocks(g, specs, merge):
        bias = bias_ref[g].reshape(2 * qb_rows, 2 * qb_rows)
        scores = []
        for q_idx, k_idx, first in specs:
            q2 = jnp.concatenate([q_refs[2 * g + r][q_idx, :].astype(BF16) for r in range(2)], axis=0)
            sc = lax.dot_general(q2, k_ref[k_idx, :].astype(BF16), dn, preferred_element_type=F32)
            scores.append(sc + (bias[:, qb_rows:] if first else bias))
        stats = []
        for sc in scores:
            m_blk = jnp.max(sc, axis=-1, keepdims=True)
            p = jnp.exp(sc - m_blk)
            stats.append((m_blk, jnp.sum(p, axis=-1, keepdims=True), p.astype(BF16)))
        outs = [jnp.dot(p, v_ref[k_idx, :].astype(BF16), preferred_element_type=F32)
                for (_, _, p), (_, k_idx, _) in zip(stats, specs)]
        for (m_blk, l_blk, _), o_blk, (q_idx, _, _) in zip(stats, outs, specs):
            for r in range(2):
                rs = slice(r * qb_rows, (r + 1) * qb_rows)
                m_b = jnp.broadcast_to(m_blk[rs], (qb_rows, HEAD_DIM))
                l_b = jnp.broadcast_to(l_blk[rs], (qb_rows, HEAD_DIM))
                if merge:
                    m_old = m_ref[r, q_idx, :]
                    m_new = jnp.maximum(m_old, m_b)
                    a_old = jnp.exp(m_old - m_new)
                    a_blk = jnp.exp(m_b - m_new)
                    l_ref[r, q_idx, :] = l_ref[r, q_idx, :] * a_old + l_b * a_blk
                    acc_ref[r, q_idx, :] = acc_ref[r, q_idx, :] * a_old + o_blk[rs] * a_blk
                    m_ref[r, q_idx, :] = m_new
                else:
                    m_ref[r, q_idx, :] = m_b
                    l_ref[r, q_idx, :] = l_b
                    acc_ref[r, q_idx, :] = o_blk[rs]

    per_step = DILATED_BLOCKS_PER_STEP
    for g, (_, dil) in enumerate(PATTERNS):
        n_blocks = s // (dil * qb_rows)
        merge = g > 0

        def idx(start, size, dil=dil):
            return pl.ds(start, size) if dil == 1 else pl.ds(start, size, stride=dil)

        def spec(c, qb, dil=dil, idx=idx):
            q0 = c + dil * qb_rows * qb
            if isinstance(qb, int) and qb == 0:
                return idx(q0, qb_rows), idx(q0, qb_rows), True
            return idx(q0, qb_rows), idx(q0 - dil * qb_rows, 2 * qb_rows), False

        if n_blocks == 1:
            assert dil % per_step == 0

            def classes(it, carry, g=g, merge=merge, spec=spec):
                blocks(g, [spec(it * per_step + u, 0) for u in range(per_step)], merge)
                return carry

            lax.fori_loop(0, dil // per_step, classes, 0)
        elif n_blocks <= per_step:

            def one_class(c, carry, g=g, merge=merge, spec=spec, n_blocks=n_blocks):
                blocks(g, [spec(c, qb) for qb in range(n_blocks)], merge)
                return carry

            lax.fori_loop(0, dil, one_class, 0)
        else:
            assert dil == 1 and (n_blocks - 1) % (per_step - 1) == 0
            blocks(g, [spec(0, 0)], merge)
            grp = per_step - 1

            def later(it, carry, g=g, merge=merge, spec=spec, grp=grp):
                blocks(g, [spec(0, 1 + it * grp + u) for u in range(grp)], merge)
                return carry

            lax.fori_loop(0, (n_blocks - 1) // grp, later, 0)

    for r in range(2):
        o_ref[:, r * HEAD_DIM:(r + 1) * HEAD_DIM] = (acc_ref[r] / l_ref[r]).astype(o_ref.dtype)


def dilated_prompt_attention(q, k, v, rel_bias, b, s, n_kv, group_heads):
    assert group_heads == 2 * n_kv
    bias = _band_bias(rel_bias, group_heads).reshape(N_GROUPS, n_kv, 2, Q_BLOCK, 2 * Q_BLOCK)

    def q_spec(g, r):
        return pl.BlockSpec((s, HEAD_DIM), lambda bi, kv: (bi, g * group_heads + 2 * kv + r))

    return pl.pallas_call(
        functools.partial(_dilated_prompt_kernel, s=s),
        grid=(b, n_kv),
        in_specs=[q_spec(g, r) for g in range(N_GROUPS) for r in range(2)] + [
            pl.BlockSpec((s, HEAD_DIM), lambda bi, kv: (bi, kv)),
            pl.BlockSpec((s, HEAD_DIM), lambda bi, kv: (bi, kv)),
            pl.BlockSpec((N_GROUPS, None, 2, Q_BLOCK, 2 * Q_BLOCK), lambda bi, kv: (0, kv, 0, 0, 0)),
        ],
        out_specs=pl.BlockSpec((s, 2 * HEAD_DIM), lambda bi, kv: (bi, kv)),
        out_shape=jax.ShapeDtypeStruct((b * s, group_heads * HEAD_DIM), BF16),
        scratch_shapes=[pltpu.VMEM((2, s, HEAD_DIM), F32)] * 3,
        compiler_params=_params("parallel", "parallel"),
        name="dilated_prompt_attention",
    )(*([q] * (N_GROUPS * 2)), k, v, bias)


def _sample_bias(rel_bias, group_heads, n_kv, t, n_buf):
    dist = n_buf + np.arange(t)[:, None] - np.arange(n_buf + t)[None, :]
    out = []
    for g, (window, dil) in enumerate(PATTERNS):
        valid = (dist >= 0) & (dist % dil == 0) & (dist <= window)
        bucket = _t5_bucket(np.clip(dist, 0, window))
        bias = _select_buckets(rel_bias[:, g * group_heads:(g + 1) * group_heads], bucket)
        out.append(jnp.where(valid[None], bias, NEG).reshape(n_kv, 2, t, n_buf + t))
    return jnp.stack(out, axis=1).reshape(n_kv, N_GROUPS * 2 * t, n_buf + t)


def _dilated_sample_kernel(q_ref, kc_ref, vc_ref, kn_ref, vn_ref, bc_ref, bn_ref, o_ref, *, n_kv, group_heads):
    t = q_ref.shape[0]
    n_buf = kc_ref.shape[0]
    for kv in range(n_kv):
        heads = [g * group_heads + 2 * kv + r for g in range(N_GROUPS) for r in range(2)]
        q6 = jnp.concatenate([q_ref[:, h * HEAD_DIM:(h + 1) * HEAD_DIM] for h in heads], axis=0).astype(BF16)
        kc = _tile_rows(kc_ref, kv).astype(BF16)
        vc = _tile_rows(vc_ref, kv).astype(BF16)
        sl = slice(kv * HEAD_DIM, (kv + 1) * HEAD_DIM)
        kn = kn_ref[:, sl].astype(BF16)
        vn = vn_ref[:, sl]
        dn = (((1,), (1,)), ((), ()))
        s_c = lax.dot_general(q6, kc, dn, preferred_element_type=F32) + bc_ref[kv]
        s_n = lax.dot_general(q6, kn, dn, preferred_element_type=F32) + bn_ref[kv]
        m_row = jnp.maximum(jnp.max(s_c, axis=-1, keepdims=True), jnp.max(s_n, axis=-1, keepdims=True))
        m = jnp.max(m_row.reshape(N_GROUPS, 2 * t, 1), axis=0)
        m_all = jnp.concatenate([m] * N_GROUPS, axis=0)
        p_c = jnp.exp(s_c - m_all)
        p_n = jnp.exp(s_n - m_all)
        l_row = jnp.sum(p_c, axis=-1, keepdims=True) + jnp.sum(p_n, axis=-1, keepdims=True)
        o_row = jnp.dot(p_c.astype(BF16), vc, preferred_element_type=F32) + jnp.dot(p_n, vn, preferred_element_type=F32)
        l = jnp.sum(l_row.reshape(N_GROUPS, 2 * t, 1), axis=0)
        o = jnp.sum(o_row.reshape(N_GROUPS, 2 * t, HEAD_DIM), axis=0) / l
        for r in range(2):
            o_ref[:, (2 * kv + r) * HEAD_DIM:(2 * kv + r + 1) * HEAD_DIM] = o[r * t:(r + 1) * t]


def dilated_sample_attention(q, k_new, v_new, cache_k, cache_v, rel_bias, b, t, n_kv, group_heads):
    n_buf = cache_k.shape[1]
    bias = _sample_bias(rel_bias, group_heads, n_kv, t, n_buf)
    bias_c, bias_n = bias[:, :, :n_buf], bias[:, :, n_buf:]
    dq = q.shape[1]
    dkv = n_kv * HEAD_DIM
    return pl.pallas_call(
        functools.partial(_dilated_sample_kernel, n_kv=n_kv, group_heads=group_heads),
        grid=(b,),
        in_specs=[
            pl.BlockSpec((t, dq), lambda i: (i, 0)),
            pl.BlockSpec((None, n_buf, n_kv, HEAD_DIM), lambda i: (i, 0, 0, 0)),
            pl.BlockSpec((None, n_buf, n_kv, HEAD_DIM), lambda i: (i, 0, 0, 0)),
            pl.BlockSpec((t, dkv), lambda i: (i, 0)),
            pl.BlockSpec((t, dkv), lambda i: (i, 0)),
            pl.BlockSpec(bias_c.shape, lambda i: (0, 0, 0)),
            pl.BlockSpec(bias_n.shape, lambda i: (0, 0, 0)),
        ],
        out_specs=pl.BlockSpec((t, group_heads * HEAD_DIM), lambda i: (i, 0)),
        out_shape=jax.ShapeDtypeStruct((b * t, group_heads * HEAD_DIM), F32),
        compiler_params=_params("parallel"),
        name="dilated_sample_attention",
    )(q, cache_k, cache_v, k_new, v_new, bias_c, bias_n)


CONV_PAD_ROWS = 8


def _upgate_kernel(x_ref, wu_ref, wg_ref, prev_ref, wc_ref, bc_ref, h_ref, st_ref, pad_ref, *, n_seq, t):
    x = x_ref[...]
    u = jnp.dot(x, wu_ref[...], preferred_element_type=F32)
    gate = jnp.dot(x, wg_ref[...], preferred_element_type=F32)
    lo = CONV_PAD_ROWS - (CONV_W - 1)
    pieces = []
    for sq in range(n_seq):
        rs = slice(sq * t, (sq + 1) * t)
        pad_ref[pl.ds(CONV_PAD_ROWS, t), :] = u[rs]
        pad_ref[pl.ds(lo, CONV_W - 1), :] = prev_ref[sq]
        uc = bc_ref[...] + pad_ref[pl.ds(lo, t), :] * wc_ref[0:1, :]
        for i in range(1, CONV_W):
            uc = uc + pad_ref[pl.ds(lo + i, t), :] * wc_ref[i:i + 1, :]
        pieces.append(0.5 * uc * (1.0 + lax.erf(uc * SQRT_HALF)) * gate[rs])
        st_ref[sq] = pad_ref[pl.ds(lo + t, CONV_W - 1), :]
    hh = pieces[0] if n_seq == 1 else jnp.concatenate(pieces, axis=0)
    h_ref[...] = hh.astype(h_ref.dtype)


def conv_ffn_hidden(xn, w_up, w_gate, prev, w_conv, b_conv, n_seq, t, seq_per_tile, tn=256):
    m, d = xn.shape
    f = w_up.shape[1]
    tm = seq_per_tile * t
    return pl.pallas_call(
        functools.partial(_upgate_kernel, n_seq=seq_per_tile, t=t),
        grid=(n_seq // seq_per_tile, f // tn),
        in_specs=[
            pl.BlockSpec((tm, d), lambda i, j: (i, 0)),
            pl.BlockSpec((d, tn), lambda i, j: (0, j)),
            pl.BlockSpec((d, tn), lambda i, j: (0, j)),
            pl.BlockSpec((seq_per_tile, CONV_W - 1, tn), lambda i, j: (i, 0, j)),
            pl.BlockSpec((CONV_W, tn), lambda i, j: (0, j)),
            pl.BlockSpec((1, tn), lambda i, j: (0, j)),
        ],
        out_specs=[
            pl.BlockSpec((tm, tn), lambda i, j: (i, j)),
            pl.BlockSpec((seq_per_tile, CONV_W - 1, tn), lambda i, j: (i, 0, j)),
        ],
        out_shape=[jax.ShapeDtypeStruct((m, f), BF16), jax.ShapeDtypeStruct((n_seq, CONV_W - 1, f), F32)],
        scratch_shapes=[pltpu.VMEM((t + CONV_PAD_ROWS, tn), F32)],
        compiler_params=_params("parallel", "parallel"),
        name="conv_ffn_hidden",
    )(xn, w_up, w_gate, prev, w_conv, b_conv)


def _forward(x, n_seq, t, weights, sample_state):
    (w_q_a, w_k_a, w_v_a, w_f_a, b_f_a, w_o_a, g_attn, g_kv, w_kv_b, w_q_b, w_o_b, rel_bias,
     g_ffn, w_up, w_gate, w_conv, b_conv, w_down, g_final) = weights
    m, d_model = x.shape
    n_heads = w_q_a.shape[1] // HEAD_DIM
    n_kv = w_kv_b.shape[1] // (2 * HEAD_DIM)
    group_heads = w_q_b.shape[1] // (N_GROUPS * HEAD_DIM)
    d_ff = w_up[0].shape[1]
    is_prompt = sample_state is None
    big = dict(bm=1024, bn=1024) if is_prompt else dict(bm=m, bn=1024)
    big_res = dict(bm=1024, bn=512) if is_prompt else dict(bm=m, bn=1024)
    out = {}

    def ffn(x, layer, prev):
        (xn,) = rmsnorm(x, g_ffn[layer][None], BF16)
        spt = 1 if is_prompt else n_seq
        h, st = conv_ffn_hidden(xn, w_up[layer], w_gate[layer], prev, w_conv[layer], b_conv[layer][None],
                                n_seq, t, spt)
        return matmul(h, w_down[layer], res=x, bk=d_ff // 2, **big_res), st

    (xn,) = rmsnorm(x, g_attn[0][None], BF16)
    q = matmul(xn, w_q_a, out_dtype=BF16, out_scale=ATTN_SCALE, **big)
    k = matmul(xn, w_k_a, **big)
    v = matmul(xn, w_v_a, **big)
    logf = forget_gate(xn, w_f_a, b_f_a)
    c = cumsum_rows(logf.reshape(n_seq, t, n_heads))
    if is_prompt:
        a = fox_prompt_attention(q, k, v, c.reshape(m, n_heads), n_seq, t, n_heads)
        prev0 = prev1 = jnp.zeros((n_seq, CONV_W - 1, d_ff), F32)
    else:
        pool_k, pool_v, pool_lf, win_k, win_v, conv_state, page_table = sample_state
        n_pages = page_table.shape[1]
        pt_flat = page_table.reshape(-1)
        d_past = page_suffix_sums(pt_flat, pool_lf, n_seq, n_pages)
        a = fox_sample_attention(q, c, d_past, k, v, pool_k, pool_v, pt_flat, n_seq, t, n_heads, n_pages)
        a = a.astype(BF16)
        prev0, prev1 = conv_state[0], conv_state[1]
    out["fox_k"], out["fox_v"], out["fox_lf"] = k, v, logf
    x = matmul(a, w_o_a, res=x, **big_res)
    x, st0 = ffn(x, 0, prev0)

    xn, xkv = rmsnorm(x, jnp.stack([g_attn[1], g_kv]), BF16)
    kv = matmul(xkv, w_kv_b, **big)
    k_b, v_b = kv[:, :n_kv * HEAD_DIM], kv[:, n_kv * HEAD_DIM:]
    q = matmul(xn, w_q_b, out_scale=ATTN_SCALE, **big)
    if is_prompt:
        a = dilated_prompt_attention(q, k_b, v_b, rel_bias, n_seq, t, n_kv, group_heads)
    else:
        a = dilated_sample_attention(q, k_b, v_b, win_k, win_v, rel_bias, n_seq, t, n_kv, group_heads)
        a = a.astype(BF16)
    out["win_k"], out["win_v"] = k_b, v_b
    x = matmul(a, w_o_b, res=x, **big_res)
    x, st1 = ffn(x, 1, prev1)
    (y,) = rmsnorm(x, g_final[None], F32)
    out["y"] = y
    out["conv"] = jnp.stack([st0, st1])
    return out


def kernel(x_prompt, x_sample, cache_fox_k, cache_fox_v, cache_fox_logf, cache_win_k, cache_win_v, state_ffn_conv,
           page_table, w_in_a, b_f_a, w_o_a, g_attn, g_kv, w_kv_b, w_q_b, w_o_b, rel_bias,
           g_ffn, w_up, w_gate, w_conv, b_conv, w_down, g_final):
    bp, sp, d_model = x_prompt.shape
    bs, ts, _ = x_sample.shape
    n_heads = b_f_a.shape[1]
    hd = n_heads * HEAD_DIM
    n_kv = cache_win_k.shape[2]
    assert w_in_a.shape[0] == 1 and w_q_b.shape[0] == 1 and g_attn.shape[0] == 2

    w_in = w_in_a[0]
    w_f = jnp.pad(w_in[:, 3 * hd:], ((0, 0), (0, HEAD_DIM - n_heads))).astype(BF16)
    n_layers = w_up.shape[0]

    def per_layer(w):
        return [w[l].astype(BF16) for l in range(n_layers)]

    weights = (
        w_in[:, :hd].astype(BF16), w_in[:, hd:2 * hd].astype(BF16), w_in[:, 2 * hd:3 * hd].astype(BF16), w_f,
        b_f_a[0][None].astype(F32), w_o_a[0].astype(BF16), g_attn, g_kv, w_kv_b.astype(BF16), w_q_b[0].astype(BF16),
        w_o_b[0].astype(BF16), rel_bias, g_ffn, per_layer(w_up), per_layer(w_gate), w_conv, b_conv,
        per_layer(w_down), g_final)

    p = _forward(x_prompt.reshape(bp * sp, d_model), bp, sp, weights, None)
    sample_state = (cache_fox_k[0], cache_fox_v[0], cache_fox_logf[0], cache_win_k, cache_win_v, state_ffn_conv,
                    page_table)
    s = _forward(x_sample.reshape(bs * ts, d_model), bs, ts, weights, sample_state)

    n_pg = sp // PAGE_SIZE
    n_buf = cache_win_k.shape[1]
    win_k_new = s["win_k"].reshape(bs, ts, n_kv, HEAD_DIM)
    win_v_new = s["win_v"].reshape(bs, ts, n_kv, HEAD_DIM)
    return (
        p["y"].reshape(bp, sp, d_model),
        s["y"].reshape(bs, ts, d_model),
        p["fox_k"].reshape(1, bp, n_pg, PAGE_SIZE, n_heads, HEAD_DIM),
        p["fox_v"].reshape(1, bp, n_pg, PAGE_SIZE, n_heads, HEAD_DIM),
        p["fox_lf"].reshape(1, bp, n_pg, PAGE_SIZE, n_heads),
        s["fox_k"].reshape(1, bs, ts, n_heads, HEAD_DIM),
        s["fox_v"].reshape(1, bs, ts, n_heads, HEAD_DIM),
        s["fox_lf"].reshape(1, bs, ts, n_heads),
        p["win_k"].reshape(bp, sp, n_kv, HEAD_DIM)[:, sp - min(BUCKET_MAX_DIST, sp):],
        p["win_v"].reshape(bp, sp, n_kv, HEAD_DIM)[:, sp - min(BUCKET_MAX_DIST, sp):],
        jnp.concatenate([cache_win_k[:, ts:], win_k_new], axis=1)[:, -n_buf:],
        jnp.concatenate([cache_win_v[:, ts:], win_v_new], axis=1)[:, -n_buf:],
        p["conv"],
        s["conv"],
    )
```

```python
import functools
import math

import numpy as np
import jax
import jax.numpy as jnp
from jax import lax
from jax.experimental import pallas as pl
from jax.experimental.pallas import tpu as pltpu

F32 = jnp.float32
BF16 = jnp.bfloat16

HEAD_DIM = 128
PAGE_SIZE = 128
PATTERNS = ((128, 1), (512, 4), (2048, 16))
N_GROUPS = len(PATTERNS)
N_BUCKETS = 32
BUCKET_MAX_DIST = max(w for w, _ in PATTERNS)
CONV_W = 3
Q_BLOCK = 128
NORM_EPS = 1e-6
ATTN_SCALE = HEAD_DIM ** -0.5
NEG = -1e30
SQRT_HALF = 0.7071067811865476

V7X_VMEM_LIMIT_BYTES = 56 * 1024 * 1024


def _params(*semantics):
    return pltpu.CompilerParams(dimension_semantics=semantics, vmem_limit_bytes=V7X_VMEM_LIMIT_BYTES)


def _rmsnorm_kernel(x_ref, g_ref, *o_refs):
    x = x_ref[...]
    y = x * lax.rsqrt(jnp.mean(x * x, axis=-1, keepdims=True) + NORM_EPS)
    for j, o_ref in enumerate(o_refs):
        o_ref[...] = (y * g_ref[j:j + 1, :]).astype(o_ref.dtype)


def rmsnorm(x, gains, out_dtype):
    m, d = x.shape
    n = gains.shape[0]
    tm = min(256, m)
    return pl.pallas_call(
        _rmsnorm_kernel,
        grid=(m // tm,),
        in_specs=[pl.BlockSpec((tm, d), lambda i: (i, 0)), pl.BlockSpec((n, d), lambda i: (0, 0))],
        out_specs=[pl.BlockSpec((tm, d), lambda i: (i, 0))] * n,
        out_shape=[jax.ShapeDtypeStruct((m, d), out_dtype)] * n,
        compiler_params=_params("parallel"),
        name="rmsnorm",
    )(x, gains)


def _matmul_kernel(*refs, has_res, nk, out_scale):
    x_ref, w_ref = refs[0], refs[1]
    res_ref = refs[2] if has_res else None
    o_ref = refs[2 + has_res]

    def finish(acc):
        if out_scale != 1.0:
            acc = acc * out_scale
        if has_res:
            acc = acc + res_ref[...]
        o_ref[...] = acc.astype(o_ref.dtype)

    part = jnp.dot(x_ref[...], w_ref[...], preferred_element_type=F32)
    if nk == 1:
        finish(part)
        return
    acc_ref = refs[3 + has_res]
    k = pl.program_id(2)

    @pl.when(k == 0)
    def _():
        acc_ref[...] = part

    @pl.when(jnp.logical_and(k > 0, k < nk - 1))
    def _():
        acc_ref[...] += part

    @pl.when(k == nk - 1)
    def _():
        finish(acc_ref[...] + part)


def matmul(x, w, layer, *, res=None, out_dtype=F32, out_scale=1.0, bm=1024, bn=1024, bk=None):
    m, kdim = x.shape
    n = w.shape[2]
    bm, bn = min(bm, m), min(bn, n)
    bk = kdim if bk is None else bk
    nk = kdim // bk
    assert m % bm == 0 and n % bn == 0 and kdim % bk == 0 and nk >= 1
    in_specs = [pl.BlockSpec((bm, bk), lambda i, j, k: (i, k)),
                pl.BlockSpec((None, bk, bn), lambda i, j, k: (layer, k, j))]
    args = [x, w]
    if res is not None:
        in_specs.append(pl.BlockSpec((bm, bn), lambda i, j, k: (i, j)))
        args.append(res)
    return pl.pallas_call(
        functools.partial(_matmul_kernel, has_res=res is not None, nk=nk, out_scale=out_scale),
        grid=(m // bm, n // bn, nk),
        in_specs=in_specs,
        out_specs=pl.BlockSpec((bm, bn), lambda i, j, k: (i, j)),
        out_shape=jax.ShapeDtypeStruct((m, n), out_dtype),
        scratch_shapes=[pltpu.VMEM((bm, bn), F32)] if nk > 1 else [],
        compiler_params=_params("parallel", "parallel", "arbitrary"),
        name="matmul",
    )(*args)


def _pair_matmul_kernel(*refs, nm, has_res, out_scale):
    if has_res:
        xp_ref, xs_ref, w_ref, rp_ref, rs_ref, op_ref, os_ref, wb_ref = refs
    else:
        xp_ref, xs_ref, w_ref, op_ref, os_ref, wb_ref = refs
        rp_ref = rs_ref = None
    i = pl.program_id(1)

    @pl.when(i == 0)
    def _():
        wb_ref[...] = w_ref[...].astype(BF16)

    def emit(x_ref, r_ref, o_ref):
        acc = jnp.dot(x_ref[...], wb_ref[...], preferred_element_type=F32)
        if out_scale != 1.0:
            acc = acc * out_scale
        if r_ref is not None:
            acc = acc + r_ref[...]
        o_ref[...] = acc.astype(o_ref.dtype)

    @pl.when(i < nm)
    def _():
        emit(xp_ref, rp_ref, op_ref)

    @pl.when(i == nm)
    def _():
        emit(xs_ref, rs_ref, os_ref)


def matmul_pair(xp, xs, w, layer, col0, n, *, res=None, out_dtype=F32, out_scale=1.0, bm=1024, bn=512):
    mp, kdim = xp.shape
    ms = xs.shape[0]
    assert mp % bm == 0 and n % bn == 0 and col0 % bn == 0 and w.shape[1] == kdim
    nm, cb0, last = mp // bm, col0 // bn, mp // bm - 1
    row = lambda j, i: (jnp.minimum(i, last), 0)
    tile = lambda j, i: (jnp.minimum(i, last), j)
    in_specs = [pl.BlockSpec((bm, kdim), row), pl.BlockSpec((ms, kdim), lambda j, i: (0, 0)),
                pl.BlockSpec((None, kdim, bn), lambda j, i: (layer, 0, cb0 + j))]
    args = [xp, xs, w]
    if res is not None:
        in_specs += [pl.BlockSpec((bm, bn), tile), pl.BlockSpec((ms, bn), lambda j, i: (0, j))]
        args += list(res)
    return pl.pallas_call(
        functools.partial(_pair_matmul_kernel, nm=nm, has_res=res is not None, out_scale=out_scale),
        grid=(n // bn, nm + 1),
        in_specs=in_specs,
        out_specs=[pl.BlockSpec((bm, bn), tile), pl.BlockSpec((ms, bn), lambda j, i: (0, j))],
        out_shape=[jax.ShapeDtypeStruct((mp, n), out_dtype), jax.ShapeDtypeStruct((ms, n), out_dtype)],
        scratch_shapes=[pltpu.VMEM((kdim, bn), BF16)],
        compiler_params=_params("parallel", "arbitrary"),
        name="matmul_pair",
    )(*args)


def _gate_kernel(x_ref, w_ref, b_ref, o_ref):
    nh = o_ref.shape[1]
    z = jnp.dot(x_ref[...], w_ref[:, :nh].astype(BF16), preferred_element_type=F32) + b_ref[...]
    o_ref[...] = -(jnp.maximum(-z, 0.0) + jnp.log1p(jnp.exp(-jnp.abs(z))))


def forget_gate(xn, w, layer, col0, b_f):
    m, d = xn.shape
    nh = b_f.shape[1]
    tm = min(512, m)
    assert col0 % HEAD_DIM == 0 and nh <= HEAD_DIM
    return pl.pallas_call(
        _gate_kernel,
        grid=(m // tm,),
        in_specs=[pl.BlockSpec((tm, d), lambda i: (i, 0)),
                  pl.BlockSpec((None, d, HEAD_DIM), lambda i: (layer, 0, col0 // HEAD_DIM)),
                  pl.BlockSpec((1, nh), lambda i: (0, 0))],
        out_specs=pl.BlockSpec((tm, nh), lambda i: (i, 0)),
        out_shape=jax.ShapeDtypeStruct((m, nh), F32),
        compiler_params=_params("parallel"),
        name="forget_gate",
    )(xn, w, b_f)


def _split_dot(tri, x):
    hi = x.astype(BF16)
    r1 = x - hi.astype(F32)
    mid = r1.astype(BF16)
    lo = (r1 - mid.astype(F32)).astype(BF16)
    out = jnp.dot(tri, lo, preferred_element_type=F32)
    out = out + jnp.dot(tri, mid, preferred_element_type=F32)
    return out + jnp.dot(tri, hi, preferred_element_type=F32)


def _cumsum_kernel(x_ref, o_ref, *, chunk):
    s = x_ref.shape[0]
    row = lax.broadcasted_iota(jnp.int32, (chunk, chunk), 0)
    col = lax.broadcasted_iota(jnp.int32, (chunk, chunk), 1)
    tri = jnp.where(col <= row, 1.0, 0.0).astype(BF16)
    carry = jnp.zeros((1, x_ref.shape[1]), F32)
    for i in range(s // chunk):
        y = _split_dot(tri, x_ref[i * chunk:(i + 1) * chunk, :]) + carry
        o_ref[i * chunk:(i + 1) * chunk, :] = y
        carry = y[chunk - 1:chunk, :]


def _cumsum_small_kernel(x_ref, o_ref):
    x = x_ref[...]
    t = x.shape[0]
    row = lax.broadcasted_iota(jnp.int32, x.shape, 0)
    for i in range(t):
        o_ref[i:i + 1, :] = jnp.sum(jnp.where(row <= i, x, 0.0), axis=0, keepdims=True)


def cumsum_rows(x):
    b, s, nh = x.shape
    body = _cumsum_small_kernel if s < 128 else functools.partial(_cumsum_kernel, chunk=min(256, s))
    return pl.pallas_call(
        body,
        grid=(b,),
        in_specs=[pl.BlockSpec((None, s, nh), lambda i: (i, 0, 0))],
        out_specs=pl.BlockSpec((None, s, nh), lambda i: (i, 0, 0)),
        out_shape=jax.ShapeDtypeStruct((b, s, nh), F32),
        compiler_params=_params("parallel"),
        name="cumsum_rows",
    )(x)


SUFFIX_PAGES_PER_STEP = 16


def _page_suffix_kernel(pt_ref, *refs, pps):
    lf_refs, o_ref, carry_ref = refs[:pps], refs[pps], refs[pps + 1]
    j = pl.program_id(1)
    n = lf_refs[0].shape[0]

    @pl.when(j == 0)
    def _():
        carry_ref[...] = jnp.zeros_like(carry_ref)

    row = lax.broadcasted_iota(jnp.int32, (n, n), 0)
    col = lax.broadcasted_iota(jnp.int32, (n, n), 1)
    upper = jnp.where(col > row, 1.0, 0.0).astype(BF16)
    carry = carry_ref[...]
    for pp in reversed(range(pps)):
        x = lf_refs[pp][...]
        o_ref[pp] = _split_dot(upper, x) + carry
        carry = carry + jnp.sum(x, axis=0, keepdims=True)
    carry_ref[...] = carry


def page_suffix_sums(page_table_flat, lf_pool, b, n_pages):
    _, n, nh = lf_pool.shape
    pps = SUFFIX_PAGES_PER_STEP
    assert n_pages % pps == 0
    n_steps = n_pages // pps

    def page_spec(pp):
        return pl.BlockSpec((None, n, nh), lambda i, j, pt: (pt[i * n_pages + n_pages - (j + 1) * pps + pp], 0, 0))

    grid_spec = pltpu.PrefetchScalarGridSpec(
        num_scalar_prefetch=1,
        grid=(b, n_steps),
        in_specs=[page_spec(pp) for pp in range(pps)],
        out_specs=pl.BlockSpec((None, pps, n, nh), lambda i, j, pt: (i, n_steps - 1 - j, 0, 0)),
        scratch_shapes=[pltpu.VMEM((1, nh), F32)],
    )
    return pl.pallas_call(
        functools.partial(_page_suffix_kernel, pps=pps),
        grid_spec=grid_spec,
        out_shape=jax.ShapeDtypeStruct((b, n_pages, n, nh), F32),
        compiler_params=_params("parallel", "arbitrary"),
        name="page_suffix_sums",
    )(page_table_flat, *([lf_pool] * pps))


def _fox_prompt_kernel(q_ref, k_ref, v_ref, c_ref, ct_ref, o_ref):
    tq = q_ref.shape[0]
    h = pl.program_id(1)
    i = pl.program_id(2)
    q = q_ref[...]
    lane = lax.broadcasted_iota(jnp.int32, c_ref.shape, 1)
    cq = jnp.sum(jnp.where(lane == h, c_ref[...], 0.0), axis=-1, keepdims=True)

    def step(j, carry, masked):
        m, l, acc = carry
        ks = pl.multiple_of(j * tq, tq)
        kb = k_ref[pl.ds(ks, tq), :].astype(BF16)
        vb = v_ref[pl.ds(ks, tq), :].astype(BF16)
        s = lax.dot_general(q, kb, (((1,), (1,)), ((), ())), preferred_element_type=F32)
        s = s + (cq - ct_ref[pl.ds(j, 1), :])
        if masked:
            row = lax.broadcasted_iota(jnp.int32, s.shape, 0)
            col = lax.broadcasted_iota(jnp.int32, s.shape, 1)
            s = jnp.where(col <= row, s, NEG)
        m_new = jnp.maximum(m, jnp.max(s, axis=-1, keepdims=True))
        alpha = jnp.exp(m - m_new)
        p = jnp.exp(s - m_new)
        l = l * alpha + jnp.sum(p, axis=-1, keepdims=True)
        acc = acc * alpha + jnp.dot(p.astype(BF16), vb, preferred_element_type=F32)
        return m_new, l, acc

    init = (jnp.full((tq, 1), NEG, F32), jnp.zeros((tq, 1), F32), jnp.zeros((tq, HEAD_DIM), F32))
    carry = lax.fori_loop(0, i, lambda j, c: step(j, c, False), init)
    _, l, acc = step(i, carry, True)
    o_ref[...] = (acc / l).astype(o_ref.dtype)


def fox_prompt_attention(q, k, v, c, b, s, n_heads, tq=512):
    nq = s // tq
    ct = c.reshape(b, s, n_heads).transpose(0, 2, 1).reshape(b * n_heads, nq, tq)
    return pl.pallas_call(
        _fox_prompt_kernel,
        grid=(b, n_heads, nq),
        in_specs=[
            pl.BlockSpec((tq, HEAD_DIM), lambda bi, h, i: (bi * nq + i, h)),
            pl.BlockSpec((s, HEAD_DIM), lambda bi, h, i: (bi, h)),
            pl.BlockSpec((s, HEAD_DIM), lambda bi, h, i: (bi, h)),
            pl.BlockSpec((tq, n_heads), lambda bi, h, i: (bi * nq + i, 0)),
            pl.BlockSpec((None, nq, tq), lambda bi, h, i: (bi * n_heads + h, 0, 0)),
        ],
        out_specs=pl.BlockSpec((tq, HEAD_DIM), lambda bi, h, i: (bi * nq + i, h)),
        out_shape=jax.ShapeDtypeStruct(q.shape, BF16),
        compiler_params=_params("parallel", "parallel", "arbitrary"),
        name="fox_prompt_attention",
    )(q, k, v, c, ct)


HEADS_PER_TILE = 8


FOX_PAGES_PER_STEP = 2


def _tile_rows(ref, sub):
    rows, n_sub, dh = ref.shape
    return ref.reshape(rows * n_sub, dh)[pl.ds(sub, rows, stride=n_sub), :]


def _fox_sample_kernel(pt_ref, q_ref, cn_ref, cnt_ref, dt_ref, kn_ref, vn_ref, *refs, n_heads, n_steps, pps):
    ng = n_heads // HEADS_PER_TILE
    nb = ng * pps
    k_refs, v_refs = refs[:nb], refs[nb:2 * nb]
    o_ref, m_ref, l_ref, acc_ref = refs[2 * nb:]
    j = pl.program_id(1)
    t = q_ref.shape[1]
    dn = (((1,), (1,)), ((), ()))

    @pl.when(j == 0)
    def _():
        m_ref[...] = jnp.full_like(m_ref, NEG)
        l_ref[...] = jnp.zeros_like(l_ref)
        acc_ref[...] = jnp.zeros_like(acc_ref)

    def head_rows(page_refs, h):
        g, hs = divmod(h, HEADS_PER_TILE)
        tiles = [_tile_rows(page_refs[pp * ng + g], hs).astype(BF16) for pp in range(pps)]
        return tiles[0] if pps == 1 else jnp.concatenate(tiles, axis=0)

    def online_update(s, pv):
        m_old = m_ref[...]
        m_new = jnp.maximum(m_old, jnp.max(s, axis=-1, keepdims=True))
        alpha = jnp.exp(m_old - m_new)
        p = jnp.exp(s - m_new)
        l_ref[...] = l_ref[...] * alpha + jnp.sum(p, axis=-1, keepdims=True)
        acc_ref[...] = acc_ref[...] * alpha + pv(p)
        m_ref[...] = m_new

    parts = []
    for h in range(n_heads):
        sh = lax.dot_general(q_ref[h], head_rows(k_refs, h), dn, preferred_element_type=F32)
        d_row = [dt_ref[pp, h:h + 1, :] for pp in range(pps)]
        parts.append(sh + (d_row[0] if pps == 1 else jnp.concatenate(d_row, axis=1)))
    cn = cn_ref[...]
    s = jnp.concatenate(parts, axis=0) + (cn if pps == 1 else jnp.concatenate([cn] * pps, axis=1))

    def pv_pages(p):
        return jnp.concatenate(
            [jnp.dot(p[h * t:(h + 1) * t].astype(BF16), head_rows(v_refs, h), preferred_element_type=F32)
             for h in range(n_heads)], axis=0)

    online_update(s, pv_pages)

    @pl.when(j == n_steps - 1)
    def _():
        parts = []
        for h in range(n_heads):
            kh = kn_ref[:, h * HEAD_DIM:(h + 1) * HEAD_DIM].astype(BF16)
            parts.append(lax.dot_general(q_ref[h], kh, dn, preferred_element_type=F32))
        s2 = jnp.concatenate(parts, axis=0) + cn_ref[:, :t] - cnt_ref[...]
        row = lax.broadcasted_iota(jnp.int32, s2.shape, 0)
        col = lax.broadcasted_iota(jnp.int32, s2.shape, 1)
        s2 = jnp.where(col <= lax.rem(row, t), s2, NEG)

        def pv_new(p):
            return jnp.concatenate(
                [jnp.dot(p[h * t:(h + 1) * t], vn_ref[:, h * HEAD_DIM:(h + 1) * HEAD_DIM], preferred_element_type=F32)
                 for h in range(n_heads)], axis=0)

        online_update(s2, pv_new)
        o_ref[...] = acc_ref[...] / l_ref[...]


def fox_sample_attention(q, c_new, d_past, k_new, v_new, k_pool, v_pool, page_table_flat, b, t, n_heads, n_pages):
    ng = n_heads // HEADS_PER_TILE
    pps = FOX_PAGES_PER_STEP
    assert n_pages % pps == 0
    n_steps = n_pages // pps
    n_pool = k_pool.shape[0]
    ht = n_heads * t
    q4 = q.reshape(b, t, n_heads, HEAD_DIM).transpose(0, 2, 1, 3)
    c_ht = c_new.transpose(0, 2, 1).reshape(b, ht, 1)
    cn = jnp.broadcast_to(c_ht, (b, ht, PAGE_SIZE))
    cnt = jnp.broadcast_to(c_new.transpose(0, 2, 1)[:, :, None, :], (b, n_heads, t, t)).reshape(b, ht, t)
    dt = d_past.transpose(0, 1, 3, 2)
    kp = k_pool.reshape(n_pool, PAGE_SIZE, ng, HEADS_PER_TILE, HEAD_DIM)
    vp = v_pool.reshape(n_pool, PAGE_SIZE, ng, HEADS_PER_TILE, HEAD_DIM)

    def page_spec(pp, g):
        return pl.BlockSpec((None, PAGE_SIZE, None, HEADS_PER_TILE, HEAD_DIM),
                            lambda i, j, pt: (pt[i * n_pages + j * pps + pp], 0, g, 0, 0))

    page_specs = [page_spec(pp, g) for pp in range(pps) for g in range(ng)]
    d = n_heads * HEAD_DIM
    grid_spec = pltpu.PrefetchScalarGridSpec(
        num_scalar_prefetch=1,
        grid=(b, n_steps),
        in_specs=[
            pl.BlockSpec((None, n_heads, t, HEAD_DIM), lambda i, j, pt: (i, 0, 0, 0)),
            pl.BlockSpec((None, ht, PAGE_SIZE), lambda i, j, pt: (i, 0, 0)),
            pl.BlockSpec((None, ht, t), lambda i, j, pt: (i, 0, 0)),
            pl.BlockSpec((None, pps, n_heads, PAGE_SIZE), lambda i, j, pt: (i, j, 0, 0)),
            pl.BlockSpec((t, d), lambda i, j, pt: (i, 0)),
            pl.BlockSpec((t, d), lambda i, j, pt: (i, 0)),
        ] + page_specs * 2,
        out_specs=pl.BlockSpec((None, ht, HEAD_DIM), lambda i, j, pt: (i, 0, 0)),
        scratch_shapes=[pltpu.VMEM((ht, 1), F32), pltpu.VMEM((ht, 1), F32), pltpu.VMEM((ht, HEAD_DIM), F32)],
    )
    o = pl.pallas_call(
        functools.partial(_fox_sample_kernel, n_heads=n_heads, n_steps=n_steps, pps=pps),
        grid_spec=grid_spec,
        out_shape=jax.ShapeDtypeStruct((b, ht, HEAD_DIM), F32),
        compiler_params=_params("parallel", "arbitrary"),
        name="fox_sample_attention",
    )(page_table_flat, q4, cn, cnt, dt, k_new, v_new, *([kp] * (ng * pps)), *([vp] * (ng * pps)))
    return o.reshape(b, n_heads, t, HEAD_DIM).transpose(0, 2, 1, 3).reshape(b * t, d)


def _t5_bucket(dist):
    n = np.asarray(dist, dtype=np.int64)
    exact = N_BUCKETS // 2
    large = exact + (np.log(np.maximum(n, 1) / exact) / np.log(BUCKET_MAX_DIST / exact) * (N_BUCKETS - exact)).astype(np.int64)
    return np.where(n < exact, n, np.minimum(large, N_BUCKETS - 1)).astype(np.int32)


def _band_bias(rel_bias, group_heads):
    a = np.arange(Q_BLOCK)[:, None]
    bcol = np.arange(2 * Q_BLOCK)[None, :]
    j = a - bcol + Q_BLOCK
    out = []
    for g, (window, dil) in enumerate(PATTERNS):
        assert window // dil == Q_BLOCK
        valid = (j >= 0) & (j <= window // dil)
        bucket = _t5_bucket(dil * np.clip(j, 0, window // dil))
        bias = _select_buckets(rel_bias[:, g * group_heads:(g + 1) * group_heads], bucket)
        out.append(jnp.where(valid[None], bias, NEG))
    return jnp.stack(out)


def _select_buckets(rb, bucket):
    onehot = (jnp.asarray(bucket.reshape(-1))[None, :] == jnp.arange(N_BUCKETS)[:, None]).astype(F32)
    sel = jnp.einsum("bh,bn->hn", rb.astype(F32), onehot, precision=lax.Precision.HIGHEST)
    return sel.reshape((rb.shape[1],) + bucket.shape)


DILATED_BLOCKS_PER_STEP = 4


def _dilated_prompt_kernel(*refs, s):
    n_q = N_GROUPS * 2
    q_refs = refs[:n_q]
    k_ref, v_ref, bias_ref, o_ref, m_ref, l_ref, acc_ref = refs[n_q:]
    qb_rows = Q_BLOCK
    dn = (((1,), (1,)), ((), ()))

    def blocks(g, specs, merge):
        bias = bias_ref[g].reshape(2 * qb_rows, 2 * qb_rows)
        scores = []
        for q_idx, k_idx, first in specs:
            q2 = jnp.concatenate([q_refs[2 * g + r][q_idx, :].astype(BF16) for r in range(2)], axis=0)
            sc = lax.dot_general(q2, k_ref[k_idx, :].astype(BF16), dn, preferred_element_type=F32)
            scores.append(sc + (bias[:, qb_rows:] if first else bias))
        stats = []
        for sc in scores:
            m_blk = jnp.max(sc, axis=-1, keepdims=True)
            p = jnp.exp(sc - m_blk)
            stats.append((m_blk, jnp.sum(p, axis=-1, keepdims=True), p.astype(BF16)))
        outs = [jnp.dot(p, v_ref[k_idx, :].astype(BF16), preferred_element_type=F32)
                for (_, _, p), (_, k_idx, _) in zip(stats, specs)]
        for (m_blk, l_blk, _), o_blk, (q_idx, _, _) in zip(stats, outs, specs):
            for r in range(2):
                rs = slice(r * qb_rows, (r + 1) * qb_rows)
                m_b = jnp.broadcast_to(m_blk[rs], (qb_rows, HEAD_DIM))
                l_b = jnp.broadcast_to(l_blk[rs], (qb_rows, HEAD_DIM))
                if merge:
                    m_old = m_ref[r, q_idx, :]
                    m_new = jnp.maximum(m_old, m_b)
                    a_old = jnp.exp(m_old - m_new)
                    a_blk = jnp.exp(m_b - m_new)
                    l_ref[r, q_idx, :] = l_ref[r, q_idx, :] * a_old + l_b * a_blk
                    acc_ref[r, q_idx, :] = acc_ref[r, q_idx, :] * a_old + o_blk[rs] * a_blk
                    m_ref[r, q_idx, :] = m_new
                else:
                    m_ref[r, q_idx, :] = m_b
                    l_ref[r, q_idx, :] = l_b
                    acc_ref[r, q_idx, :] = o_blk[rs]

    per_step = DILATED_BLOCKS_PER_STEP
    for g, (_, dil) in enumerate(PATTERNS):
        n_blocks = s // (dil * qb_rows)
        merge = g > 0

        def idx(start, size, dil=dil):
            return pl.ds(start, size) if dil == 1 else pl.ds(start, size, stride=dil)

        def spec(c, qb, dil=dil, idx=idx):
            q0 = c + dil * qb_rows * qb
            if isinstance(qb, int) and qb == 0:
                return idx(q0, qb_rows), idx(q0, qb_rows), True
            return idx(q0, qb_rows), idx(q0 - dil * qb_rows, 2 * qb_rows), False

        if n_blocks == 1:
            assert dil % per_step == 0

            def classes(it, carry, g=g, merge=merge, spec=spec):
                blocks(g, [spec(it * per_step + u, 0) for u in range(per_step)], merge)
                return carry

            lax.fori_loop(0, dil // per_step, classes, 0)
        elif n_blocks <= per_step:

            def one_class(c, carry, g=g, merge=merge, spec=spec, n_blocks=n_blocks):
                blocks(g, [spec(c, qb) for qb in range(n_blocks)], merge)
                return carry

            lax.fori_loop(0, dil, one_class, 0)
        else:
            assert dil == 1 and (n_blocks - 1) % (per_step - 1) == 0
            blocks(g, [spec(0, 0)], merge)
            grp = per_step - 1

            def later(it, carry, g=g, merge=merge, spec=spec, grp=grp):
                blocks(g, [spec(0, 1 + it * grp + u) for u in range(grp)], merge)
                return carry

            lax.fori_loop(0, (n_blocks - 1) // grp, later, 0)

    for r in range(2):
        o_ref[:, r * HEAD_DIM:(r + 1) * HEAD_DIM] = (acc_ref[r] / l_ref[r]).astype(o_ref.dtype)


def dilated_prompt_attention(q, k, v, rel_bias, b, s, n_kv, group_heads):
    assert group_heads == 2 * n_kv
    bias = _band_bias(rel_bias, group_heads).reshape(N_GROUPS, n_kv, 2, Q_BLOCK, 2 * Q_BLOCK)

    def q_spec(g, r):
        return pl.BlockSpec((s, HEAD_DIM), lambda bi, kv: (bi, g * group_heads + 2 * kv + r))

    return pl.pallas_call(
        functools.partial(_dilated_prompt_kernel, s=s),
        grid=(b, n_kv),
        in_specs=[q_spec(g, r) for g in range(N_GROUPS) for r in range(2)] + [
            pl.BlockSpec((s, HEAD_DIM), lambda bi, kv: (bi, kv)),
            pl.BlockSpec((s, HEAD_DIM), lambda bi, kv: (bi, kv)),
            pl.BlockSpec((N_GROUPS, None, 2, Q_BLOCK, 2 * Q_BLOCK), lambda bi, kv: (0, kv, 0, 0, 0)),
        ],
        out_specs=pl.BlockSpec((s, 2 * HEAD_DIM), lambda bi, kv: (bi, kv)),
        out_shape=jax.ShapeDtypeStruct((b * s, group_heads * HEAD_DIM), BF16),
        scratch_shapes=[pltpu.VMEM((2, s, HEAD_DIM), F32)] * 3,
        compiler_params=_params("parallel", "parallel"),
        name="dilated_prompt_attention",
    )(*([q] * (N_GROUPS * 2)), k, v, bias)


def _sample_bias(rel_bias, group_heads, n_kv, t, n_buf):
    dist = n_buf + np.arange(t)[:, None] - np.arange(n_buf + t)[None, :]
    out = []
    for g, (window, dil) in enumerate(PATTERNS):
        valid = (dist >= 0) & (dist % dil == 0) & (dist <= window)
        bucket = _t5_bucket(np.clip(dist, 0, window))
        bias = _select_buckets(rel_bias[:, g * group_heads:(g + 1) * group_heads], bucket)
        out.append(jnp.where(valid[None], bias, NEG).reshape(n_kv, 2, t, n_buf + t))
    return jnp.stack(out, axis=1).reshape(n_kv, N_GROUPS * 2 * t, n_buf + t)


def _dilated_sample_kernel(q_ref, kc_ref, vc_ref, kn_ref, vn_ref, bc_ref, bn_ref, o_ref, *, n_kv, group_heads):
    t = q_ref.shape[0]
    n_buf = kc_ref.shape[0]
    for kv in range(n_kv):
        heads = [g * group_heads + 2 * kv + r for g in range(N_GROUPS) for r in range(2)]
        q6 = jnp.concatenate([q_ref[:, h * HEAD_DIM:(h + 1) * HEAD_DIM] for h in heads], axis=0).astype(BF16)
        kc = _tile_rows(kc_ref, kv).astype(BF16)
        vc = _tile_rows(vc_ref, kv).astype(BF16)
        sl = slice(kv * HEAD_DIM, (kv + 1) * HEAD_DIM)
        kn = kn_ref[:, sl].astype(BF16)
        vn = vn_ref[:, sl]
        dn = (((1,), (1,)), ((), ()))
        s_c = lax.dot_general(q6, kc, dn, preferred_element_type=F32) + bc_ref[kv]
        s_n = lax.dot_general(q6, kn, dn, preferred_element_type=F32) + bn_ref[kv]
        m_row = jnp.maximum(jnp.max(s_c, axis=-1, keepdims=True), jnp.max(s_n, axis=-1, keepdims=True))
        m = jnp.max(m_row.reshape(N_GROUPS, 2 * t, 1), axis=0)
        m_all = jnp.concatenate([m] * N_GROUPS, axis=0)
        p_c = jnp.exp(s_c - m_all)
        p_n = jnp.exp(s_n - m_all)
        l_row = jnp.sum(p_c, axis=-1, keepdims=True) + jnp.sum(p_n, axis=-1, keepdims=True)
        o_row = jnp.dot(p_c.astype(BF16), vc, preferred_element_type=F32) + jnp.dot(p_n, vn, preferred_element_type=F32)
        l = jnp.sum(l_row.reshape(N_GROUPS, 2 * t, 1), axis=0)
        o = jnp.sum(o_row.reshape(N_GROUPS, 2 * t, HEAD_DIM), axis=0) / l
        for r in range(2):
            o_ref[:, (2 * kv + r) * HEAD_DIM:(2 * kv + r + 1) * HEAD_DIM] = o[r * t:(r + 1) * t]


def dilated_sample_attention(q, k_new, v_new, cache_k, cache_v, rel_bias, b, t, n_kv, group_heads):
    n_buf = cache_k.shape[1]
    bias = _sample_bias(rel_bias, group_heads, n_kv, t, n_buf)
    bias_c, bias_n = bias[:, :, :n_buf], bias[:, :, n_buf:]
    dq = q.shape[1]
    dkv = n_kv * HEAD_DIM
    return pl.pallas_call(
        functools.partial(_dilated_sample_kernel, n_kv=n_kv, group_heads=group_heads),
        grid=(b,),
        in_specs=[
            pl.BlockSpec((t, dq), lambda i: (i, 0)),
            pl.BlockSpec((None, n_buf, n_kv, HEAD_DIM), lambda i: (i, 0, 0, 0)),
            pl.BlockSpec((None, n_buf, n_kv, HEAD_DIM), lambda i: (i, 0, 0, 0)),
            pl.BlockSpec((t, dkv), lambda i: (i, 0)),
            pl.BlockSpec((t, dkv), lambda i: (i, 0)),
            pl.BlockSpec(bias_c.shape, lambda i: (0, 0, 0)),
            pl.BlockSpec(bias_n.shape, lambda i: (0, 0, 0)),
        ],
        out_specs=pl.BlockSpec((t, group_heads * HEAD_DIM), lambda i: (i, 0)),
        out_shape=jax.ShapeDtypeStruct((b * t, group_heads * HEAD_DIM), F32),
        compiler_params=_params("parallel"),
        name="dilated_sample_attention",
    )(q, cache_k, cache_v, k_new, v_new, bias_c, bias_n)


CONV_PAD_ROWS = 8


FFN_ROW_TILE = 1024
FFN_COL_TILE = 256
FFN_ROW_CHUNKS = 4


def _ffn_pair_kernel(xp_ref, xs_ref, wu_ref, wg_ref, prev_ref, wc_ref, bc_ref, hp_ref, hs_ref, stp_ref, sts_ref,
                     wub_ref, wgb_ref, pad_ref, *, nm, tiles_per_seq, n_seq_s, t_s):
    i = pl.program_id(1)
    lo = CONV_PAD_ROWS - (CONV_W - 1)

    @pl.when(i == 0)
    def _():
        wub_ref[...] = wu_ref[...].astype(BF16)
        wgb_ref[...] = wg_ref[...].astype(BF16)

    def conv_gelu_gate(gate, start, t):
        uc = bc_ref[...] + pad_ref[pl.ds(lo + start, t), :] * wc_ref[0:1, :]
        for tap in range(1, CONV_W):
            uc = uc + pad_ref[pl.ds(lo + start + tap, t), :] * wc_ref[tap:tap + 1, :]
        return 0.5 * uc * (1.0 + lax.erf(uc * SQRT_HALF)) * gate

    @pl.when(i < nm)
    def _():
        tm = xp_ref.shape[0]
        rc = tm // FFN_ROW_CHUNKS

        @pl.when(i % tiles_per_seq == 0)
        def _():
            pad_ref[pl.ds(lo, CONV_W - 1), :] = jnp.zeros((CONV_W - 1, pad_ref.shape[1]), F32)

        gates = []
        for c in range(FFN_ROW_CHUNKS):
            x = xp_ref[c * rc:(c + 1) * rc, :]
            pad_ref[pl.ds(CONV_PAD_ROWS + c * rc, rc), :] = jnp.dot(x, wub_ref[...], preferred_element_type=F32)
            gates.append(jnp.dot(x, wgb_ref[...], preferred_element_type=F32))
        for c in range(FFN_ROW_CHUNKS):
            hp_ref[c * rc:(c + 1) * rc, :] = conv_gelu_gate(gates[c], c * rc, rc).astype(hp_ref.dtype)
        tail = pad_ref[pl.ds(lo + tm, CONV_W - 1), :]
        pad_ref[pl.ds(lo, CONV_W - 1), :] = tail
        stp_ref[...] = tail

    @pl.when(i == nm)
    def _():
        x = xs_ref[...]
        u = jnp.dot(x, wub_ref[...], preferred_element_type=F32)
        gate = jnp.dot(x, wgb_ref[...], preferred_element_type=F32)
        pieces = []
        for sq in range(n_seq_s):
            rs = slice(sq * t_s, (sq + 1) * t_s)
            pad_ref[pl.ds(CONV_PAD_ROWS, t_s), :] = u[rs]
            pad_ref[pl.ds(lo, CONV_W - 1), :] = prev_ref[sq]
            pieces.append(conv_gelu_gate(gate[rs], 0, t_s))
            sts_ref[sq] = pad_ref[pl.ds(lo + t_s, CONV_W - 1), :]
        hs_ref[...] = jnp.concatenate(pieces, axis=0).astype(hs_ref.dtype)


def conv_ffn_hidden_pair(xp, xs, w_up, w_gate, layer, prev_s, w_conv, b_conv, n_seq_p, t_p, n_seq_s, t_s):
    mp, d = xp.shape
    ms = xs.shape[0]
    f = w_up.shape[2]
    tm, tn = FFN_ROW_TILE, FFN_COL_TILE
    assert t_p % tm == 0 and f % tn == 0 and ms == n_seq_s * t_s
    nm, tps, last = mp // tm, t_p // tm, mp // tm - 1
    row = lambda j, i: (jnp.minimum(i, last), 0)
    wspec = pl.BlockSpec((None, d, tn), lambda j, i: (layer, 0, j))
    return pl.pallas_call(
        functools.partial(_ffn_pair_kernel, nm=nm, tiles_per_seq=tps, n_seq_s=n_seq_s, t_s=t_s),
        grid=(f // tn, nm + 1),
        in_specs=[
            pl.BlockSpec((tm, d), row),
            pl.BlockSpec((ms, d), lambda j, i: (0, 0)),
            wspec, wspec,
            pl.BlockSpec((None, n_seq_s, CONV_W - 1, tn), lambda j, i: (layer, 0, 0, j)),
            pl.BlockSpec((None, CONV_W, tn), lambda j, i: (layer, 0, j)),
            pl.BlockSpec((None, 1, tn), lambda j, i: (layer, 0, j)),
        ],
        out_specs=[
            pl.BlockSpec((tm, tn), lambda j, i: (jnp.minimum(i, last), j)),
            pl.BlockSpec((ms, tn), lambda j, i: (0, j)),
            pl.BlockSpec((None, CONV_W - 1, tn), lambda j, i: (jnp.minimum(i, last) // tps, 0, j)),
            pl.BlockSpec((n_seq_s, CONV_W - 1, tn), lambda j, i: (0, 0, j)),
        ],
        out_shape=[jax.ShapeDtypeStruct((mp, f), BF16), jax.ShapeDtypeStruct((ms, f), BF16),
                   jax.ShapeDtypeStruct((n_seq_p, CONV_W - 1, f), F32),
                   jax.ShapeDtypeStruct((n_seq_s, CONV_W - 1, f), F32)],
        scratch_shapes=[pltpu.VMEM((d, tn), BF16), pltpu.VMEM((d, tn), BF16),
                        pltpu.VMEM((tm + CONV_PAD_ROWS, tn), F32)],
        compiler_params=_params("parallel", "arbitrary"),
        name="conv_ffn_hidden_pair",
    )(xp, xs, w_up, w_gate, prev_s, w_conv, b_conv.reshape(b_conv.shape[0], 1, f))


def kernel(x_prompt, x_sample, cache_fox_k, cache_fox_v, cache_fox_logf, cache_win_k, cache_win_v, state_ffn_conv,
           page_table, w_in_a, b_f_a, w_o_a, g_attn, g_kv, w_kv_b, w_q_b, w_o_b, rel_bias,
           g_ffn, w_up, w_gate, w_conv, b_conv, w_down, g_final):
    bp, sp, d_model = x_prompt.shape
    bs, ts, _ = x_sample.shape
    n_heads = b_f_a.shape[1]
    hd = n_heads * HEAD_DIM
    n_kv = cache_win_k.shape[2]
    n_buf = cache_win_k.shape[1]
    kvw = n_kv * HEAD_DIM
    group_heads = w_q_b.shape[2] // (N_GROUPS * HEAD_DIM)
    d_ff = w_up.shape[2]
    n_pages = page_table.shape[1]
    assert w_in_a.shape[0] == 1 and w_q_b.shape[0] == 1 and g_attn.shape[0] == 2
    mp, ms = bp * sp, bs * ts

    b_f = b_f_a[0][None].astype(F32)
    w_down_b = w_down.astype(BF16)
    xp = x_prompt.reshape(mp, d_model)
    xs = x_sample.reshape(ms, d_model)

    def norm(x, gains, dtype=BF16):
        return rmsnorm(x, gains, dtype)

    def ffn(xp, xs, layer):
        (xnp,) = norm(xp, g_ffn[layer][None])
        (xns,) = norm(xs, g_ffn[layer][None])
        hp, hs, stp, sts = conv_ffn_hidden_pair(xnp, xns, w_up, w_gate, layer, state_ffn_conv, w_conv, b_conv,
                                                bp, sp, bs, ts)
        xp = matmul(hp, w_down_b, layer, res=xp, bm=1024, bn=512, bk=d_ff // 2)
        xs = matmul(hs, w_down_b, layer, res=xs, bm=ms, bn=1024, bk=d_ff // 2)
        return xp, xs, stp, sts

    (xnp,) = norm(xp, g_attn[0][None])
    (xns,) = norm(xs, g_attn[0][None])
    qp, qs = matmul_pair(xnp, xns, w_in_a, 0, 0, hd, out_dtype=BF16, out_scale=ATTN_SCALE)
    kp, ks = matmul_pair(xnp, xns, w_in_a, 0, hd, hd)
    vp, vs = matmul_pair(xnp, xns, w_in_a, 0, 2 * hd, hd)
    lfp = forget_gate(xnp, w_in_a, 0, 3 * hd, b_f)
    lfs = forget_gate(xns, w_in_a, 0, 3 * hd, b_f)
    cp = cumsum_rows(lfp.reshape(bp, sp, n_heads))
    cs = cumsum_rows(lfs.reshape(bs, ts, n_heads))
    ap = fox_prompt_attention(qp, kp, vp, cp.reshape(mp, n_heads), bp, sp, n_heads)
    pt_flat = page_table.reshape(-1)
    d_past = page_suffix_sums(pt_flat, cache_fox_logf[0], bs, n_pages)
    a_s = fox_sample_attention(qs, cs, d_past, ks, vs, cache_fox_k[0], cache_fox_v[0], pt_flat, bs, ts, n_heads,
                               n_pages).astype(BF16)
    xp, xs = matmul_pair(ap, a_s, w_o_a, 0, 0, d_model, res=(xp, xs))
    xp, xs, st0p, st0s = ffn(xp, xs, 0)

    xnp, xkvp = norm(xp, jnp.stack([g_attn[1], g_kv]))
    xns, xkvs = norm(xs, jnp.stack([g_attn[1], g_kv]))
    kvp, kvs = matmul_pair(xkvp, xkvs, w_kv_b[None], 0, 0, 2 * kvw)
    qp, qs = matmul_pair(xnp, xns, w_q_b, 0, 0, w_q_b.shape[2], out_scale=ATTN_SCALE)
    kbp, vbp = kvp[:, :kvw], kvp[:, kvw:]
    kbs, vbs = kvs[:, :kvw], kvs[:, kvw:]
    ap = dilated_prompt_attention(qp, kbp, vbp, rel_bias, bp, sp, n_kv, group_heads)
    a_s = dilated_sample_attention(qs, kbs, vbs, cache_win_k, cache_win_v, rel_bias, bs, ts, n_kv,
                                   group_heads).astype(BF16)
    xp, xs = matmul_pair(ap, a_s, w_o_b, 0, 0, d_model, res=(xp, xs))
    xp, xs, st1p, st1s = ffn(xp, xs, 1)
    (yp,) = norm(xp, g_final[None], F32)
    (ys,) = norm(xs, g_final[None], F32)

    n_pg = sp // PAGE_SIZE
    keep = min(BUCKET_MAX_DIST, sp)
    return (
        yp.reshape(bp, sp, d_model),
        ys.reshape(bs, ts, d_model),
        kp.reshape(1, bp, n_pg, PAGE_SIZE, n_heads, HEAD_DIM),
        vp.reshape(1, bp, n_pg, PAGE_SIZE, n_heads, HEAD_DIM),
        lfp.reshape(1, bp, n_pg, PAGE_SIZE, n_heads),
        ks.reshape(1, bs, ts, n_heads, HEAD_DIM),
        vs.reshape(1, bs, ts, n_heads, HEAD_DIM),
        lfs.reshape(1, bs, ts, n_heads),
        kbp.reshape(bp, sp, n_kv, HEAD_DIM)[:, sp - keep:],
        vbp.reshape(bp, sp, n_kv, HEAD_DIM)[:, sp - keep:],
        jnp.concatenate([cache_win_k[:, ts:], kbs.reshape(bs, ts, n_kv, HEAD_DIM)], axis=1)[:, -n_buf:],
        jnp.concatenate([cache_win_v[:, ts:], vbs.reshape(bs, ts, n_kv, HEAD_DIM)], axis=1)[:, -n_buf:],
        jnp.stack([st0p, st1p]),
        jnp.stack([st0s, st1s]),
    )
```

```python
import functools
import math

import numpy as np
import jax
import jax.numpy as jnp
from jax import lax
from jax.experimental import pallas as pl
from jax.experimental.pallas import tpu as pltpu

F32 = jnp.float32
BF16 = jnp.bfloat16

HEAD_DIM = 128
PAGE_SIZE = 128
PATTERNS = ((128, 1), (512, 4), (2048, 16))
N_GROUPS = len(PATTERNS)
N_BUCKETS = 32
BUCKET_MAX_DIST = max(w for w, _ in PATTERNS)
CONV_W = 3
Q_BLOCK = 128
NORM_EPS = 1e-6
ATTN_SCALE = HEAD_DIM ** -0.5
NEG = -1e30
SQRT_HALF = 0.7071067811865476

V7X_VMEM_LIMIT_BYTES = 56 * 1024 * 1024


def _params(*semantics):
    return pltpu.CompilerParams(dimension_semantics=semantics, vmem_limit_bytes=V7X_VMEM_LIMIT_BYTES)


def _rmsnorm_kernel(x_ref, g_ref, *o_refs):
    x = x_ref[...]
    y = x * lax.rsqrt(jnp.mean(x * x, axis=-1, keepdims=True) + NORM_EPS)
    for j, o_ref in enumerate(o_refs):
        o_ref[...] = (y * g_ref[j:j + 1, :]).astype(o_ref.dtype)


def rmsnorm(x, gains, out_dtype):
    m, d = x.shape
    n = gains.shape[0]
    tm = min(256, m)
    return pl.pallas_call(
        _rmsnorm_kernel,
        grid=(m // tm,),
        in_specs=[pl.BlockSpec((tm, d), lambda i: (i, 0)), pl.BlockSpec((n, d), lambda i: (0, 0))],
        out_specs=[pl.BlockSpec((tm, d), lambda i: (i, 0))] * n,
        out_shape=[jax.ShapeDtypeStruct((m, d), out_dtype)] * n,
        compiler_params=_params("parallel"),
        name="rmsnorm",
    )(x, gains)


def _matmul_kernel(*refs, has_res, nk, out_scale):
    x_ref, w_ref = refs[0], refs[1]
    res_ref = refs[2] if has_res else None
    o_ref = refs[2 + has_res]

    def finish(acc):
        if out_scale != 1.0:
            acc = acc * out_scale
        if has_res:
            acc = acc + res_ref[...]
        o_ref[...] = acc.astype(o_ref.dtype)

    part = jnp.dot(x_ref[...], w_ref[...], preferred_element_type=F32)
    if nk == 1:
        finish(part)
        return
    acc_ref = refs[3 + has_res]
    k = pl.program_id(2)

    @pl.when(k == 0)
    def _():
        acc_ref[...] = part

    @pl.when(jnp.logical_and(k > 0, k < nk - 1))
    def _():
        acc_ref[...] += part

    @pl.when(k == nk - 1)
    def _():
        finish(acc_ref[...] + part)


def matmul(x, w, layer, *, res=None, out_dtype=F32, out_scale=1.0, bm=1024, bn=1024, bk=None):
    m, kdim = x.shape
    n = w.shape[2]
    bm, bn = min(bm, m), min(bn, n)
    bk = kdim if bk is None else bk
    nk = kdim // bk
    assert m % bm == 0 and n % bn == 0 and kdim % bk == 0 and nk >= 1
    in_specs = [pl.BlockSpec((bm, bk), lambda i, j, k: (i, k)),
                pl.BlockSpec((None, bk, bn), lambda i, j, k: (layer, k, j))]
    args = [x, w]
    if res is not None:
        in_specs.append(pl.BlockSpec((bm, bn), lambda i, j, k: (i, j)))
        args.append(res)
    return pl.pallas_call(
        functools.partial(_matmul_kernel, has_res=res is not None, nk=nk, out_scale=out_scale),
        grid=(m // bm, n // bn, nk),
        in_specs=in_specs,
        out_specs=pl.BlockSpec((bm, bn), lambda i, j, k: (i, j)),
        out_shape=jax.ShapeDtypeStruct((m, n), out_dtype),
        scratch_shapes=[pltpu.VMEM((bm, bn), F32)] if nk > 1 else [],
        compiler_params=_params("parallel", "parallel", "arbitrary"),
        name="matmul",
    )(*args)


def _pair_matmul_kernel(*refs, nm, has_res, out_scale, transposed):
    if has_res:
        xp_ref, xs_ref, w_ref, rp_ref, rs_ref, op_ref, os_ref, wb_ref = refs
    else:
        xp_ref, xs_ref, w_ref, op_ref, os_ref, wb_ref = refs
        rp_ref = rs_ref = None
    i = pl.program_id(1)

    @pl.when(i == 0)
    def _():
        wb_ref[...] = w_ref[...].astype(BF16)

    def emit(x_ref, r_ref, o_ref):
        dims = (((1,), (1 if transposed else 0,)), ((), ()))
        acc = lax.dot_general(x_ref[...], wb_ref[...], dims, preferred_element_type=F32)
        if out_scale != 1.0:
            acc = acc * out_scale
        if r_ref is not None:
            acc = acc + r_ref[...]
        o_ref[...] = acc.astype(o_ref.dtype)

    @pl.when(i < nm)
    def _():
        emit(xp_ref, rp_ref, op_ref)

    @pl.when(i == nm)
    def _():
        emit(xs_ref, rs_ref, os_ref)


def matmul_pair(xp, xs, w, layer, col0, n, *, res=None, out_dtype=F32, out_scale=1.0, transposed=False,
                bm=1024, bn=512):
    mp, kdim = xp.shape
    ms = xs.shape[0]
    assert mp % bm == 0 and n % bn == 0 and col0 % bn == 0 and w.shape[2 if transposed else 1] == kdim
    nm, cb0, last = mp // bm, col0 // bn, mp // bm - 1
    row = lambda j, i: (jnp.minimum(i, last), 0)
    tile = lambda j, i: (jnp.minimum(i, last), j)
    if transposed:
        w_spec = pl.BlockSpec((None, bn, kdim), lambda j, i: (layer, cb0 + j, 0))
    else:
        w_spec = pl.BlockSpec((None, kdim, bn), lambda j, i: (layer, 0, cb0 + j))
    in_specs = [pl.BlockSpec((bm, kdim), row), pl.BlockSpec((ms, kdim), lambda j, i: (0, 0)), w_spec]
    args = [xp, xs, w]
    if res is not None:
        in_specs += [pl.BlockSpec((bm, bn), tile), pl.BlockSpec((ms, bn), lambda j, i: (0, j))]
        args += list(res)
    return pl.pallas_call(
        functools.partial(_pair_matmul_kernel, nm=nm, has_res=res is not None, out_scale=out_scale,
                          transposed=transposed),
        grid=(n // bn, nm + 1),
        in_specs=in_specs,
        out_specs=[pl.BlockSpec((bm, bn), tile), pl.BlockSpec((ms, bn), lambda j, i: (0, j))],
        out_shape=[jax.ShapeDtypeStruct((mp, n), out_dtype), jax.ShapeDtypeStruct((ms, n), out_dtype)],
        scratch_shapes=[pltpu.VMEM((bn, kdim) if transposed else (kdim, bn), BF16)],
        compiler_params=_params("parallel", "arbitrary"),
        name="matmul_pair",
    )(*args)


def _gate_kernel(x_ref, w_ref, b_ref, o_ref):
    z = lax.dot_general(x_ref[...], w_ref[...].astype(BF16), (((1,), (1,)), ((), ())),
                        preferred_element_type=F32) + b_ref[...]
    o_ref[...] = -(jnp.maximum(-z, 0.0) + jnp.log1p(jnp.exp(-jnp.abs(z))))


def forget_gate(xn, w_t, layer, row0, b_f):
    m, d = xn.shape
    nh = b_f.shape[1]
    tm = min(512, m)
    assert row0 % nh == 0 and nh % 8 == 0
    return pl.pallas_call(
        _gate_kernel,
        grid=(m // tm,),
        in_specs=[pl.BlockSpec((tm, d), lambda i: (i, 0)),
                  pl.BlockSpec((None, nh, d), lambda i: (layer, row0 // nh, 0)),
                  pl.BlockSpec((1, nh), lambda i: (0, 0))],
        out_specs=pl.BlockSpec((tm, nh), lambda i: (i, 0)),
        out_shape=jax.ShapeDtypeStruct((m, nh), F32),
        compiler_params=_params("parallel"),
        name="forget_gate",
    )(xn, w_t, b_f)


def _split_dot(tri, x):
    hi = x.astype(BF16)
    r1 = x - hi.astype(F32)
    mid = r1.astype(BF16)
    lo = (r1 - mid.astype(F32)).astype(BF16)
    out = jnp.dot(tri, lo, preferred_element_type=F32)
    out = out + jnp.dot(tri, mid, preferred_element_type=F32)
    return out + jnp.dot(tri, hi, preferred_element_type=F32)


def _cumsum_kernel(x_ref, o_ref, *, chunk):
    s = x_ref.shape[0]
    row = lax.broadcasted_iota(jnp.int32, (chunk, chunk), 0)
    col = lax.broadcasted_iota(jnp.int32, (chunk, chunk), 1)
    tri = jnp.where(col <= row, 1.0, 0.0).astype(BF16)
    carry = jnp.zeros((1, x_ref.shape[1]), F32)
    for i in range(s // chunk):
        y = _split_dot(tri, x_ref[i * chunk:(i + 1) * chunk, :]) + carry
        o_ref[i * chunk:(i + 1) * chunk, :] = y
        carry = y[chunk - 1:chunk, :]


def _cumsum_small_kernel(x_ref, o_ref):
    x = x_ref[...]
    t = x.shape[0]
    row = lax.broadcasted_iota(jnp.int32, x.shape, 0)
    for i in range(t):
        o_ref[i:i + 1, :] = jnp.sum(jnp.where(row <= i, x, 0.0), axis=0, keepdims=True)


def cumsum_rows(x):
    b, s, nh = x.shape
    body = _cumsum_small_kernel if s < 128 else functools.partial(_cumsum_kernel, chunk=min(256, s))
    return pl.pallas_call(
        body,
        grid=(b,),
        in_specs=[pl.BlockSpec((None, s, nh), lambda i: (i, 0, 0))],
        out_specs=pl.BlockSpec((None, s, nh), lambda i: (i, 0, 0)),
        out_shape=jax.ShapeDtypeStruct((b, s, nh), F32),
        compiler_params=_params("parallel"),
        name="cumsum_rows",
    )(x)


SUFFIX_PAGES_PER_STEP = 16


def _page_suffix_kernel(pt_ref, *refs, pps):
    lf_refs, o_ref, carry_ref = refs[:pps], refs[pps], refs[pps + 1]
    j = pl.program_id(1)
    n = lf_refs[0].shape[0]

    @pl.when(j == 0)
    def _():
        carry_ref[...] = jnp.zeros_like(carry_ref)

    row = lax.broadcasted_iota(jnp.int32, (n, n), 0)
    col = lax.broadcasted_iota(jnp.int32, (n, n), 1)
    upper = jnp.where(col > row, 1.0, 0.0).astype(BF16)
    carry = carry_ref[...]
    for pp in reversed(range(pps)):
        x = lf_refs[pp][...]
        o_ref[pp] = _split_dot(upper, x) + carry
        carry = carry + jnp.sum(x, axis=0, keepdims=True)
    carry_ref[...] = carry


def page_suffix_sums(page_table_flat, lf_pool, b, n_pages):
    _, n, nh = lf_pool.shape
    pps = SUFFIX_PAGES_PER_STEP
    assert n_pages % pps == 0
    n_steps = n_pages // pps

    def page_spec(pp):
        return pl.BlockSpec((None, n, nh), lambda i, j, pt: (pt[i * n_pages + n_pages - (j + 1) * pps + pp], 0, 0))

    grid_spec = pltpu.PrefetchScalarGridSpec(
        num_scalar_prefetch=1,
        grid=(b, n_steps),
        in_specs=[page_spec(pp) for pp in range(pps)],
        out_specs=pl.BlockSpec((None, pps, n, nh), lambda i, j, pt: (i, n_steps - 1 - j, 0, 0)),
        scratch_shapes=[pltpu.VMEM((1, nh), F32)],
    )
    return pl.pallas_call(
        functools.partial(_page_suffix_kernel, pps=pps),
        grid_spec=grid_spec,
        out_shape=jax.ShapeDtypeStruct((b, n_pages, n, nh), F32),
        compiler_params=_params("parallel", "arbitrary"),
        name="page_suffix_sums",
    )(page_table_flat, *([lf_pool] * pps))


def _fox_prompt_kernel(q_ref, k_ref, v_ref, c_ref, ct_ref, o_ref, *, nq):
    tq = q_ref.shape[0]
    h = pl.program_id(1)
    i = pl.program_id(2)
    q = q_ref[...]
    lane = lax.broadcasted_iota(jnp.int32, c_ref.shape, 1)
    cq = jnp.sum(jnp.where(lane == h, c_ref[...], 0.0), axis=-1, keepdims=True)
    dn = (((1,), (1,)), ((), ()))

    def scores(j):
        kb = k_ref[j * tq:(j + 1) * tq, :].astype(BF16)
        return lax.dot_general(q, kb, dn, preferred_element_type=F32) + (cq - ct_ref[j:j + 1, :])

    def update(j, s, carry, masked):
        m, l, acc = carry
        if masked:
            row = lax.broadcasted_iota(jnp.int32, s.shape, 0)
            col = lax.broadcasted_iota(jnp.int32, s.shape, 1)
            s = jnp.where(col <= row, s, NEG)
        m_new = jnp.maximum(m, jnp.max(s, axis=-1, keepdims=True))
        alpha = jnp.exp(m - m_new)
        p = jnp.exp(s - m_new)
        l = l * alpha + jnp.sum(p, axis=-1, keepdims=True)
        vb = v_ref[j * tq:(j + 1) * tq, :].astype(BF16)
        acc = acc * alpha + jnp.dot(p.astype(BF16), vb, preferred_element_type=F32)
        return m_new, l, acc

    for ii in range(nq):

        @pl.when(i == ii)
        def _(ii=ii):
            carry = (jnp.full((tq, 1), NEG, F32), jnp.zeros((tq, 1), F32), jnp.zeros((tq, HEAD_DIM), F32))
            s_next = scores(0)
            for j in range(ii + 1):
                s_cur = s_next
                if j < ii:
                    s_next = scores(j + 1)
                carry = update(j, s_cur, carry, masked=(j == ii))
            _, l, acc = carry
            o_ref[...] = (acc / l).astype(o_ref.dtype)


def fox_prompt_attention(q, k, v, c, b, s, n_heads, tq=512):
    nq = s // tq
    ct = c.reshape(b, s, n_heads).transpose(0, 2, 1).reshape(b * n_heads, nq, tq)
    return pl.pallas_call(
        functools.partial(_fox_prompt_kernel, nq=nq),
        grid=(b, n_heads, nq),
        in_specs=[
            pl.BlockSpec((tq, HEAD_DIM), lambda bi, h, i: (bi * nq + i, h)),
            pl.BlockSpec((s, HEAD_DIM), lambda bi, h, i: (bi, h)),
            pl.BlockSpec((s, HEAD_DIM), lambda bi, h, i: (bi, h)),
            pl.BlockSpec((tq, n_heads), lambda bi, h, i: (bi * nq + i, 0)),
            pl.BlockSpec((None, nq, tq), lambda bi, h, i: (bi * n_heads + h, 0, 0)),
        ],
        out_specs=pl.BlockSpec((tq, HEAD_DIM), lambda bi, h, i: (bi * nq + i, h)),
        out_shape=jax.ShapeDtypeStruct(q.shape, BF16),
        compiler_params=_params("parallel", "parallel", "arbitrary"),
        name="fox_prompt_attention",
    )(q, k, v, c, ct)


HEADS_PER_TILE = 8


FOX_PAGES_PER_STEP = 2


def _tile_rows(ref, sub):
    rows, n_sub, dh = ref.shape
    return ref.reshape(rows * n_sub, dh)[pl.ds(sub, rows, stride=n_sub), :]


def _fox_sample_kernel(pt_ref, q_ref, cn_ref, cnt_ref, dt_ref, kn_ref, vn_ref, *refs, n_heads, n_steps, pps):
    ng = n_heads // HEADS_PER_TILE
    nb = ng * pps
    k_refs, v_refs = refs[:nb], refs[nb:2 * nb]
    o_ref, m_ref, l_ref, acc_ref = refs[2 * nb:]
    j = pl.program_id(1)
    t = q_ref.shape[1]
    dn = (((1,), (1,)), ((), ()))

    @pl.when(j == 0)
    def _():
        m_ref[...] = jnp.full_like(m_ref, NEG)
        l_ref[...] = jnp.zeros_like(l_ref)
        acc_ref[...] = jnp.zeros_like(acc_ref)

    def head_rows(page_refs, h):
        g, hs = divmod(h, HEADS_PER_TILE)
        tiles = [_tile_rows(page_refs[pp * ng + g], hs).astype(BF16) for pp in range(pps)]
        return tiles[0] if pps == 1 else jnp.concatenate(tiles, axis=0)

    def online_update(s, pv):
        m_old = m_ref[...]
        m_new = jnp.maximum(m_old, jnp.max(s, axis=-1, keepdims=True))
        alpha = jnp.exp(m_old - m_new)
        p = jnp.exp(s - m_new)
        l_ref[...] = l_ref[...] * alpha + jnp.sum(p, axis=-1, keepdims=True)
        acc_ref[...] = acc_ref[...] * alpha + pv(p)
        m_ref[...] = m_new

    parts = []
    for h in range(n_heads):
        sh = lax.dot_general(q_ref[h], head_rows(k_refs, h), dn, preferred_element_type=F32)
        d_row = [dt_ref[pp, h:h + 1, :] for pp in range(pps)]
        parts.append(sh + (d_row[0] if pps == 1 else jnp.concatenate(d_row, axis=1)))
    cn = cn_ref[...]
    s = jnp.concatenate(parts, axis=0) + (cn if pps == 1 else jnp.concatenate([cn] * pps, axis=1))

    def pv_pages(p):
        return jnp.concatenate(
            [jnp.dot(p[h * t:(h + 1) * t].astype(BF16), head_rows(v_refs, h), preferred_element_type=F32)
             for h in range(n_heads)], axis=0)

    online_update(s, pv_pages)

    @pl.when(j == n_steps - 1)
    def _():
        parts = []
        for h in range(n_heads):
            kh = kn_ref[:, h * HEAD_DIM:(h + 1) * HEAD_DIM].astype(BF16)
            parts.append(lax.dot_general(q_ref[h], kh, dn, preferred_element_type=F32))
        s2 = jnp.concatenate(parts, axis=0) + cn_ref[:, :t] - cnt_ref[...]
        row = lax.broadcasted_iota(jnp.int32, s2.shape, 0)
        col = lax.broadcasted_iota(jnp.int32, s2.shape, 1)
        s2 = jnp.where(col <= lax.rem(row, t), s2, NEG)

        def pv_new(p):
            return jnp.concatenate(
                [jnp.dot(p[h * t:(h + 1) * t], vn_ref[:, h * HEAD_DIM:(h + 1) * HEAD_DIM], preferred_element_type=F32)
                 for h in range(n_heads)], axis=0)

        online_update(s2, pv_new)
        o_ref[...] = acc_ref[...] / l_ref[...]


def fox_sample_attention(q, c_new, d_past, k_new, v_new, k_pool, v_pool, page_table_flat, b, t, n_heads, n_pages):
    ng = n_heads // HEADS_PER_TILE
    pps = FOX_PAGES_PER_STEP
    assert n_pages % pps == 0
    n_steps = n_pages // pps
    n_pool = k_pool.shape[0]
    ht = n_heads * t
    q4 = q.reshape(b, t, n_heads, HEAD_DIM).transpose(0, 2, 1, 3)
    c_ht = c_new.transpose(0, 2, 1).reshape(b, ht, 1)
    cn = jnp.broadcast_to(c_ht, (b, ht, PAGE_SIZE))
    cnt = jnp.broadcast_to(c_new.transpose(0, 2, 1)[:, :, None, :], (b, n_heads, t, t)).reshape(b, ht, t)
    dt = d_past.transpose(0, 1, 3, 2)
    kp = k_pool.reshape(n_pool, PAGE_SIZE, ng, HEADS_PER_TILE, HEAD_DIM)
    vp = v_pool.reshape(n_pool, PAGE_SIZE, ng, HEADS_PER_TILE, HEAD_DIM)

    def page_spec(pp, g):
        return pl.BlockSpec((None, PAGE_SIZE, None, HEADS_PER_TILE, HEAD_DIM),
                            lambda i, j, pt: (pt[i * n_pages + j * pps + pp], 0, g, 0, 0))

    page_specs = [page_spec(pp, g) for pp in range(pps) for g in range(ng)]
    d = n_heads * HEAD_DIM
    grid_spec = pltpu.PrefetchScalarGridSpec(
        num_scalar_prefetch=1,
        grid=(b, n_steps),
        in_specs=[
            pl.BlockSpec((None, n_heads, t, HEAD_DIM), lambda i, j, pt: (i, 0, 0, 0)),
            pl.BlockSpec((None, ht, PAGE_SIZE), lambda i, j, pt: (i, 0, 0)),
            pl.BlockSpec((None, ht, t), lambda i, j, pt: (i, 0, 0)),
            pl.BlockSpec((None, pps, n_heads, PAGE_SIZE), lambda i, j, pt: (i, j, 0, 0)),
            pl.BlockSpec((t, d), lambda i, j, pt: (i, 0)),
            pl.BlockSpec((t, d), lambda i, j, pt: (i, 0)),
        ] + page_specs * 2,
        out_specs=pl.BlockSpec((None, ht, HEAD_DIM), lambda i, j, pt: (i, 0, 0)),
        scratch_shapes=[pltpu.VMEM((ht, 1), F32), pltpu.VMEM((ht, 1), F32), pltpu.VMEM((ht, HEAD_DIM), F32)],
    )
    o = pl.pallas_call(
        functools.partial(_fox_sample_kernel, n_heads=n_heads, n_steps=n_steps, pps=pps),
        grid_spec=grid_spec,
        out_shape=jax.ShapeDtypeStruct((b, ht, HEAD_DIM), F32),
        compiler_params=_params("parallel", "arbitrary"),
        name="fox_sample_attention",
    )(page_table_flat, q4, cn, cnt, dt, k_new, v_new, *([kp] * (ng * pps)), *([vp] * (ng * pps)))
    return o.reshape(b, n_heads, t, HEAD_DIM).transpose(0, 2, 1, 3).reshape(b * t, d)


def _t5_bucket(dist):
    n = np.asarray(dist, dtype=np.int64)
    exact = N_BUCKETS // 2
    large = exact + (np.log(np.maximum(n, 1) / exact) / np.log(BUCKET_MAX_DIST / exact) * (N_BUCKETS - exact)).astype(np.int64)
    return np.where(n < exact, n, np.minimum(large, N_BUCKETS - 1)).astype(np.int32)


def _band_bias(rel_bias, group_heads):
    a = np.arange(Q_BLOCK)[:, None]
    bcol = np.arange(2 * Q_BLOCK)[None, :]
    j = a - bcol + Q_BLOCK
    out = []
    for g, (window, dil) in enumerate(PATTERNS):
        assert window // dil == Q_BLOCK
        valid = (j >= 0) & (j <= window // dil)
        bucket = _t5_bucket(dil * np.clip(j, 0, window // dil))
        bias = _select_buckets(rel_bias[:, g * group_heads:(g + 1) * group_heads], bucket)
        out.append(jnp.where(valid[None], bias, NEG))
    return jnp.stack(out)


def _select_buckets(rb, bucket):
    onehot = (jnp.asarray(bucket.reshape(-1))[None, :] == jnp.arange(N_BUCKETS)[:, None]).astype(F32)
    sel = jnp.einsum("bh,bn->hn", rb.astype(F32), onehot, precision=lax.Precision.HIGHEST)
    return sel.reshape((rb.shape[1],) + bucket.shape)


DILATED_BLOCKS_PER_STEP = 4


def _dilated_prompt_kernel(*refs, s):
    n_q = N_GROUPS * 2
    q_refs = refs[:n_q]
    k_ref, v_ref, bias_ref, o_ref, m_ref, l_ref, acc_ref = refs[n_q:]
    qb_rows = Q_BLOCK
    dn = (((1,), (1,)), ((), ()))

    def blocks(g, specs, merge):
        bias = bias_ref[g].reshape(2 * qb_rows, 2 * qb_rows)
        scores = []
        for q_idx, k_idx, first in specs:
            q2 = jnp.concatenate([q_refs[2 * g + r][q_idx, :].astype(BF16) for r in range(2)], axis=0)
            sc = lax.dot_general(q2, k_ref[k_idx, :].astype(BF16), dn, preferred_element_type=F32)
            scores.append(sc + (bias[:, qb_rows:] if first else bias))
        stats = []
        for sc in scores:
            m_blk = jnp.max(sc, axis=-1, keepdims=True)
            p = jnp.exp(sc - m_blk)
            stats.append((m_blk, jnp.sum(p, axis=-1, keepdims=True), p.astype(BF16)))
        outs = [jnp.dot(p, v_ref[k_idx, :].astype(BF16), preferred_element_type=F32)
                for (_, _, p), (_, k_idx, _) in zip(stats, specs)]
        for (m_blk, l_blk, _), o_blk, (q_idx, _, _) in zip(stats, outs, specs):
            for r in range(2):
                rs = slice(r * qb_rows, (r + 1) * qb_rows)
                m_b = jnp.broadcast_to(m_blk[rs], (qb_rows, HEAD_DIM))
                l_b = jnp.broadcast_to(l_blk[rs], (qb_rows, HEAD_DIM))
                if merge:
                    m_old = m_ref[r, q_idx, :]
                    m_new = jnp.maximum(m_old, m_b)
                    a_old = jnp.exp(m_old - m_new)
                    a_blk = jnp.exp(m_b - m_new)
                    l_ref[r, q_idx, :] = l_ref[r, q_idx, :] * a_old + l_b * a_blk
                    acc_ref[r, q_idx, :] = acc_ref[r, q_idx, :] * a_old + o_blk[rs] * a_blk
                    m_ref[r, q_idx, :] = m_new
                else:
                    m_ref[r, q_idx, :] = m_b
                    l_ref[r, q_idx, :] = l_b
                    acc_ref[r, q_idx, :] = o_blk[rs]

    per_step = DILATED_BLOCKS_PER_STEP
    for g, (_, dil) in enumerate(PATTERNS):
        n_blocks = s // (dil * qb_rows)
        merge = g > 0

        def idx(start, size, dil=dil):
            return pl.ds(start, size) if dil == 1 else pl.ds(start, size, stride=dil)

        def spec(c, qb, dil=dil, idx=idx):
            q0 = c + dil * qb_rows * qb
            if isinstance(qb, int) and qb == 0:
                return idx(q0, qb_rows), idx(q0, qb_rows), True
            return idx(q0, qb_rows), idx(q0 - dil * qb_rows, 2 * qb_rows), False

        if n_blocks == 1:
            assert dil % per_step == 0

            def classes(it, carry, g=g, merge=merge, spec=spec):
                blocks(g, [spec(it * per_step + u, 0) for u in range(per_step)], merge)
                return carry

            lax.fori_loop(0, dil // per_step, classes, 0)
        elif n_blocks <= per_step:

            def one_class(c, carry, g=g, merge=merge, spec=spec, n_blocks=n_blocks):
                blocks(g, [spec(c, qb) for qb in range(n_blocks)], merge)
                return carry

            lax.fori_loop(0, dil, one_class, 0)
        else:
            assert dil == 1 and (n_blocks - 1) % (per_step - 1) == 0
            blocks(g, [spec(0, 0)], merge)
            grp = per_step - 1

            def later(it, carry, g=g, merge=merge, spec=spec, grp=grp):
                blocks(g, [spec(0, 1 + it * grp + u) for u in range(grp)], merge)
                return carry

            lax.fori_loop(0, (n_blocks - 1) // grp, later, 0)

    for r in range(2):
        o_ref[:, r * HEAD_DIM:(r + 1) * HEAD_DIM] = (acc_ref[r] / l_ref[r]).astype(o_ref.dtype)


def dilated_prompt_attention(q, k, v, rel_bias, b, s, n_kv, group_heads):
    assert group_heads == 2 * n_kv
    bias = _band_bias(rel_bias, group_heads).reshape(N_GROUPS, n_kv, 2, Q_BLOCK, 2 * Q_BLOCK)

    def q_spec(g, r):
        return pl.BlockSpec((s, HEAD_DIM), lambda bi, kv: (bi, g * group_heads + 2 * kv + r))

    return pl.pallas_call(
        functools.partial(_dilated_prompt_kernel, s=s),
        grid=(b, n_kv),
        in_specs=[q_spec(g, r) for g in range(N_GROUPS) for r in range(2)] + [
            pl.BlockSpec((s, HEAD_DIM), lambda bi, kv: (bi, kv)),
            pl.BlockSpec((s, HEAD_DIM), lambda bi, kv: (bi, kv)),
            pl.BlockSpec((N_GROUPS, None, 2, Q_BLOCK, 2 * Q_BLOCK), lambda bi, kv: (0, kv, 0, 0, 0)),
        ],
        out_specs=pl.BlockSpec((s, 2 * HEAD_DIM), lambda bi, kv: (bi, kv)),
        out_shape=jax.ShapeDtypeStruct((b * s, group_heads * HEAD_DIM), BF16),
        scratch_shapes=[pltpu.VMEM((2, s, HEAD_DIM), F32)] * 3,
        compiler_params=_params("parallel", "parallel"),
        name="dilated_prompt_attention",
    )(*([q] * (N_GROUPS * 2)), k, v, bias)


def _sample_bias(rel_bias, group_heads, n_kv, t, n_buf):
    dist = n_buf + np.arange(t)[:, None] - np.arange(n_buf + t)[None, :]
    out = []
    for g, (window, dil) in enumerate(PATTERNS):
        valid = (dist >= 0) & (dist % dil == 0) & (dist <= window)
        bucket = _t5_bucket(np.clip(dist, 0, window))
        bias = _select_buckets(rel_bias[:, g * group_heads:(g + 1) * group_heads], bucket)
        out.append(jnp.where(valid[None], bias, NEG).reshape(n_kv, 2, t, n_buf + t))
    return jnp.stack(out, axis=1).reshape(n_kv, N_GROUPS * 2 * t, n_buf + t)


def _dilated_sample_kernel(q_ref, kc_ref, vc_ref, kn_ref, vn_ref, bc_ref, bn_ref, o_ref, *, n_kv, group_heads):
    t = q_ref.shape[0]
    n_buf = kc_ref.shape[0]
    for kv in range(n_kv):
        heads = [g * group_heads + 2 * kv + r for g in range(N_GROUPS) for r in range(2)]
        q6 = jnp.concatenate([q_ref[:, h * HEAD_DIM:(h + 1) * HEAD_DIM] for h in heads], axis=0).astype(BF16)
        kc = _tile_rows(kc_ref, kv).astype(BF16)
        vc = _tile_rows(vc_ref, kv).astype(BF16)
        sl = slice(kv * HEAD_DIM, (kv + 1) * HEAD_DIM)
        kn = kn_ref[:, sl].astype(BF16)
        vn = vn_ref[:, sl]
        dn = (((1,), (1,)), ((), ()))
        s_c = lax.dot_general(q6, kc, dn, preferred_element_type=F32) + bc_ref[kv]
        s_n = lax.dot_general(q6, kn, dn, preferred_element_type=F32) + bn_ref[kv]
        m_row = jnp.maximum(jnp.max(s_c, axis=-1, keepdims=True), jnp.max(s_n, axis=-1, keepdims=True))
        m = jnp.max(m_row.reshape(N_GROUPS, 2 * t, 1), axis=0)
        m_all = jnp.concatenate([m] * N_GROUPS, axis=0)
        p_c = jnp.exp(s_c - m_all)
        p_n = jnp.exp(s_n - m_all)
        l_row = jnp.sum(p_c, axis=-1, keepdims=True) + jnp.sum(p_n, axis=-1, keepdims=True)
        o_row = jnp.dot(p_c.astype(BF16), vc, preferred_element_type=F32) + jnp.dot(p_n, vn, preferred_element_type=F32)
        l = jnp.sum(l_row.reshape(N_GROUPS, 2 * t, 1), axis=0)
        o = jnp.sum(o_row.reshape(N_GROUPS, 2 * t, HEAD_DIM), axis=0) / l
        for r in range(2):
            o_ref[:, (2 * kv + r) * HEAD_DIM:(2 * kv + r + 1) * HEAD_DIM] = o[r * t:(r + 1) * t]


def dilated_sample_attention(q, k_new, v_new, cache_k, cache_v, rel_bias, b, t, n_kv, group_heads):
    n_buf = cache_k.shape[1]
    bias = _sample_bias(rel_bias, group_heads, n_kv, t, n_buf)
    bias_c, bias_n = bias[:, :, :n_buf], bias[:, :, n_buf:]
    dq = q.shape[1]
    dkv = n_kv * HEAD_DIM
    return pl.pallas_call(
        functools.partial(_dilated_sample_kernel, n_kv=n_kv, group_heads=group_heads),
        grid=(b,),
        in_specs=[
            pl.BlockSpec((t, dq), lambda i: (i, 0)),
            pl.BlockSpec((None, n_buf, n_kv, HEAD_DIM), lambda i: (i, 0, 0, 0)),
            pl.BlockSpec((None, n_buf, n_kv, HEAD_DIM), lambda i: (i, 0, 0, 0)),
            pl.BlockSpec((t, dkv), lambda i: (i, 0)),
            pl.BlockSpec((t, dkv), lambda i: (i, 0)),
            pl.BlockSpec(bias_c.shape, lambda i: (0, 0, 0)),
            pl.BlockSpec(bias_n.shape, lambda i: (0, 0, 0)),
        ],
        out_specs=pl.BlockSpec((t, group_heads * HEAD_DIM), lambda i: (i, 0)),
        out_shape=jax.ShapeDtypeStruct((b * t, group_heads * HEAD_DIM), F32),
        compiler_params=_params("parallel"),
        name="dilated_sample_attention",
    )(q, cache_k, cache_v, k_new, v_new, bias_c, bias_n)


CONV_PAD_ROWS = 8


FFN_COL_TILE = 256
FFN_ROW_CHUNKS = 4


def _ffn_hidden_kernel(x_ref, wu_ref, wg_ref, prev_ref, wc_ref, bc_ref, h_ref, st_ref, pad_ref, *, n_seq, t):
    lo = CONV_PAD_ROWS - (CONV_W - 1)
    chunks = FFN_ROW_CHUNKS if t % (FFN_ROW_CHUNKS * 128) == 0 else 1
    rc = t // chunks

    def conv_gelu_gate(gate, start, rows):
        uc = bc_ref[...] + pad_ref[pl.ds(lo + start, rows), :] * wc_ref[0:1, :]
        for tap in range(1, CONV_W):
            uc = uc + pad_ref[pl.ds(lo + start + tap, rows), :] * wc_ref[tap:tap + 1, :]
        return 0.5 * uc * (1.0 + lax.erf(uc * SQRT_HALF)) * gate

    pieces = []
    for sq in range(n_seq):
        pad_ref[pl.ds(lo, CONV_W - 1), :] = prev_ref[sq]
        gates = []
        for c in range(chunks):
            x = x_ref[sq * t + c * rc:sq * t + (c + 1) * rc, :]
            pad_ref[pl.ds(CONV_PAD_ROWS + c * rc, rc), :] = jnp.dot(x, wu_ref[...], preferred_element_type=F32)
            gates.append(jnp.dot(x, wg_ref[...], preferred_element_type=F32))
        for c in range(chunks):
            hh = conv_gelu_gate(gates[c], c * rc, rc)
            if n_seq == 1:
                h_ref[c * rc:(c + 1) * rc, :] = hh.astype(h_ref.dtype)
            else:
                pieces.append(hh)
        st_ref[sq] = pad_ref[pl.ds(lo + t, CONV_W - 1), :]
    if n_seq > 1:
        h_ref[...] = jnp.concatenate(pieces, axis=0).astype(h_ref.dtype)


def conv_ffn_hidden(xn, w_up, w_gate, layer, prev, w_conv, b_conv, n_seq, t, seq_per_tile):
    m, d = xn.shape
    f = w_up.shape[2]
    tn = FFN_COL_TILE
    tm = seq_per_tile * t
    wspec = pl.BlockSpec((None, d, tn), lambda i, j: (layer, 0, j))
    return pl.pallas_call(
        functools.partial(_ffn_hidden_kernel, n_seq=seq_per_tile, t=t),
        grid=(n_seq // seq_per_tile, f // tn),
        in_specs=[
            pl.BlockSpec((tm, d), lambda i, j: (i, 0)),
            wspec, wspec,
            pl.BlockSpec((seq_per_tile, CONV_W - 1, tn), lambda i, j: (i, 0, j)),
            pl.BlockSpec((None, CONV_W, tn), lambda i, j: (layer, 0, j)),
            pl.BlockSpec((None, 1, tn), lambda i, j: (layer, 0, j)),
        ],
        out_specs=[
            pl.BlockSpec((tm, tn), lambda i, j: (i, j)),
            pl.BlockSpec((seq_per_tile, CONV_W - 1, tn), lambda i, j: (i, 0, j)),
        ],
        out_shape=[jax.ShapeDtypeStruct((m, f), BF16), jax.ShapeDtypeStruct((n_seq, CONV_W - 1, f), F32)],
        scratch_shapes=[pltpu.VMEM((t + CONV_PAD_ROWS, tn), F32)],
        compiler_params=_params("parallel", "parallel"),
        name="conv_ffn_hidden",
    )(xn, w_up, w_gate, prev, w_conv, b_conv.reshape(b_conv.shape[0], 1, f))


def kernel(x_prompt, x_sample, cache_fox_k, cache_fox_v, cache_fox_logf, cache_win_k, cache_win_v, state_ffn_conv,
           page_table, w_in_a, b_f_a, w_o_a, g_attn, g_kv, w_kv_b, w_q_b, w_o_b, rel_bias,
           g_ffn, w_up, w_gate, w_conv, b_conv, w_down, g_final):
    bp, sp, d_model = x_prompt.shape
    bs, ts, _ = x_sample.shape
    n_heads = b_f_a.shape[1]
    hd = n_heads * HEAD_DIM
    n_kv = cache_win_k.shape[2]
    n_buf = cache_win_k.shape[1]
    kvw = n_kv * HEAD_DIM
    group_heads = w_q_b.shape[2] // (N_GROUPS * HEAD_DIM)
    d_ff = w_up.shape[2]
    n_pages = page_table.shape[1]
    assert w_in_a.shape[0] == 1 and w_q_b.shape[0] == 1 and g_attn.shape[0] == 2
    mp, ms = bp * sp, bs * ts

    b_f = b_f_a[0][None].astype(F32)
    w_in_t = jnp.swapaxes(w_in_a, 1, 2)
    w_up_b, w_gate_b, w_down_b = w_up.astype(BF16), w_gate.astype(BF16), w_down.astype(BF16)
    zero_state = jnp.zeros((bp, CONV_W - 1, d_ff), F32)
    xp = x_prompt.reshape(mp, d_model)
    xs = x_sample.reshape(ms, d_model)

    def norm(x, gains, dtype=BF16):
        return rmsnorm(x, gains, dtype)

    def ffn(xp, xs, layer):
        (xnp,) = norm(xp, g_ffn[layer][None])
        (xns,) = norm(xs, g_ffn[layer][None])
        hp, stp = conv_ffn_hidden(xnp, w_up_b, w_gate_b, layer, zero_state, w_conv, b_conv, bp, sp, 1)
        hs, sts = conv_ffn_hidden(xns, w_up_b, w_gate_b, layer, state_ffn_conv[layer], w_conv, b_conv, bs, ts, bs)
        xp = matmul(hp, w_down_b, layer, res=xp, bm=1024, bn=512, bk=d_ff // 2)
        xs = matmul(hs, w_down_b, layer, res=xs, bm=ms, bn=1024, bk=d_ff // 2)
        return xp, xs, stp, sts

    (xnp,) = norm(xp, g_attn[0][None])
    (xns,) = norm(xs, g_attn[0][None])
    qp, qs = matmul_pair(xnp, xns, w_in_t, 0, 0, hd, out_dtype=BF16, out_scale=ATTN_SCALE, transposed=True)
    kp, ks = matmul_pair(xnp, xns, w_in_t, 0, hd, hd, transposed=True)
    vp, vs = matmul_pair(xnp, xns, w_in_t, 0, 2 * hd, hd, transposed=True)
    lfp = forget_gate(xnp, w_in_t, 0, 3 * hd, b_f)
    lfs = forget_gate(xns, w_in_t, 0, 3 * hd, b_f)
    cp = cumsum_rows(lfp.reshape(bp, sp, n_heads))
    cs = cumsum_rows(lfs.reshape(bs, ts, n_heads))
    ap = fox_prompt_attention(qp, kp, vp, cp.reshape(mp, n_heads), bp, sp, n_heads)
    pt_flat = page_table.reshape(-1)
    d_past = page_suffix_sums(pt_flat, cache_fox_logf[0], bs, n_pages)
    a_s = fox_sample_attention(qs, cs, d_past, ks, vs, cache_fox_k[0], cache_fox_v[0], pt_flat, bs, ts, n_heads,
                               n_pages).astype(BF16)
    xp, xs = matmul_pair(ap, a_s, w_o_a, 0, 0, d_model, res=(xp, xs))
    xp, xs, st0p, st0s = ffn(xp, xs, 0)

    xnp, xkvp = norm(xp, jnp.stack([g_attn[1], g_kv]))
    xns, xkvs = norm(xs, jnp.stack([g_attn[1], g_kv]))
    kvp, kvs = matmul_pair(xkvp, xkvs, w_kv_b[None], 0, 0, 2 * kvw)
    qp, qs = matmul_pair(xnp, xns, w_q_b, 0, 0, w_q_b.shape[2], out_scale=ATTN_SCALE)
    kbp, vbp = kvp[:, :kvw], kvp[:, kvw:]
    kbs, vbs = kvs[:, :kvw], kvs[:, kvw:]
    ap = dilated_prompt_attention(qp, kbp, vbp, rel_bias, bp, sp, n_kv, group_heads)
    a_s = dilated_sample_attention(qs, kbs, vbs, cache_win_k, cache_win_v, rel_bias, bs, ts, n_kv,
                                   group_heads).astype(BF16)
    xp, xs = matmul_pair(ap, a_s, w_o_b, 0, 0, d_model, res=(xp, xs))
    xp, xs, st1p, st1s = ffn(xp, xs, 1)
    (yp,) = norm(xp, g_final[None], F32)
    (ys,) = norm(xs, g_final[None], F32)

    n_pg = sp // PAGE_SIZE
    keep = min(BUCKET_MAX_DIST, sp)
    return (
        yp.reshape(bp, sp, d_model),
        ys.reshape(bs, ts, d_model),
        kp.reshape(1, bp, n_pg, PAGE_SIZE, n_heads, HEAD_DIM),
        vp.reshape(1, bp, n_pg, PAGE_SIZE, n_heads, HEAD_DIM),
        lfp.reshape(1, bp, n_pg, PAGE_SIZE, n_heads),
        ks.reshape(1, bs, ts, n_heads, HEAD_DIM),
        vs.reshape(1, bs, ts, n_heads, HEAD_DIM),
        lfs.reshape(1, bs, ts, n_heads),
        kbp.reshape(bp, sp, n_kv, HEAD_DIM)[:, sp - keep:],
        vbp.reshape(bp, sp, n_kv, HEAD_DIM)[:, sp - keep:],
        jnp.concatenate([cache_win_k[:, ts:], kbs.reshape(bs, ts, n_kv, HEAD_DIM)], axis=1)[:, -n_buf:],
        jnp.concatenate([cache_win_v[:, ts:], vbs.reshape(bs, ts, n_kv, HEAD_DIM)], axis=1)[:, -n_buf:],
        jnp.stack([st0p, st1p]),
        jnp.stack([st0s, st1s]),
    )
```

```python
import functools
import math

import numpy as np
import jax
import jax.numpy as jnp
from jax import lax
from jax.experimental import pallas as pl
from jax.experimental.pallas import tpu as pltpu

F32 = jnp.float32
BF16 = jnp.bfloat16

HEAD_DIM = 128
PAGE_SIZE = 128
PATTERNS = ((128, 1), (512, 4), (2048, 16))
N_GROUPS = len(PATTERNS)
N_BUCKETS = 32
BUCKET_MAX_DIST = max(w for w, _ in PATTERNS)
CONV_W = 3
Q_BLOCK = 128
NORM_EPS = 1e-6
ATTN_SCALE = HEAD_DIM ** -0.5
NEG = -1e30
SQRT_HALF = 0.7071067811865476

V7X_VMEM_LIMIT_BYTES = 56 * 1024 * 1024


def _params(*semantics):
    return pltpu.CompilerParams(dimension_semantics=semantics, vmem_limit_bytes=V7X_VMEM_LIMIT_BYTES)


def _rmsnorm_kernel(x_ref, g_ref, *o_refs):
    x = x_ref[...]
    y = x * lax.rsqrt(jnp.mean(x * x, axis=-1, keepdims=True) + NORM_EPS)
    for j, o_ref in enumerate(o_refs):
        o_ref[...] = (y * g_ref[j:j + 1, :]).astype(o_ref.dtype)


def rmsnorm(x, gains, out_dtype):
    m, d = x.shape
    n = gains.shape[0]
    tm = min(256, m)
    return pl.pallas_call(
        _rmsnorm_kernel,
        grid=(m // tm,),
        in_specs=[pl.BlockSpec((tm, d), lambda i: (i, 0)), pl.BlockSpec((n, d), lambda i: (0, 0))],
        out_specs=[pl.BlockSpec((tm, d), lambda i: (i, 0))] * n,
        out_shape=[jax.ShapeDtypeStruct((m, d), out_dtype)] * n,
        compiler_params=_params("parallel"),
        name="rmsnorm",
    )(x, gains)


def _matmul_kernel(*refs, has_res, nk, out_scale):
    x_ref, w_ref = refs[0], refs[1]
    res_ref = refs[2] if has_res else None
    o_ref = refs[2 + has_res]

    def finish(acc):
        if out_scale != 1.0:
            acc = acc * out_scale
        if has_res:
            acc = acc + res_ref[...]
        o_ref[...] = acc.astype(o_ref.dtype)

    part = jnp.dot(x_ref[...], w_ref[...], preferred_element_type=F32)
    if nk == 1:
        finish(part)
        return
    acc_ref = refs[3 + has_res]
    k = pl.program_id(2)

    @pl.when(k == 0)
    def _():
        acc_ref[...] = part

    @pl.when(jnp.logical_and(k > 0, k < nk - 1))
    def _():
        acc_ref[...] += part

    @pl.when(k == nk - 1)
    def _():
        finish(acc_ref[...] + part)


def matmul(x, w, layer, *, res=None, out_dtype=F32, out_scale=1.0, bm=1024, bn=1024, bk=None):
    m, kdim = x.shape
    n = w.shape[2]
    bm, bn = min(bm, m), min(bn, n)
    bk = kdim if bk is None else bk
    nk = kdim // bk
    assert m % bm == 0 and n % bn == 0 and kdim % bk == 0 and nk >= 1
    in_specs = [pl.BlockSpec((bm, bk), lambda i, j, k: (i, k)),
                pl.BlockSpec((None, bk, bn), lambda i, j, k: (layer, k, j))]
    args = [x, w]
    if res is not None:
        in_specs.append(pl.BlockSpec((bm, bn), lambda i, j, k: (i, j)))
        args.append(res)
    return pl.pallas_call(
        functools.partial(_matmul_kernel, has_res=res is not None, nk=nk, out_scale=out_scale),
        grid=(m // bm, n // bn, nk),
        in_specs=in_specs,
        out_specs=pl.BlockSpec((bm, bn), lambda i, j, k: (i, j)),
        out_shape=jax.ShapeDtypeStruct((m, n), out_dtype),
        scratch_shapes=[pltpu.VMEM((bm, bn), F32)] if nk > 1 else [],
        compiler_params=_params("parallel", "parallel", "arbitrary"),
        name="matmul",
    )(*args)


def _pair_matmul_kernel(*refs, nm, has_res, out_scale, transposed):
    if has_res:
        xp_ref, xs_ref, w_ref, rp_ref, rs_ref, op_ref, os_ref, wb_ref = refs
    else:
        xp_ref, xs_ref, w_ref, op_ref, os_ref, wb_ref = refs
        rp_ref = rs_ref = None
    i = pl.program_id(1)

    @pl.when(i == 0)
    def _():
        wb_ref[...] = w_ref[...].astype(BF16)

    def emit(x_ref, r_ref, o_ref):
        dims = (((1,), (1 if transposed else 0,)), ((), ()))
        acc = lax.dot_general(x_ref[...], wb_ref[...], dims, preferred_element_type=F32)
        if out_scale != 1.0:
            acc = acc * out_scale
        if r_ref is not None:
            acc = acc + r_ref[...]
        o_ref[...] = acc.astype(o_ref.dtype)

    @pl.when(i < nm)
    def _():
        emit(xp_ref, rp_ref, op_ref)

    @pl.when(i == nm)
    def _():
        emit(xs_ref, rs_ref, os_ref)


def matmul_pair(xp, xs, w, layer, col0, n, *, res=None, out_dtype=F32, out_scale=1.0, transposed=False,
                bm=1024, bn=512):
    mp, kdim = xp.shape
    ms = xs.shape[0]
    assert mp % bm == 0 and n % bn == 0 and col0 % bn == 0 and w.shape[2 if transposed else 1] == kdim
    nm, cb0, last = mp // bm, col0 // bn, mp // bm - 1
    row = lambda j, i: (jnp.minimum(i, last), 0)
    tile = lambda j, i: (jnp.minimum(i, last), j)
    if transposed:
        w_spec = pl.BlockSpec((None, bn, kdim), lambda j, i: (layer, cb0 + j, 0))
    else:
        w_spec = pl.BlockSpec((None, kdim, bn), lambda j, i: (layer, 0, cb0 + j))
    in_specs = [pl.BlockSpec((bm, kdim), row), pl.BlockSpec((ms, kdim), lambda j, i: (0, 0)), w_spec]
    args = [xp, xs, w]
    if res is not None:
        in_specs += [pl.BlockSpec((bm, bn), tile), pl.BlockSpec((ms, bn), lambda j, i: (0, j))]
        args += list(res)
    return pl.pallas_call(
        functools.partial(_pair_matmul_kernel, nm=nm, has_res=res is not None, out_scale=out_scale,
                          transposed=transposed),
        grid=(n // bn, nm + 1),
        in_specs=in_specs,
        out_specs=[pl.BlockSpec((bm, bn), tile), pl.BlockSpec((ms, bn), lambda j, i: (0, j))],
        out_shape=[jax.ShapeDtypeStruct((mp, n), out_dtype), jax.ShapeDtypeStruct((ms, n), out_dtype)],
        scratch_shapes=[pltpu.VMEM((bn, kdim) if transposed else (kdim, bn), BF16)],
        compiler_params=_params("parallel", "arbitrary"),
        name="matmul_pair",
    )(*args)


def _gate_kernel(x_ref, w_ref, b_ref, o_ref):
    z = lax.dot_general(x_ref[...], w_ref[...].astype(BF16), (((1,), (1,)), ((), ())),
                        preferred_element_type=F32) + b_ref[...]
    o_ref[...] = -(jnp.maximum(-z, 0.0) + jnp.log1p(jnp.exp(-jnp.abs(z))))


def forget_gate(xn, w_t, layer, row0, b_f):
    m, d = xn.shape
    nh = b_f.shape[1]
    tm = min(512, m)
    assert row0 % nh == 0 and nh % 8 == 0
    return pl.pallas_call(
        _gate_kernel,
        grid=(m // tm,),
        in_specs=[pl.BlockSpec((tm, d), lambda i: (i, 0)),
                  pl.BlockSpec((None, nh, d), lambda i: (layer, row0 // nh, 0)),
                  pl.BlockSpec((1, nh), lambda i: (0, 0))],
        out_specs=pl.BlockSpec((tm, nh), lambda i: (i, 0)),
        out_shape=jax.ShapeDtypeStruct((m, nh), F32),
        compiler_params=_params("parallel"),
        name="forget_gate",
    )(xn, w_t, b_f)


def _split_dot(tri, x):
    hi = x.astype(BF16)
    r1 = x - hi.astype(F32)
    mid = r1.astype(BF16)
    lo = (r1 - mid.astype(F32)).astype(BF16)
    out = jnp.dot(tri, lo, preferred_element_type=F32)
    out = out + jnp.dot(tri, mid, preferred_element_type=F32)
    return out + jnp.dot(tri, hi, preferred_element_type=F32)


def _cumsum_kernel(x_ref, o_ref, *, chunk):
    s = x_ref.shape[0]
    row = lax.broadcasted_iota(jnp.int32, (chunk, chunk), 0)
    col = lax.broadcasted_iota(jnp.int32, (chunk, chunk), 1)
    tri = jnp.where(col <= row, 1.0, 0.0).astype(BF16)
    carry = jnp.zeros((1, x_ref.shape[1]), F32)
    for i in range(s // chunk):
        y = _split_dot(tri, x_ref[i * chunk:(i + 1) * chunk, :]) + carry
        o_ref[i * chunk:(i + 1) * chunk, :] = y
        carry = y[chunk - 1:chunk, :]


def _cumsum_small_kernel(x_ref, o_ref):
    x = x_ref[...]
    t = x.shape[0]
    row = lax.broadcasted_iota(jnp.int32, x.shape, 0)
    for i in range(t):
        o_ref[i:i + 1, :] = jnp.sum(jnp.where(row <= i, x, 0.0), axis=0, keepdims=True)


def cumsum_rows(x):
    b, s, nh = x.shape
    body = _cumsum_small_kernel if s < 128 else functools.partial(_cumsum_kernel, chunk=min(256, s))
    return pl.pallas_call(
        body,
        grid=(b,),
        in_specs=[pl.BlockSpec((None, s, nh), lambda i: (i, 0, 0))],
        out_specs=pl.BlockSpec((None, s, nh), lambda i: (i, 0, 0)),
        out_shape=jax.ShapeDtypeStruct((b, s, nh), F32),
        compiler_params=_params("parallel"),
        name="cumsum_rows",
    )(x)


SUFFIX_PAGES_PER_STEP = 16


def _page_suffix_kernel(pt_ref, *refs, pps):
    lf_refs, o_ref, carry_ref = refs[:pps], refs[pps], refs[pps + 1]
    j = pl.program_id(1)
    n = lf_refs[0].shape[0]

    @pl.when(j == 0)
    def _():
        carry_ref[...] = jnp.zeros_like(carry_ref)

    row = lax.broadcasted_iota(jnp.int32, (n, n), 0)
    col = lax.broadcasted_iota(jnp.int32, (n, n), 1)
    upper = jnp.where(col > row, 1.0, 0.0).astype(BF16)
    carry = carry_ref[...]
    for pp in reversed(range(pps)):
        x = lf_refs[pp][...]
        o_ref[pp] = _split_dot(upper, x) + carry
        carry = carry + jnp.sum(x, axis=0, keepdims=True)
    carry_ref[...] = carry


def page_suffix_sums(page_table_flat, lf_pool, b, n_pages):
    _, n, nh = lf_pool.shape
    pps = SUFFIX_PAGES_PER_STEP
    assert n_pages % pps == 0
    n_steps = n_pages // pps

    def page_spec(pp):
        return pl.BlockSpec((None, n, nh), lambda i, j, pt: (pt[i * n_pages + n_pages - (j + 1) * pps + pp], 0, 0))

    grid_spec = pltpu.PrefetchScalarGridSpec(
        num_scalar_prefetch=1,
        grid=(b, n_steps),
        in_specs=[page_spec(pp) for pp in range(pps)],
        out_specs=pl.BlockSpec((None, pps, n, nh), lambda i, j, pt: (i, n_steps - 1 - j, 0, 0)),
        scratch_shapes=[pltpu.VMEM((1, nh), F32)],
    )
    return pl.pallas_call(
        functools.partial(_page_suffix_kernel, pps=pps),
        grid_spec=grid_spec,
        out_shape=jax.ShapeDtypeStruct((b, n_pages, n, nh), F32),
        compiler_params=_params("parallel", "arbitrary"),
        name="page_suffix_sums",
    )(page_table_flat, *([lf_pool] * pps))


def _fox_prompt_kernel(q_ref, k_ref, v_ref, c_ref, ct_ref, o_ref, *, nq):
    tq = q_ref.shape[0]
    h = pl.program_id(1)
    i = pl.program_id(2)
    q = q_ref[...]
    lane = lax.broadcasted_iota(jnp.int32, c_ref.shape, 1)
    cq = jnp.sum(jnp.where(lane == h, c_ref[...], 0.0), axis=-1, keepdims=True)
    dn = (((1,), (1,)), ((), ()))

    def scores(j):
        kb = k_ref[j * tq:(j + 1) * tq, :].astype(BF16)
        return lax.dot_general(q, kb, dn, preferred_element_type=F32) + (cq - ct_ref[j:j + 1, :])

    def update(j, s, carry, masked):
        m, l, acc = carry
        if masked:
            row = lax.broadcasted_iota(jnp.int32, s.shape, 0)
            col = lax.broadcasted_iota(jnp.int32, s.shape, 1)
            s = jnp.where(col <= row, s, NEG)
        m_new = jnp.maximum(m, jnp.max(s, axis=-1, keepdims=True))
        alpha = jnp.exp(m - m_new)
        p = jnp.exp(s - m_new)
        l = l * alpha + jnp.sum(p, axis=-1, keepdims=True)
        vb = v_ref[j * tq:(j + 1) * tq, :].astype(BF16)
        acc = acc * alpha + jnp.dot(p.astype(BF16), vb, preferred_element_type=F32)
        return m_new, l, acc

    for ii in range(nq):

        @pl.when(i == ii)
        def _(ii=ii):
            carry = (jnp.full((tq, 1), NEG, F32), jnp.zeros((tq, 1), F32), jnp.zeros((tq, HEAD_DIM), F32))
            s_next = scores(0)
            for j in range(ii + 1):
                s_cur = s_next
                if j < ii:
                    s_next = scores(j + 1)
                carry = update(j, s_cur, carry, masked=(j == ii))
            _, l, acc = carry
            o_ref[...] = (acc / l).astype(o_ref.dtype)


def fox_prompt_attention(q, k, v, c, b, s, n_heads, tq=512):
    nq = s // tq
    ct = c.reshape(b, s, n_heads).transpose(0, 2, 1).reshape(b * n_heads, nq, tq)
    return pl.pallas_call(
        functools.partial(_fox_prompt_kernel, nq=nq),
        grid=(b, n_heads, nq),
        in_specs=[
            pl.BlockSpec((tq, HEAD_DIM), lambda bi, h, i: (bi * nq + i, h)),
            pl.BlockSpec((s, HEAD_DIM), lambda bi, h, i: (bi, h)),
            pl.BlockSpec((s, HEAD_DIM), lambda bi, h, i: (bi, h)),
            pl.BlockSpec((tq, n_heads), lambda bi, h, i: (bi * nq + i, 0)),
            pl.BlockSpec((None, nq, tq), lambda bi, h, i: (bi * n_heads + h, 0, 0)),
        ],
        out_specs=pl.BlockSpec((tq, HEAD_DIM), lambda bi, h, i: (bi * nq + i, h)),
        out_shape=jax.ShapeDtypeStruct(q.shape, BF16),
        compiler_params=_params("parallel", "parallel", "arbitrary"),
        name="fox_prompt_attention",
    )(q, k, v, c, ct)


HEADS_PER_TILE = 8


FOX_PAGES_PER_STEP = 2


def _tile_rows(ref, sub):
    rows, n_sub, dh = ref.shape
    return ref.reshape(rows * n_sub, dh)[pl.ds(sub, rows, stride=n_sub), :]


def _fox_sample_kernel(pt_ref, q_ref, cn_ref, cnt_ref, dt_ref, kn_ref, vn_ref, *refs, n_heads, n_steps, pps):
    ng = n_heads // HEADS_PER_TILE
    nb = ng * pps
    k_refs, v_refs = refs[:nb], refs[nb:2 * nb]
    o_ref, m_ref, l_ref, acc_ref = refs[2 * nb:]
    j = pl.program_id(1)
    t = q_ref.shape[1]
    dn = (((1,), (1,)), ((), ()))

    @pl.when(j == 0)
    def _():
        m_ref[...] = jnp.full_like(m_ref, NEG)
        l_ref[...] = jnp.zeros_like(l_ref)
        acc_ref[...] = jnp.zeros_like(acc_ref)

    def head_rows(page_refs, h):
        g, hs = divmod(h, HEADS_PER_TILE)
        tiles = [_tile_rows(page_refs[pp * ng + g], hs).astype(BF16) for pp in range(pps)]
        return tiles[0] if pps == 1 else jnp.concatenate(tiles, axis=0)

    def online_update(s, pv):
        m_old = m_ref[...]
        m_new = jnp.maximum(m_old, jnp.max(s, axis=-1, keepdims=True))
        alpha = jnp.exp(m_old - m_new)
        p = jnp.exp(s - m_new)
        l_ref[...] = l_ref[...] * alpha + jnp.sum(p, axis=-1, keepdims=True)
        acc_ref[...] = acc_ref[...] * alpha + pv(p)
        m_ref[...] = m_new

    parts = []
    for h in range(n_heads):
        sh = lax.dot_general(q_ref[h], head_rows(k_refs, h), dn, preferred_element_type=F32)
        d_row = [dt_ref[pp, h:h + 1, :] for pp in range(pps)]
        parts.append(sh + (d_row[0] if pps == 1 else jnp.concatenate(d_row, axis=1)))
    cn = cn_ref[...]
    s = jnp.concatenate(parts, axis=0) + (cn if pps == 1 else jnp.concatenate([cn] * pps, axis=1))

    def pv_pages(p):
        return jnp.concatenate(
            [jnp.dot(p[h * t:(h + 1) * t].astype(BF16), head_rows(v_refs, h), preferred_element_type=F32)
             for h in range(n_heads)], axis=0)

    online_update(s, pv_pages)

    @pl.when(j == n_steps - 1)
    def _():
        parts = []
        for h in range(n_heads):
            kh = kn_ref[:, h * HEAD_DIM:(h + 1) * HEAD_DIM].astype(BF16)
            parts.append(lax.dot_general(q_ref[h], kh, dn, preferred_element_type=F32))
        s2 = jnp.concatenate(parts, axis=0) + cn_ref[:, :t] - cnt_ref[...]
        row = lax.broadcasted_iota(jnp.int32, s2.shape, 0)
        col = lax.broadcasted_iota(jnp.int32, s2.shape, 1)
        s2 = jnp.where(col <= lax.rem(row, t), s2, NEG)

        def pv_new(p):
            return jnp.concatenate(
                [jnp.dot(p[h * t:(h + 1) * t], vn_ref[:, h * HEAD_DIM:(h + 1) * HEAD_DIM], preferred_element_type=F32)
                 for h in range(n_heads)], axis=0)

        online_update(s2, pv_new)
        o_ref[...] = acc_ref[...] / l_ref[...]


def fox_sample_attention(q, c_new, d_past, k_new, v_new, k_pool, v_pool, page_table_flat, b, t, n_heads, n_pages):
    ng = n_heads // HEADS_PER_TILE
    pps = FOX_PAGES_PER_STEP
    assert n_pages % pps == 0
    n_steps = n_pages // pps
    n_pool = k_pool.shape[0]
    ht = n_heads * t
    q4 = q.reshape(b, t, n_heads, HEAD_DIM).transpose(0, 2, 1, 3)
    c_ht = c_new.transpose(0, 2, 1).reshape(b, ht, 1)
    cn = jnp.broadcast_to(c_ht, (b, ht, PAGE_SIZE))
    cnt = jnp.broadcast_to(c_new.transpose(0, 2, 1)[:, :, None, :], (b, n_heads, t, t)).reshape(b, ht, t)
    dt = d_past.transpose(0, 1, 3, 2)
    kp = k_pool.reshape(n_pool, PAGE_SIZE, ng, HEADS_PER_TILE, HEAD_DIM)
    vp = v_pool.reshape(n_pool, PAGE_SIZE, ng, HEADS_PER_TILE, HEAD_DIM)

    def page_spec(pp, g):
        return pl.BlockSpec((None, PAGE_SIZE, None, HEADS_PER_TILE, HEAD_DIM),
                            lambda i, j, pt: (pt[i * n_pages + j * pps + pp], 0, g, 0, 0))

    page_specs = [page_spec(pp, g) for pp in range(pps) for g in range(ng)]
    d = n_heads * HEAD_DIM
    grid_spec = pltpu.PrefetchScalarGridSpec(
        num_scalar_prefetch=1,
        grid=(b, n_steps),
        in_specs=[
            pl.BlockSpec((None, n_heads, t, HEAD_DIM), lambda i, j, pt: (i, 0, 0, 0)),
            pl.BlockSpec((None, ht, PAGE_SIZE), lambda i, j, pt: (i, 0, 0)),
            pl.BlockSpec((None, ht, t), lambda i, j, pt: (i, 0, 0)),
            pl.BlockSpec((None, pps, n_heads, PAGE_SIZE), lambda i, j, pt: (i, j, 0, 0)),
            pl.BlockSpec((t, d), lambda i, j, pt: (i, 0)),
            pl.BlockSpec((t, d), lambda i, j, pt: (i, 0)),
        ] + page_specs * 2,
        out_specs=pl.BlockSpec((None, ht, HEAD_DIM), lambda i, j, pt: (i, 0, 0)),
        scratch_shapes=[pltpu.VMEM((ht, 1), F32), pltpu.VMEM((ht, 1), F32), pltpu.VMEM((ht, HEAD_DIM), F32)],
    )
    o = pl.pallas_call(
        functools.partial(_fox_sample_kernel, n_heads=n_heads, n_steps=n_steps, pps=pps),
        grid_spec=grid_spec,
        out_shape=jax.ShapeDtypeStruct((b, ht, HEAD_DIM), F32),
        compiler_params=_params("parallel", "arbitrary"),
        name="fox_sample_attention",
    )(page_table_flat, q4, cn, cnt, dt, k_new, v_new, *([kp] * (ng * pps)), *([vp] * (ng * pps)))
    return o.reshape(b, n_heads, t, HEAD_DIM).transpose(0, 2, 1, 3).reshape(b * t, d)


def _t5_bucket(dist):
    n = np.asarray(dist, dtype=np.int64)
    exact = N_BUCKETS // 2
    large = exact + (np.log(np.maximum(n, 1) / exact) / np.log(BUCKET_MAX_DIST / exact) * (N_BUCKETS - exact)).astype(np.int64)
    return np.where(n < exact, n, np.minimum(large, N_BUCKETS - 1)).astype(np.int32)


def _band_bias(rel_bias, group_heads):
    a = np.arange(Q_BLOCK)[:, None]
    bcol = np.arange(2 * Q_BLOCK)[None, :]
    j = a - bcol + Q_BLOCK
    out = []
    for g, (window, dil) in enumerate(PATTERNS):
        assert window // dil == Q_BLOCK
        valid = (j >= 0) & (j <= window // dil)
        bucket = _t5_bucket(dil * np.clip(j, 0, window // dil))
        bias = _select_buckets(rel_bias[:, g * group_heads:(g + 1) * group_heads], bucket)
        out.append(jnp.where(valid[None], bias, NEG))
    return jnp.stack(out)


def _select_buckets(rb, bucket):
    onehot = (jnp.asarray(bucket.reshape(-1))[None, :] == jnp.arange(N_BUCKETS)[:, None]).astype(F32)
    sel = jnp.einsum("bh,bn->hn", rb.astype(F32), onehot, precision=lax.Precision.HIGHEST)
    return sel.reshape((rb.shape[1],) + bucket.shape)


DILATED_BLOCKS_PER_STEP = 4


def _dilated_prompt_kernel(*refs, s):
    n_q = N_GROUPS * 2
    q_refs = refs[:n_q]
    k_ref, v_ref, bias_ref, o_ref, m_ref, l_ref, acc_ref = refs[n_q:]
    qb_rows = Q_BLOCK
    dn = (((1,), (1,)), ((), ()))

    def blocks(g, specs, merge):
        bias = bias_ref[g].reshape(2 * qb_rows, 2 * qb_rows)
        scores = []
        for q_idx, k_idx, first in specs:
            q2 = jnp.concatenate([q_refs[2 * g + r][q_idx, :].astype(BF16) for r in range(2)], axis=0)
            sc = lax.dot_general(q2, k_ref[k_idx, :].astype(BF16), dn, preferred_element_type=F32)
            scores.append(sc + (bias[:, qb_rows:] if first else bias))
        stats = []
        for sc in scores:
            m_blk = jnp.max(sc, axis=-1, keepdims=True)
            p = jnp.exp(sc - m_blk)
            stats.append((m_blk, jnp.sum(p, axis=-1, keepdims=True), p.astype(BF16)))
        outs = [jnp.dot(p, v_ref[k_idx, :].astype(BF16), preferred_element_type=F32)
                for (_, _, p), (_, k_idx, _) in zip(stats, specs)]
        for (m_blk, l_blk, _), o_blk, (q_idx, _, _) in zip(stats, outs, specs):
            for r in range(2):
                rs = slice(r * qb_rows, (r + 1) * qb_rows)
                m_b = jnp.broadcast_to(m_blk[rs], (qb_rows, HEAD_DIM))
                l_b = jnp.broadcast_to(l_blk[rs], (qb_rows, HEAD_DIM))
                if merge:
                    m_old = m_ref[r, q_idx, :]
                    m_new = jnp.maximum(m_old, m_b)
                    a_old = jnp.exp(m_old - m_new)
                    a_blk = jnp.exp(m_b - m_new)
                    l_ref[r, q_idx, :] = l_ref[r, q_idx, :] * a_old + l_b * a_blk
                    acc_ref[r, q_idx, :] = acc_ref[r, q_idx, :] * a_old + o_blk[rs] * a_blk
                    m_ref[r, q_idx, :] = m_new
                else:
                    m_ref[r, q_idx, :] = m_b
                    l_ref[r, q_idx, :] = l_b
                    acc_ref[r, q_idx, :] = o_blk[rs]

    per_step = DILATED_BLOCKS_PER_STEP
    for g, (_, dil) in enumerate(PATTERNS):
        n_blocks = s // (dil * qb_rows)
        merge = g > 0

        def idx(start, size, dil=dil):
            return pl.ds(start, size) if dil == 1 else pl.ds(start, size, stride=dil)

        def spec(c, qb, dil=dil, idx=idx):
            q0 = c + dil * qb_rows * qb
            if isinstance(qb, int) and qb == 0:
                return idx(q0, qb_rows), idx(q0, qb_rows), True
            return idx(q0, qb_rows), idx(q0 - dil * qb_rows, 2 * qb_rows), False

        if n_blocks == 1:
            assert dil % per_step == 0

            def classes(it, carry, g=g, merge=merge, spec=spec):
                blocks(g, [spec(it * per_step + u, 0) for u in range(per_step)], merge)
                return carry

            lax.fori_loop(0, dil // per_step, classes, 0)
        elif n_blocks <= per_step:

            def one_class(c, carry, g=g, merge=merge, spec=spec, n_blocks=n_blocks):
                blocks(g, [spec(c, qb) for qb in range(n_blocks)], merge)
                return carry

            lax.fori_loop(0, dil, one_class, 0)
        else:
            assert dil == 1 and (n_blocks - 1) % (per_step - 1) == 0
            blocks(g, [spec(0, 0)], merge)
            grp = per_step - 1

            def later(it, carry, g=g, merge=merge, spec=spec, grp=grp):
                blocks(g, [spec(0, 1 + it * grp + u) for u in range(grp)], merge)
                return carry

            lax.fori_loop(0, (n_blocks - 1) // grp, later, 0)

    for r in range(2):
        o_ref[:, r * HEAD_DIM:(r + 1) * HEAD_DIM] = (acc_ref[r] / l_ref[r]).astype(o_ref.dtype)


def dilated_prompt_attention(q, k, v, rel_bias, b, s, n_kv, group_heads):
    assert group_heads == 2 * n_kv
    bias = _band_bias(rel_bias, group_heads).reshape(N_GROUPS, n_kv, 2, Q_BLOCK, 2 * Q_BLOCK)

    def q_spec(g, r):
        return pl.BlockSpec((s, HEAD_DIM), lambda bi, kv: (bi, g * group_heads + 2 * kv + r))

    return pl.pallas_call(
        functools.partial(_dilated_prompt_kernel, s=s),
        grid=(b, n_kv),
        in_specs=[q_spec(g, r) for g in range(N_GROUPS) for r in range(2)] + [
            pl.BlockSpec((s, HEAD_DIM), lambda bi, kv: (bi, kv)),
            pl.BlockSpec((s, HEAD_DIM), lambda bi, kv: (bi, kv)),
            pl.BlockSpec((N_GROUPS, None, 2, Q_BLOCK, 2 * Q_BLOCK), lambda bi, kv: (0, kv, 0, 0, 0)),
        ],
        out_specs=pl.BlockSpec((s, 2 * HEAD_DIM), lambda bi, kv: (bi, kv)),
        out_shape=jax.ShapeDtypeStruct((b * s, group_heads * HEAD_DIM), BF16),
        scratch_shapes=[pltpu.VMEM((2, s, HEAD_DIM), F32)] * 3,
        compiler_params=_params("parallel", "parallel"),
        name="dilated_prompt_attention",
    )(*([q] * (N_GROUPS * 2)), k, v, bias)


def _sample_bias(rel_bias, group_heads, n_kv, t, n_buf):
    dist = n_buf + np.arange(t)[:, None] - np.arange(n_buf + t)[None, :]
    out = []
    for g, (window, dil) in enumerate(PATTERNS):
        valid = (dist >= 0) & (dist % dil == 0) & (dist <= window)
        bucket = _t5_bucket(np.clip(dist, 0, window))
        bias = _select_buckets(rel_bias[:, g * group_heads:(g + 1) * group_heads], bucket)
        out.append(jnp.where(valid[None], bias, NEG).reshape(n_kv, 2, t, n_buf + t))
    return jnp.stack(out, axis=1).reshape(n_kv, N_GROUPS * 2 * t, n_buf + t)


def _dilated_sample_kernel(q_ref, kc_ref, vc_ref, kn_ref, vn_ref, bc_ref, bn_ref, o_ref, *, n_kv, group_heads):
    t = q_ref.shape[0]
    n_buf = kc_ref.shape[0]
    for kv in range(n_kv):
        heads = [g * group_heads + 2 * kv + r for g in range(N_GROUPS) for r in range(2)]
        q6 = jnp.concatenate([q_ref[:, h * HEAD_DIM:(h + 1) * HEAD_DIM] for h in heads], axis=0).astype(BF16)
        kc = _tile_rows(kc_ref, kv).astype(BF16)
        vc = _tile_rows(vc_ref, kv).astype(BF16)
        sl = slice(kv * HEAD_DIM, (kv + 1) * HEAD_DIM)
        kn = kn_ref[:, sl].astype(BF16)
        vn = vn_ref[:, sl]
        dn = (((1,), (1,)), ((), ()))
        s_c = lax.dot_general(q6, kc, dn, preferred_element_type=F32) + bc_ref[kv]
        s_n = lax.dot_general(q6, kn, dn, preferred_element_type=F32) + bn_ref[kv]
        m_row = jnp.maximum(jnp.max(s_c, axis=-1, keepdims=True), jnp.max(s_n, axis=-1, keepdims=True))
        m = jnp.max(m_row.reshape(N_GROUPS, 2 * t, 1), axis=0)
        m_all = jnp.concatenate([m] * N_GROUPS, axis=0)
        p_c = jnp.exp(s_c - m_all)
        p_n = jnp.exp(s_n - m_all)
        l_row = jnp.sum(p_c, axis=-1, keepdims=True) + jnp.sum(p_n, axis=-1, keepdims=True)
        o_row = jnp.dot(p_c.astype(BF16), vc, preferred_element_type=F32) + jnp.dot(p_n, vn, preferred_element_type=F32)
        l = jnp.sum(l_row.reshape(N_GROUPS, 2 * t, 1), axis=0)
        o = jnp.sum(o_row.reshape(N_GROUPS, 2 * t, HEAD_DIM), axis=0) / l
        for r in range(2):
            o_ref[:, (2 * kv + r) * HEAD_DIM:(2 * kv + r + 1) * HEAD_DIM] = o[r * t:(r + 1) * t]


def dilated_sample_attention(q, k_new, v_new, cache_k, cache_v, rel_bias, b, t, n_kv, group_heads):
    n_buf = cache_k.shape[1]
    bias = _sample_bias(rel_bias, group_heads, n_kv, t, n_buf)
    bias_c, bias_n = bias[:, :, :n_buf], bias[:, :, n_buf:]
    dq = q.shape[1]
    dkv = n_kv * HEAD_DIM
    return pl.pallas_call(
        functools.partial(_dilated_sample_kernel, n_kv=n_kv, group_heads=group_heads),
        grid=(b,),
        in_specs=[
            pl.BlockSpec((t, dq), lambda i: (i, 0)),
            pl.BlockSpec((None, n_buf, n_kv, HEAD_DIM), lambda i: (i, 0, 0, 0)),
            pl.BlockSpec((None, n_buf, n_kv, HEAD_DIM), lambda i: (i, 0, 0, 0)),
            pl.BlockSpec((t, dkv), lambda i: (i, 0)),
            pl.BlockSpec((t, dkv), lambda i: (i, 0)),
            pl.BlockSpec(bias_c.shape, lambda i: (0, 0, 0)),
            pl.BlockSpec(bias_n.shape, lambda i: (0, 0, 0)),
        ],
        out_specs=pl.BlockSpec((t, group_heads * HEAD_DIM), lambda i: (i, 0)),
        out_shape=jax.ShapeDtypeStruct((b * t, group_heads * HEAD_DIM), F32),
        compiler_params=_params("parallel"),
        name="dilated_sample_attention",
    )(q, cache_k, cache_v, k_new, v_new, bias_c, bias_n)


CONV_PAD_ROWS = 8


FFN_COL_TILE = 256
FFN_ROW_CHUNKS = 4


def _ffn_hidden_kernel(xp_ref, xs_ref, wu_ref, wg_ref, prev_ref, wc_ref, bc_ref, hp_ref, hs_ref, stp_ref, sts_ref,
                       pad_ref, *, n_seq_s, t_s):
    i = pl.program_id(0)
    t = xp_ref.shape[0]
    lo = CONV_PAD_ROWS - (CONV_W - 1)
    rc = t // FFN_ROW_CHUNKS
    wu = wu_ref[...].astype(BF16)
    wg = wg_ref[...].astype(BF16)

    def conv_gelu_gate(gate, start, rows):
        uc = bc_ref[...] + pad_ref[pl.ds(lo + start, rows), :] * wc_ref[0:1, :]
        for tap in range(1, CONV_W):
            uc = uc + pad_ref[pl.ds(lo + start + tap, rows), :] * wc_ref[tap:tap + 1, :]
        return 0.5 * uc * (1.0 + lax.erf(uc * SQRT_HALF)) * gate

    @pl.when(i == 0)
    def _():
        x = xs_ref[...]
        u = jnp.dot(x, wu, preferred_element_type=F32)
        gate = jnp.dot(x, wg, preferred_element_type=F32)
        pieces = []
        for sq in range(n_seq_s):
            rs = slice(sq * t_s, (sq + 1) * t_s)
            pad_ref[pl.ds(CONV_PAD_ROWS, t_s), :] = u[rs]
            pad_ref[pl.ds(lo, CONV_W - 1), :] = prev_ref[sq]
            pieces.append(conv_gelu_gate(gate[rs], 0, t_s))
            sts_ref[sq] = pad_ref[pl.ds(lo + t_s, CONV_W - 1), :]
        hs_ref[...] = jnp.concatenate(pieces, axis=0).astype(hs_ref.dtype)

    pad_ref[pl.ds(lo, CONV_W - 1), :] = jnp.zeros((CONV_W - 1, pad_ref.shape[1]), F32)
    gates = []
    for c in range(FFN_ROW_CHUNKS):
        x = xp_ref[c * rc:(c + 1) * rc, :]
        pad_ref[pl.ds(CONV_PAD_ROWS + c * rc, rc), :] = jnp.dot(x, wu, preferred_element_type=F32)
        gates.append(jnp.dot(x, wg, preferred_element_type=F32))
    for c in range(FFN_ROW_CHUNKS):
        hp_ref[c * rc:(c + 1) * rc, :] = conv_gelu_gate(gates[c], c * rc, rc).astype(hp_ref.dtype)
    stp_ref[...] = pad_ref[pl.ds(lo + t, CONV_W - 1), :]


def conv_ffn_hidden(xp, xs, w_up, w_gate, layer, prev_s, w_conv, b_conv, n_seq_p, t_p, n_seq_s, t_s):
    mp, d = xp.shape
    ms = xs.shape[0]
    f = w_up.shape[2]
    tn = FFN_COL_TILE
    assert mp == n_seq_p * t_p and ms == n_seq_s * t_s and f % tn == 0 and t_p % (FFN_ROW_CHUNKS * 128) == 0
    last = f // tn - 1
    wspec = pl.BlockSpec((None, d, tn), lambda i, j: (layer, 0, j))
    s_col = lambda i, j: jnp.where(i == 0, j, last)
    return pl.pallas_call(
        functools.partial(_ffn_hidden_kernel, n_seq_s=n_seq_s, t_s=t_s),
        grid=(n_seq_p, f // tn),
        in_specs=[
            pl.BlockSpec((t_p, d), lambda i, j: (i, 0), pipeline_mode=pl.Buffered(1)),
            pl.BlockSpec((ms, d), lambda i, j: (0, 0)),
            wspec, wspec,
            pl.BlockSpec((None, n_seq_s, CONV_W - 1, tn), lambda i, j: (layer, 0, 0, s_col(i, j))),
            pl.BlockSpec((None, CONV_W, tn), lambda i, j: (layer, 0, j)),
            pl.BlockSpec((None, 1, tn), lambda i, j: (layer, 0, j)),
        ],
        out_specs=[
            pl.BlockSpec((t_p, tn), lambda i, j: (i, j)),
            pl.BlockSpec((ms, tn), lambda i, j: (0, s_col(i, j))),
            pl.BlockSpec((None, CONV_W - 1, tn), lambda i, j: (i, 0, j)),
            pl.BlockSpec((n_seq_s, CONV_W - 1, tn), lambda i, j: (0, 0, s_col(i, j))),
        ],
        out_shape=[jax.ShapeDtypeStruct((mp, f), BF16), jax.ShapeDtypeStruct((ms, f), BF16),
                   jax.ShapeDtypeStruct((n_seq_p, CONV_W - 1, f), F32),
                   jax.ShapeDtypeStruct((n_seq_s, CONV_W - 1, f), F32)],
        scratch_shapes=[pltpu.VMEM((t_p + CONV_PAD_ROWS, tn), F32)],
        compiler_params=_params("arbitrary", "arbitrary"),
        name="conv_ffn_hidden",
    )(xp, xs, w_up, w_gate, prev_s, w_conv, b_conv.reshape(b_conv.shape[0], 1, f))


def _shift_append_kernel(old_ref, new_ref, o_ref):
    n_keep = old_ref.shape[1]
    o_ref[:n_keep] = old_ref[0]
    o_ref[n_keep:] = new_ref[...]


def shift_append(cache, new):
    b, n_buf, kv, dh = cache.shape
    t = new.shape[1]
    old_block = (pl.Element(1), pl.Element(n_buf - t), pl.Element(kv), pl.Element(dh))
    return pl.pallas_call(
        _shift_append_kernel,
        grid=(b,),
        in_specs=[pl.BlockSpec(old_block, lambda i: (i, t, 0, 0)),
                  pl.BlockSpec((None, t, kv, dh), lambda i: (i, 0, 0, 0))],
        out_specs=pl.BlockSpec((None, n_buf, kv, dh), lambda i: (i, 0, 0, 0)),
        out_shape=jax.ShapeDtypeStruct(cache.shape, cache.dtype),
        compiler_params=_params("parallel"),
        name="shift_append",
    )(cache, new)


def kernel(x_prompt, x_sample, cache_fox_k, cache_fox_v, cache_fox_logf, cache_win_k, cache_win_v, state_ffn_conv,
           page_table, w_in_a, b_f_a, w_o_a, g_attn, g_kv, w_kv_b, w_q_b, w_o_b, rel_bias,
           g_ffn, w_up, w_gate, w_conv, b_conv, w_down, g_final):
    bp, sp, d_model = x_prompt.shape
    bs, ts, _ = x_sample.shape
    n_heads = b_f_a.shape[1]
    hd = n_heads * HEAD_DIM
    n_kv = cache_win_k.shape[2]
    kvw = n_kv * HEAD_DIM
    group_heads = w_q_b.shape[2] // (N_GROUPS * HEAD_DIM)
    d_ff = w_up.shape[2]
    n_pages = page_table.shape[1]
    assert w_in_a.shape[0] == 1 and w_q_b.shape[0] == 1 and g_attn.shape[0] == 2
    mp, ms = bp * sp, bs * ts

    b_f = b_f_a[0][None].astype(F32)
    w_in_t = jnp.swapaxes(w_in_a, 1, 2)
    w_down_b = w_down.astype(BF16)
    xp = x_prompt.reshape(mp, d_model)
    xs = x_sample.reshape(ms, d_model)

    def norm(x, gains, dtype=BF16):
        return rmsnorm(x, gains, dtype)

    def ffn(xp, xs, layer):
        (xnp,) = norm(xp, g_ffn[layer][None])
        (xns,) = norm(xs, g_ffn[layer][None])
        hp, hs, stp, sts = conv_ffn_hidden(xnp, xns, w_up, w_gate, layer, state_ffn_conv, w_conv, b_conv,
                                           bp, sp, bs, ts)
        xp = matmul(hp, w_down_b, layer, res=xp, bm=1024, bn=512, bk=d_ff // 2)
        xs = matmul(hs, w_down_b, layer, res=xs, bm=ms, bn=1024, bk=d_ff // 2)
        return xp, xs, stp, sts

    (xnp,) = norm(xp, g_attn[0][None])
    (xns,) = norm(xs, g_attn[0][None])
    qp, qs = matmul_pair(xnp, xns, w_in_t, 0, 0, hd, out_dtype=BF16, out_scale=ATTN_SCALE, transposed=True)
    kp, ks = matmul_pair(xnp, xns, w_in_t, 0, hd, hd, transposed=True)
    vp, vs = matmul_pair(xnp, xns, w_in_t, 0, 2 * hd, hd, transposed=True)
    lfp = forget_gate(xnp, w_in_t, 0, 3 * hd, b_f)
    lfs = forget_gate(xns, w_in_t, 0, 3 * hd, b_f)
    cp = cumsum_rows(lfp.reshape(bp, sp, n_heads))
    cs = cumsum_rows(lfs.reshape(bs, ts, n_heads))
    ap = fox_prompt_attention(qp, kp, vp, cp.reshape(mp, n_heads), bp, sp, n_heads)
    pt_flat = page_table.reshape(-1)
    d_past = page_suffix_sums(pt_flat, cache_fox_logf[0], bs, n_pages)
    a_s = fox_sample_attention(qs, cs, d_past, ks, vs, cache_fox_k[0], cache_fox_v[0], pt_flat, bs, ts, n_heads,
                               n_pages).astype(BF16)
    xp, xs = matmul_pair(ap, a_s, w_o_a, 0, 0, d_model, res=(xp, xs))
    xp, xs, st0p, st0s = ffn(xp, xs, 0)

    xnp, xkvp = norm(xp, jnp.stack([g_attn[1], g_kv]))
    xns, xkvs = norm(xs, jnp.stack([g_attn[1], g_kv]))
    kvp, kvs = matmul_pair(xkvp, xkvs, w_kv_b[None], 0, 0, 2 * kvw)
    qp, qs = matmul_pair(xnp, xns, w_q_b, 0, 0, w_q_b.shape[2], out_scale=ATTN_SCALE)
    kbp, vbp = kvp[:, :kvw], kvp[:, kvw:]
    kbs, vbs = kvs[:, :kvw], kvs[:, kvw:]
    ap = dilated_prompt_attention(qp, kbp, vbp, rel_bias, bp, sp, n_kv, group_heads)
    a_s = dilated_sample_attention(qs, kbs, vbs, cache_win_k, cache_win_v, rel_bias, bs, ts, n_kv,
                                   group_heads).astype(BF16)
    xp, xs = matmul_pair(ap, a_s, w_o_b, 0, 0, d_model, res=(xp, xs))
    xp, xs, st1p, st1s = ffn(xp, xs, 1)
    (yp,) = norm(xp, g_final[None], F32)
    (ys,) = norm(xs, g_final[None], F32)

    n_pg = sp // PAGE_SIZE
    keep = min(BUCKET_MAX_DIST, sp)
    return (
        yp.reshape(bp, sp, d_model),
        ys.reshape(bs, ts, d_model),
        kp.reshape(1, bp, n_pg, PAGE_SIZE, n_heads, HEAD_DIM),
        vp.reshape(1, bp, n_pg, PAGE_SIZE, n_heads, HEAD_DIM),
        lfp.reshape(1, bp, n_pg, PAGE_SIZE, n_heads),
        ks.reshape(1, bs, ts, n_heads, HEAD_DIM),
        vs.reshape(1, bs, ts, n_heads, HEAD_DIM),
        lfs.reshape(1, bs, ts, n_heads),
        kbp.reshape(bp, sp, n_kv, HEAD_DIM)[:, sp - keep:],
        vbp.reshape(bp, sp, n_kv, HEAD_DIM)[:, sp - keep:],
        shift_append(cache_win_k, kbs.reshape(bs, ts, n_kv, HEAD_DIM)),
        shift_append(cache_win_v, vbs.reshape(bs, ts, n_kv, HEAD_DIM)),
        jnp.stack([st0p, st1p]),
        jnp.stack([st0s, st1s]),
    )
```

```python
import functools
import math

import numpy as np
import jax
import jax.numpy as jnp
from jax import lax
from jax.experimental import pallas as pl
from jax.experimental.pallas import tpu as pltpu

F32 = jnp.float32
BF16 = jnp.bfloat16

HEAD_DIM = 128
PAGE_SIZE = 128
PATTERNS = ((128, 1), (512, 4), (2048, 16))
N_GROUPS = len(PATTERNS)
N_BUCKETS = 32
BUCKET_MAX_DIST = max(w for w, _ in PATTERNS)
CONV_W = 3
Q_BLOCK = 128
NORM_EPS = 1e-6
ATTN_SCALE = HEAD_DIM ** -0.5
NEG = -1e30
SQRT_HALF = 0.7071067811865476

V7X_VMEM_LIMIT_BYTES = 56 * 1024 * 1024


def _params(*semantics):
    return pltpu.CompilerParams(dimension_semantics=semantics, vmem_limit_bytes=V7X_VMEM_LIMIT_BYTES)


def _rmsnorm_kernel(x_ref, g_ref, *o_refs):
    x = x_ref[...]
    y = x * lax.rsqrt(jnp.mean(x * x, axis=-1, keepdims=True) + NORM_EPS)
    for j, o_ref in enumerate(o_refs):
        o_ref[...] = (y * g_ref[j:j + 1, :]).astype(o_ref.dtype)


def rmsnorm(x, gains, out_dtype):
    m, d = x.shape
    n = gains.shape[0]
    tm = min(256, m)
    return pl.pallas_call(
        _rmsnorm_kernel,
        grid=(m // tm,),
        in_specs=[pl.BlockSpec((tm, d), lambda i: (i, 0)), pl.BlockSpec((n, d), lambda i: (0, 0))],
        out_specs=[pl.BlockSpec((tm, d), lambda i: (i, 0))] * n,
        out_shape=[jax.ShapeDtypeStruct((m, d), out_dtype)] * n,
        compiler_params=_params("parallel"),
        name="rmsnorm",
    )(x, gains)


def _matmul_kernel(*refs, has_res, nk, out_scale):
    x_ref, w_ref = refs[0], refs[1]
    res_ref = refs[2] if has_res else None
    o_ref = refs[2 + has_res]

    def finish(acc):
        if out_scale != 1.0:
            acc = acc * out_scale
        if has_res:
            acc = acc + res_ref[...]
        o_ref[...] = acc.astype(o_ref.dtype)

    part = jnp.dot(x_ref[...], w_ref[...], preferred_element_type=F32)
    if nk == 1:
        finish(part)
        return
    acc_ref = refs[3 + has_res]
    k = pl.program_id(2)

    @pl.when(k == 0)
    def _():
        acc_ref[...] = part

    @pl.when(jnp.logical_and(k > 0, k < nk - 1))
    def _():
        acc_ref[...] += part

    @pl.when(k == nk - 1)
    def _():
        finish(acc_ref[...] + part)


def matmul(x, w, layer, *, res=None, out_dtype=F32, out_scale=1.0, bm=1024, bn=1024, bk=None):
    m, kdim = x.shape
    n = w.shape[2]
    bm, bn = min(bm, m), min(bn, n)
    bk = kdim if bk is None else bk
    nk = kdim // bk
    assert m % bm == 0 and n % bn == 0 and kdim % bk == 0 and nk >= 1
    in_specs = [pl.BlockSpec((bm, bk), lambda i, j, k: (i, k)),
                pl.BlockSpec((None, bk, bn), lambda i, j, k: (layer, k, j))]
    args = [x, w]
    if res is not None:
        in_specs.append(pl.BlockSpec((bm, bn), lambda i, j, k: (i, j)))
        args.append(res)
    return pl.pallas_call(
        functools.partial(_matmul_kernel, has_res=res is not None, nk=nk, out_scale=out_scale),
        grid=(m // bm, n // bn, nk),
        in_specs=in_specs,
        out_specs=pl.BlockSpec((bm, bn), lambda i, j, k: (i, j)),
        out_shape=jax.ShapeDtypeStruct((m, n), out_dtype),
        scratch_shapes=[pltpu.VMEM((bm, bn), F32)] if nk > 1 else [],
        compiler_params=_params("parallel", "parallel", "arbitrary"),
        name="matmul",
    )(*args)


def _pair_matmul_kernel(*refs, nm, has_res, out_scale, transposed):
    if has_res:
        xp_ref, xs_ref, w_ref, rp_ref, rs_ref, op_ref, os_ref, wb_ref = refs
    else:
        xp_ref, xs_ref, w_ref, op_ref, os_ref, wb_ref = refs
        rp_ref = rs_ref = None
    i = pl.program_id(1)

    @pl.when(i == 0)
    def _():
        wb_ref[...] = w_ref[...].astype(BF16)

    def emit(x_ref, r_ref, o_ref):
        dims = (((1,), (1 if transposed else 0,)), ((), ()))
        acc = lax.dot_general(x_ref[...], wb_ref[...], dims, preferred_element_type=F32)
        if out_scale != 1.0:
            acc = acc * out_scale
        if r_ref is not None:
            acc = acc + r_ref[...]
        o_ref[...] = acc.astype(o_ref.dtype)

    @pl.when(i < nm)
    def _():
        emit(xp_ref, rp_ref, op_ref)

    @pl.when(i == nm)
    def _():
        emit(xs_ref, rs_ref, os_ref)


def matmul_pair(xp, xs, w, layer, col0, n, *, res=None, out_dtype=F32, out_scale=1.0, transposed=False,
                bm=256, bn=1024):
    mp, kdim = xp.shape
    ms = xs.shape[0]
    assert mp % bm == 0 and n % bn == 0 and col0 % bn == 0 and w.shape[2 if transposed else 1] == kdim
    nm, cb0, last = mp // bm, col0 // bn, mp // bm - 1
    row = lambda j, i: (jnp.minimum(i, last), 0)
    tile = lambda j, i: (jnp.minimum(i, last), j)
    if transposed:
        w_spec = pl.BlockSpec((None, bn, kdim), lambda j, i: (layer, cb0 + j, 0))
    else:
        w_spec = pl.BlockSpec((None, kdim, bn), lambda j, i: (layer, 0, cb0 + j))
    in_specs = [pl.BlockSpec((bm, kdim), row), pl.BlockSpec((ms, kdim), lambda j, i: (0, 0)), w_spec]
    args = [xp, xs, w]
    if res is not None:
        in_specs += [pl.BlockSpec((bm, bn), tile), pl.BlockSpec((ms, bn), lambda j, i: (0, j))]
        args += list(res)
    return pl.pallas_call(
        functools.partial(_pair_matmul_kernel, nm=nm, has_res=res is not None, out_scale=out_scale,
                          transposed=transposed),
        grid=(n // bn, nm + 1),
        in_specs=in_specs,
        out_specs=[pl.BlockSpec((bm, bn), tile), pl.BlockSpec((ms, bn), lambda j, i: (0, j))],
        out_shape=[jax.ShapeDtypeStruct((mp, n), out_dtype), jax.ShapeDtypeStruct((ms, n), out_dtype)],
        scratch_shapes=[pltpu.VMEM((bn, kdim) if transposed else (kdim, bn), BF16)],
        compiler_params=_params("parallel", "arbitrary"),
        name="matmul_pair",
    )(*args)


def _gate_kernel(x_ref, w_ref, b_ref, o_ref):
    z = lax.dot_general(x_ref[...], w_ref[...].astype(BF16), (((1,), (1,)), ((), ())),
                        preferred_element_type=F32) + b_ref[...]
    o_ref[...] = -(jnp.maximum(-z, 0.0) + jnp.log1p(jnp.exp(-jnp.abs(z))))


def forget_gate(xn, w_t, layer, row0, b_f):
    m, d = xn.shape
    nh = b_f.shape[1]
    tm = min(512, m)
    assert row0 % nh == 0 and nh % 8 == 0
    return pl.pallas_call(
        _gate_kernel,
        grid=(m // tm,),
        in_specs=[pl.BlockSpec((tm, d), lambda i: (i, 0)),
                  pl.BlockSpec((None, nh, d), lambda i: (layer, row0 // nh, 0)),
                  pl.BlockSpec((1, nh), lambda i: (0, 0))],
        out_specs=pl.BlockSpec((tm, nh), lambda i: (i, 0)),
        out_shape=jax.ShapeDtypeStruct((m, nh), F32),
        compiler_params=_params("parallel"),
        name="forget_gate",
    )(xn, w_t, b_f)


def _split_dot(tri, x):
    hi = x.astype(BF16)
    r1 = x - hi.astype(F32)
    mid = r1.astype(BF16)
    lo = (r1 - mid.astype(F32)).astype(BF16)
    out = jnp.dot(tri, lo, preferred_element_type=F32)
    out = out + jnp.dot(tri, mid, preferred_element_type=F32)
    return out + jnp.dot(tri, hi, preferred_element_type=F32)


def _cumsum_kernel(x_ref, o_ref, *, chunk):
    s = x_ref.shape[0]
    row = lax.broadcasted_iota(jnp.int32, (chunk, chunk), 0)
    col = lax.broadcasted_iota(jnp.int32, (chunk, chunk), 1)
    tri = jnp.where(col <= row, 1.0, 0.0).astype(BF16)
    carry = jnp.zeros((1, x_ref.shape[1]), F32)
    for i in range(s // chunk):
        y = _split_dot(tri, x_ref[i * chunk:(i + 1) * chunk, :]) + carry
        o_ref[i * chunk:(i + 1) * chunk, :] = y
        carry = y[chunk - 1:chunk, :]


def _cumsum_small_kernel(x_ref, o_ref):
    x = x_ref[...]
    t = x.shape[0]
    row = lax.broadcasted_iota(jnp.int32, x.shape, 0)
    for i in range(t):
        o_ref[i:i + 1, :] = jnp.sum(jnp.where(row <= i, x, 0.0), axis=0, keepdims=True)


def cumsum_rows(x):
    b, s, nh = x.shape
    body = _cumsum_small_kernel if s < 128 else functools.partial(_cumsum_kernel, chunk=min(256, s))
    return pl.pallas_call(
        body,
        grid=(b,),
        in_specs=[pl.BlockSpec((None, s, nh), lambda i: (i, 0, 0))],
        out_specs=pl.BlockSpec((None, s, nh), lambda i: (i, 0, 0)),
        out_shape=jax.ShapeDtypeStruct((b, s, nh), F32),
        compiler_params=_params("parallel"),
        name="cumsum_rows",
    )(x)


SUFFIX_PAGES_PER_STEP = 16


def _page_suffix_kernel(pt_ref, *refs, pps):
    lf_refs, o_ref, carry_ref = refs[:pps], refs[pps], refs[pps + 1]
    j = pl.program_id(1)
    n = lf_refs[0].shape[0]

    @pl.when(j == 0)
    def _():
        carry_ref[...] = jnp.zeros_like(carry_ref)

    row = lax.broadcasted_iota(jnp.int32, (n, n), 0)
    col = lax.broadcasted_iota(jnp.int32, (n, n), 1)
    upper = jnp.where(col > row, 1.0, 0.0).astype(BF16)
    carry = carry_ref[...]
    for pp in reversed(range(pps)):
        x = lf_refs[pp][...]
        o_ref[pp] = _split_dot(upper, x) + carry
        carry = carry + jnp.sum(x, axis=0, keepdims=True)
    carry_ref[...] = carry


def page_suffix_sums(page_table_flat, lf_pool, b, n_pages):
    _, n, nh = lf_pool.shape
    pps = SUFFIX_PAGES_PER_STEP
    assert n_pages % pps == 0
    n_steps = n_pages // pps

    def page_spec(pp):
        return pl.BlockSpec((None, n, nh), lambda i, j, pt: (pt[i * n_pages + n_pages - (j + 1) * pps + pp], 0, 0))

    grid_spec = pltpu.PrefetchScalarGridSpec(
        num_scalar_prefetch=1,
        grid=(b, n_steps),
        in_specs=[page_spec(pp) for pp in range(pps)],
        out_specs=pl.BlockSpec((None, pps, n, nh), lambda i, j, pt: (i, n_steps - 1 - j, 0, 0)),
        scratch_shapes=[pltpu.VMEM((1, nh), F32)],
    )
    return pl.pallas_call(
        functools.partial(_page_suffix_kernel, pps=pps),
        grid_spec=grid_spec,
        out_shape=jax.ShapeDtypeStruct((b, n_pages, n, nh), F32),
        compiler_params=_params("parallel", "arbitrary"),
        name="page_suffix_sums",
    )(page_table_flat, *([lf_pool] * pps))


def _fox_prompt_kernel(q_ref, k_ref, v_ref, c_ref, ct_ref, o_ref, *, nq):
    tq = q_ref.shape[0]
    h = pl.program_id(1)
    i = pl.program_id(2)
    q = q_ref[...]
    lane = lax.broadcasted_iota(jnp.int32, c_ref.shape, 1)
    cq = jnp.sum(jnp.where(lane == h, c_ref[...], 0.0), axis=-1, keepdims=True)
    dn = (((1,), (1,)), ((), ()))

    def scores(j):
        kb = k_ref[j * tq:(j + 1) * tq, :].astype(BF16)
        return lax.dot_general(q, kb, dn, preferred_element_type=F32) + (cq - ct_ref[j:j + 1, :])

    def update(j, s, carry, masked):
        m, l, acc = carry
        if masked:
            row = lax.broadcasted_iota(jnp.int32, s.shape, 0)
            col = lax.broadcasted_iota(jnp.int32, s.shape, 1)
            s = jnp.where(col <= row, s, NEG)
        m_new = jnp.maximum(m, jnp.max(s, axis=-1, keepdims=True))
        alpha = jnp.exp(m - m_new)
        p = jnp.exp(s - m_new)
        l = l * alpha + jnp.sum(p, axis=-1, keepdims=True)
        vb = v_ref[j * tq:(j + 1) * tq, :].astype(BF16)
        acc = acc * alpha + jnp.dot(p.astype(BF16), vb, preferred_element_type=F32)
        return m_new, l, acc

    for ii in range(nq):

        @pl.when(i == ii)
        def _(ii=ii):
            carry = (jnp.full((tq, 1), NEG, F32), jnp.zeros((tq, 1), F32), jnp.zeros((tq, HEAD_DIM), F32))
            s_next = scores(0)
            for j in range(ii + 1):
                s_cur = s_next
                if j < ii:
                    s_next = scores(j + 1)
                carry = update(j, s_cur, carry, masked=(j == ii))
            _, l, acc = carry
            o_ref[...] = (acc / l).astype(o_ref.dtype)


def fox_prompt_attention(q, k, v, c, b, s, n_heads, tq=512):
    nq = s // tq
    ct = c.reshape(b, s, n_heads).transpose(0, 2, 1).reshape(b * n_heads, nq, tq)
    return pl.pallas_call(
        functools.partial(_fox_prompt_kernel, nq=nq),
        grid=(b, n_heads, nq),
        in_specs=[
            pl.BlockSpec((tq, HEAD_DIM), lambda bi, h, i: (bi * nq + i, h)),
            pl.BlockSpec((s, HEAD_DIM), lambda bi, h, i: (bi, h)),
            pl.BlockSpec((s, HEAD_DIM), lambda bi, h, i: (bi, h)),
            pl.BlockSpec((tq, n_heads), lambda bi, h, i: (bi * nq + i, 0)),
            pl.BlockSpec((None, nq, tq), lambda bi, h, i: (bi * n_heads + h, 0, 0)),
        ],
        out_specs=pl.BlockSpec((tq, HEAD_DIM), lambda bi, h, i: (bi * nq + i, h)),
        out_shape=jax.ShapeDtypeStruct(q.shape, BF16),
        compiler_params=_params("parallel", "parallel", "arbitrary"),
        name="fox_prompt_attention",
    )(q, k, v, c, ct)


HEADS_PER_TILE = 8


FOX_PAGES_PER_STEP = 2


def _tile_rows(ref, sub):
    rows, n_sub, dh = ref.shape
    return ref.reshape(rows * n_sub, dh)[pl.ds(sub, rows, stride=n_sub), :]


def _fox_sample_kernel(pt_ref, q_ref, cn_ref, cnt_ref, dt_ref, kn_ref, vn_ref, *refs, n_heads, n_steps, pps):
    ng = n_heads // HEADS_PER_TILE
    nb = ng * pps
    k_refs, v_refs = refs[:nb], refs[nb:2 * nb]
    o_ref, m_ref, l_ref, acc_ref = refs[2 * nb:]
    j = pl.program_id(1)
    t = q_ref.shape[1]
    dn = (((1,), (1,)), ((), ()))

    @pl.when(j == 0)
    def _():
        m_ref[...] = jnp.full_like(m_ref, NEG)
        l_ref[...] = jnp.zeros_like(l_ref)
        acc_ref[...] = jnp.zeros_like(acc_ref)

    def head_rows(page_refs, h):
        g, hs = divmod(h, HEADS_PER_TILE)
        tiles = [_tile_rows(page_refs[pp * ng + g], hs).astype(BF16) for pp in range(pps)]
        return tiles[0] if pps == 1 else jnp.concatenate(tiles, axis=0)

    def online_update(s, pv):
        m_old = m_ref[...]
        m_new = jnp.maximum(m_old, jnp.max(s, axis=-1, keepdims=True))
        alpha = jnp.exp(m_old - m_new)
        p = jnp.exp(s - m_new)
        l_ref[...] = l_ref[...] * alpha + jnp.sum(p, axis=-1, keepdims=True)
        acc_ref[...] = acc_ref[...] * alpha + pv(p)
        m_ref[...] = m_new

    parts = []
    for h in range(n_heads):
        sh = lax.dot_general(q_ref[h], head_rows(k_refs, h), dn, preferred_element_type=F32)
        d_row = [dt_ref[pp, h:h + 1, :] for pp in range(pps)]
        parts.append(sh + (d_row[0] if pps == 1 else jnp.concatenate(d_row, axis=1)))
    cn = cn_ref[...]
    s = jnp.concatenate(parts, axis=0) + (cn if pps == 1 else jnp.concatenate([cn] * pps, axis=1))

    def pv_pages(p):
        return jnp.concatenate(
            [jnp.dot(p[h * t:(h + 1) * t].astype(BF16), head_rows(v_refs, h), preferred_element_type=F32)
             for h in range(n_heads)], axis=0)

    online_update(s, pv_pages)

    @pl.when(j == n_steps - 1)
    def _():
        parts = []
        for h in range(n_heads):
            kh = kn_ref[:, h * HEAD_DIM:(h + 1) * HEAD_DIM].astype(BF16)
            parts.append(lax.dot_general(q_ref[h], kh, dn, preferred_element_type=F32))
        s2 = jnp.concatenate(parts, axis=0) + cn_ref[:, :t] - cnt_ref[...]
        row = lax.broadcasted_iota(jnp.int32, s2.shape, 0)
        col = lax.broadcasted_iota(jnp.int32, s2.shape, 1)
        s2 = jnp.where(col <= lax.rem(row, t), s2, NEG)

        def pv_new(p):
            return jnp.concatenate(
                [jnp.dot(p[h * t:(h + 1) * t], vn_ref[:, h * HEAD_DIM:(h + 1) * HEAD_DIM], preferred_element_type=F32)
                 for h in range(n_heads)], axis=0)

        online_update(s2, pv_new)
        o_ref[...] = acc_ref[...] / l_ref[...]


def fox_sample_attention(q, c_new, d_past, k_new, v_new, k_pool, v_pool, page_table_flat, b, t, n_heads, n_pages):
    ng = n_heads // HEADS_PER_TILE
    pps = FOX_PAGES_PER_STEP
    assert n_pages % pps == 0
    n_steps = n_pages // pps
    n_pool = k_pool.shape[0]
    ht = n_heads * t
    q4 = q.reshape(b, t, n_heads, HEAD_DIM).transpose(0, 2, 1, 3)
    c_ht = c_new.transpose(0, 2, 1).reshape(b, ht, 1)
    cn = jnp.broadcast_to(c_ht, (b, ht, PAGE_SIZE))
    cnt = jnp.broadcast_to(c_new.transpose(0, 2, 1)[:, :, None, :], (b, n_heads, t, t)).reshape(b, ht, t)
    dt = d_past.transpose(0, 1, 3, 2)
    kp = k_pool.reshape(n_pool, PAGE_SIZE, ng, HEADS_PER_TILE, HEAD_DIM)
    vp = v_pool.reshape(n_pool, PAGE_SIZE, ng, HEADS_PER_TILE, HEAD_DIM)

    def page_spec(pp, g):
        return pl.BlockSpec((None, PAGE_SIZE, None, HEADS_PER_TILE, HEAD_DIM),
                            lambda i, j, pt: (pt[i * n_pages + j * pps + pp], 0, g, 0, 0))

    page_specs = [page_spec(pp, g) for pp in range(pps) for g in range(ng)]
    d = n_heads * HEAD_DIM
    grid_spec = pltpu.PrefetchScalarGridSpec(
        num_scalar_prefetch=1,
        grid=(b, n_steps),
        in_specs=[
            pl.BlockSpec((None, n_heads, t, HEAD_DIM), lambda i, j, pt: (i, 0, 0, 0)),
            pl.BlockSpec((None, ht, PAGE_SIZE), lambda i, j, pt: (i, 0, 0)),
            pl.BlockSpec((None, ht, t), lambda i, j, pt: (i, 0, 0)),
            pl.BlockSpec((None, pps, n_heads, PAGE_SIZE), lambda i, j, pt: (i, j, 0, 0)),
            pl.BlockSpec((t, d), lambda i, j, pt: (i, 0)),
            pl.BlockSpec((t, d), lambda i, j, pt: (i, 0)),
        ] + page_specs * 2,
        out_specs=pl.BlockSpec((None, ht, HEAD_DIM), lambda i, j, pt: (i, 0, 0)),
        scratch_shapes=[pltpu.VMEM((ht, 1), F32), pltpu.VMEM((ht, 1), F32), pltpu.VMEM((ht, HEAD_DIM), F32)],
    )
    o = pl.pallas_call(
        functools.partial(_fox_sample_kernel, n_heads=n_heads, n_steps=n_steps, pps=pps),
        grid_spec=grid_spec,
        out_shape=jax.ShapeDtypeStruct((b, ht, HEAD_DIM), F32),
        compiler_params=_params("parallel", "arbitrary"),
        name="fox_sample_attention",
    )(page_table_flat, q4, cn, cnt, dt, k_new, v_new, *([kp] * (ng * pps)), *([vp] * (ng * pps)))
    return o.reshape(b, n_heads, t, HEAD_DIM).transpose(0, 2, 1, 3).reshape(b * t, d)


def _t5_bucket(dist):
    n = np.asarray(dist, dtype=np.int64)
    exact = N_BUCKETS // 2
    large = exact + (np.log(np.maximum(n, 1) / exact) / np.log(BUCKET_MAX_DIST / exact) * (N_BUCKETS - exact)).astype(np.int64)
    return np.where(n < exact, n, np.minimum(large, N_BUCKETS - 1)).astype(np.int32)


def _band_bias(rel_bias, group_heads):
    a = np.arange(Q_BLOCK)[:, None]
    bcol = np.arange(2 * Q_BLOCK)[None, :]
    j = a - bcol + Q_BLOCK
    out = []
    for g, (window, dil) in enumerate(PATTERNS):
        assert window // dil == Q_BLOCK
        valid = (j >= 0) & (j <= window // dil)
        bucket = _t5_bucket(dil * np.clip(j, 0, window // dil))
        bias = _select_buckets(rel_bias[:, g * group_heads:(g + 1) * group_heads], bucket)
        out.append(jnp.where(valid[None], bias, NEG))
    return jnp.stack(out)


def _select_buckets(rb, bucket):
    onehot = (jnp.asarray(bucket.reshape(-1))[None, :] == jnp.arange(N_BUCKETS)[:, None]).astype(F32)
    sel = jnp.einsum("bh,bn->hn", rb.astype(F32), onehot, precision=lax.Precision.HIGHEST)
    return sel.reshape((rb.shape[1],) + bucket.shape)


DILATED_BLOCKS_PER_STEP = 4


def _dilated_prompt_kernel(*refs, s):
    n_q = N_GROUPS * 2
    q_refs = refs[:n_q]
    k_ref, v_ref, bias_ref, o_ref, m_ref, l_ref, acc_ref = refs[n_q:]
    qb_rows = Q_BLOCK
    dn = (((1,), (1,)), ((), ()))

    def blocks(g, specs, merge):
        bias = bias_ref[g].reshape(2 * qb_rows, 2 * qb_rows)
        scores = []
        for q_idx, k_idx, first in specs:
            q2 = jnp.concatenate([q_refs[2 * g + r][q_idx, :].astype(BF16) for r in range(2)], axis=0)
            sc = lax.dot_general(q2, k_ref[k_idx, :].astype(BF16), dn, preferred_element_type=F32)
            scores.append(sc + (bias[:, qb_rows:] if first else bias))
        stats = []
        for sc in scores:
            m_blk = jnp.max(sc, axis=-1, keepdims=True)
            p = jnp.exp(sc - m_blk)
            stats.append((m_blk, jnp.sum(p, axis=-1, keepdims=True), p.astype(BF16)))
        outs = [jnp.dot(p, v_ref[k_idx, :].astype(BF16), preferred_element_type=F32)
                for (_, _, p), (_, k_idx, _) in zip(stats, specs)]
        for (m_blk, l_blk, _), o_blk, (q_idx, _, _) in zip(stats, outs, specs):
            for r in range(2):
                rs = slice(r * qb_rows, (r + 1) * qb_rows)
                m_b = jnp.broadcast_to(m_blk[rs], (qb_rows, HEAD_DIM))
                l_b = jnp.broadcast_to(l_blk[rs], (qb_rows, HEAD_DIM))
                if merge:
                    m_old = m_ref[r, q_idx, :]
                    m_new = jnp.maximum(m_old, m_b)
                    a_old = jnp.exp(m_old - m_new)
                    a_blk = jnp.exp(m_b - m_new)
                    l_ref[r, q_idx, :] = l_ref[r, q_idx, :] * a_old + l_b * a_blk
                    acc_ref[r, q_idx, :] = acc_ref[r, q_idx, :] * a_old + o_blk[rs] * a_blk
                    m_ref[r, q_idx, :] = m_new
                else:
                    m_ref[r, q_idx, :] = m_b
                    l_ref[r, q_idx, :] = l_b
                    acc_ref[r, q_idx, :] = o_blk[rs]

    per_step = DILATED_BLOCKS_PER_STEP
    for g, (_, dil) in enumerate(PATTERNS):
        n_blocks = s // (dil * qb_rows)
        merge = g > 0

        def idx(start, size, dil=dil):
            return pl.ds(start, size) if dil == 1 else pl.ds(start, size, stride=dil)

        def spec(c, qb, dil=dil, idx=idx):
            q0 = c + dil * qb_rows * qb
            if isinstance(qb, int) and qb == 0:
                return idx(q0, qb_rows), idx(q0, qb_rows), True
            return idx(q0, qb_rows), idx(q0 - dil * qb_rows, 2 * qb_rows), False

        if n_blocks == 1:
            assert dil % per_step == 0

            def classes(it, carry, g=g, merge=merge, spec=spec):
                blocks(g, [spec(it * per_step + u, 0) for u in range(per_step)], merge)
                return carry

            lax.fori_loop(0, dil // per_step, classes, 0)
        elif n_blocks <= per_step:

            def one_class(c, carry, g=g, merge=merge, spec=spec, n_blocks=n_blocks):
                blocks(g, [spec(c, qb) for qb in range(n_blocks)], merge)
                return carry

            lax.fori_loop(0, dil, one_class, 0)
        else:
            assert dil == 1 and (n_blocks - 1) % (per_step - 1) == 0
            blocks(g, [spec(0, 0)], merge)
            grp = per_step - 1

            def later(it, carry, g=g, merge=merge, spec=spec, grp=grp):
                blocks(g, [spec(0, 1 + it * grp + u) for u in range(grp)], merge)
                return carry

            lax.fori_loop(0, (n_blocks - 1) // grp, later, 0)

    for r in range(2):
        o_ref[:, r * HEAD_DIM:(r + 1) * HEAD_DIM] = (acc_ref[r] / l_ref[r]).astype(o_ref.dtype)


def dilated_prompt_attention(q, k, v, rel_bias, b, s, n_kv, group_heads):
    assert group_heads == 2 * n_kv
    bias = _band_bias(rel_bias, group_heads).reshape(N_GROUPS, n_kv, 2, Q_BLOCK, 2 * Q_BLOCK)

    def q_spec(g, r):
        return pl.BlockSpec((s, HEAD_DIM), lambda bi, kv: (bi, g * group_heads + 2 * kv + r))

    return pl.pallas_call(
        functools.partial(_dilated_prompt_kernel, s=s),
        grid=(b, n_kv),
        in_specs=[q_spec(g, r) for g in range(N_GROUPS) for r in range(2)] + [
            pl.BlockSpec((s, HEAD_DIM), lambda bi, kv: (bi, kv)),
            pl.BlockSpec((s, HEAD_DIM), lambda bi, kv: (bi, kv)),
            pl.BlockSpec((N_GROUPS, None, 2, Q_BLOCK, 2 * Q_BLOCK), lambda bi, kv: (0, kv, 0, 0, 0)),
        ],
        out_specs=pl.BlockSpec((s, 2 * HEAD_DIM), lambda bi, kv: (bi, kv)),
        out_shape=jax.ShapeDtypeStruct((b * s, group_heads * HEAD_DIM), BF16),
        scratch_shapes=[pltpu.VMEM((2, s, HEAD_DIM), F32)] * 3,
        compiler_params=_params("parallel", "parallel"),
        name="dilated_prompt_attention",
    )(*([q] * (N_GROUPS * 2)), k, v, bias)


def _sample_bias(rel_bias, group_heads, n_kv, t, n_buf):
    dist = n_buf + np.arange(t)[:, None] - np.arange(n_buf + t)[None, :]
    out = []
    for g, (window, dil) in enumerate(PATTERNS):
        valid = (dist >= 0) & (dist % dil == 0) & (dist <= window)
        bucket = _t5_bucket(np.clip(dist, 0, window))
        bias = _select_buckets(rel_bias[:, g * group_heads:(g + 1) * group_heads], bucket)
        out.append(jnp.where(valid[None], bias, NEG).reshape(n_kv, 2, t, n_buf + t))
    return jnp.stack(out, axis=1).reshape(n_kv, N_GROUPS * 2 * t, n_buf + t)


def _dilated_sample_kernel(q_ref, kc_ref, vc_ref, kn_ref, vn_ref, bc_ref, bn_ref, o_ref, *, n_kv, group_heads):
    t = q_ref.shape[0]
    n_buf = kc_ref.shape[0]
    for kv in range(n_kv):
        heads = [g * group_heads + 2 * kv + r for g in range(N_GROUPS) for r in range(2)]
        q6 = jnp.concatenate([q_ref[:, h * HEAD_DIM:(h + 1) * HEAD_DIM] for h in heads], axis=0).astype(BF16)
        kc = _tile_rows(kc_ref, kv).astype(BF16)
        vc = _tile_rows(vc_ref, kv).astype(BF16)
        sl = slice(kv * HEAD_DIM, (kv + 1) * HEAD_DIM)
        kn = kn_ref[:, sl].astype(BF16)
        vn = vn_ref[:, sl]
        dn = (((1,), (1,)), ((), ()))
        s_c = lax.dot_general(q6, kc, dn, preferred_element_type=F32) + bc_ref[kv]
        s_n = lax.dot_general(q6, kn, dn, preferred_element_type=F32) + bn_ref[kv]
        m_row = jnp.maximum(jnp.max(s_c, axis=-1, keepdims=True), jnp.max(s_n, axis=-1, keepdims=True))
        m = jnp.max(m_row.reshape(N_GROUPS, 2 * t, 1), axis=0)
        m_all = jnp.concatenate([m] * N_GROUPS, axis=0)
        p_c = jnp.exp(s_c - m_all)
        p_n = jnp.exp(s_n - m_all)
        l_row = jnp.sum(p_c, axis=-1, keepdims=True) + jnp.sum(p_n, axis=-1, keepdims=True)
        o_row = jnp.dot(p_c.astype(BF16), vc, preferred_element_type=F32) + jnp.dot(p_n, vn, preferred_element_type=F32)
        l = jnp.sum(l_row.reshape(N_GROUPS, 2 * t, 1), axis=0)
        o = jnp.sum(o_row.reshape(N_GROUPS, 2 * t, HEAD_DIM), axis=0) / l
        for r in range(2):
            o_ref[:, (2 * kv + r) * HEAD_DIM:(2 * kv + r + 1) * HEAD_DIM] = o[r * t:(r + 1) * t]


def dilated_sample_attention(q, k_new, v_new, cache_k, cache_v, rel_bias, b, t, n_kv, group_heads):
    n_buf = cache_k.shape[1]
    bias = _sample_bias(rel_bias, group_heads, n_kv, t, n_buf)
    bias_c, bias_n = bias[:, :, :n_buf], bias[:, :, n_buf:]
    dq = q.shape[1]
    dkv = n_kv * HEAD_DIM
    return pl.pallas_call(
        functools.partial(_dilated_sample_kernel, n_kv=n_kv, group_heads=group_heads),
        grid=(b,),
        in_specs=[
            pl.BlockSpec((t, dq), lambda i: (i, 0)),
            pl.BlockSpec((None, n_buf, n_kv, HEAD_DIM), lambda i: (i, 0, 0, 0)),
            pl.BlockSpec((None, n_buf, n_kv, HEAD_DIM), lambda i: (i, 0, 0, 0)),
            pl.BlockSpec((t, dkv), lambda i: (i, 0)),
            pl.BlockSpec((t, dkv), lambda i: (i, 0)),
            pl.BlockSpec(bias_c.shape, lambda i: (0, 0, 0)),
            pl.BlockSpec(bias_n.shape, lambda i: (0, 0, 0)),
        ],
        out_specs=pl.BlockSpec((t, group_heads * HEAD_DIM), lambda i: (i, 0)),
        out_shape=jax.ShapeDtypeStruct((b * t, group_heads * HEAD_DIM), F32),
        compiler_params=_params("parallel"),
        name="dilated_sample_attention",
    )(q, cache_k, cache_v, k_new, v_new, bias_c, bias_n)


CONV_PAD_ROWS = 8


FFN_COL_TILE = 256
FFN_ROW_CHUNKS = 4


def _ffn_hidden_kernel(xp_ref, xs_ref, wu_ref, wg_ref, prev_ref, wc_ref, bc_ref, hp_ref, hs_ref, stp_ref, sts_ref,
                       pad_ref, *, n_seq_s, t_s):
    i = pl.program_id(0)
    t = xp_ref.shape[0]
    lo = CONV_PAD_ROWS - (CONV_W - 1)
    rc = t // FFN_ROW_CHUNKS
    wu = wu_ref[...].astype(BF16)
    wg = wg_ref[...].astype(BF16)

    def conv_gelu_gate(gate, start, rows):
        uc = bc_ref[...] + pad_ref[pl.ds(lo + start, rows), :] * wc_ref[0:1, :]
        for tap in range(1, CONV_W):
            uc = uc + pad_ref[pl.ds(lo + start + tap, rows), :] * wc_ref[tap:tap + 1, :]
        return 0.5 * uc * (1.0 + lax.erf(uc * SQRT_HALF)) * gate

    @pl.when(i == 0)
    def _():
        x = xs_ref[...]
        u = jnp.dot(x, wu, preferred_element_type=F32)
        gate = jnp.dot(x, wg, preferred_element_type=F32)
        pieces = []
        for sq in range(n_seq_s):
            rs = slice(sq * t_s, (sq + 1) * t_s)
            pad_ref[pl.ds(CONV_PAD_ROWS, t_s), :] = u[rs]
            pad_ref[pl.ds(lo, CONV_W - 1), :] = prev_ref[sq]
            pieces.append(conv_gelu_gate(gate[rs], 0, t_s))
            sts_ref[sq] = pad_ref[pl.ds(lo + t_s, CONV_W - 1), :]
        hs_ref[...] = jnp.concatenate(pieces, axis=0).astype(hs_ref.dtype)

    pad_ref[pl.ds(lo, CONV_W - 1), :] = jnp.zeros((CONV_W - 1, pad_ref.shape[1]), F32)
    gates = []
    for c in range(FFN_ROW_CHUNKS):
        x = xp_ref[c * rc:(c + 1) * rc, :]
        pad_ref[pl.ds(CONV_PAD_ROWS + c * rc, rc), :] = jnp.dot(x, wu, preferred_element_type=F32)
        gates.append(jnp.dot(x, wg, preferred_element_type=F32))
    for c in range(FFN_ROW_CHUNKS):
        hp_ref[c * rc:(c + 1) * rc, :] = conv_gelu_gate(gates[c], c * rc, rc).astype(hp_ref.dtype)
    stp_ref[...] = pad_ref[pl.ds(lo + t, CONV_W - 1), :]


def conv_ffn_hidden(xp, xs, w_up, w_gate, layer, prev_s, w_conv, b_conv, n_seq_p, t_p, n_seq_s, t_s):
    mp, d = xp.shape
    ms = xs.shape[0]
    f = w_up.shape[2]
    tn = FFN_COL_TILE
    assert mp == n_seq_p * t_p and ms == n_seq_s * t_s and f % tn == 0 and t_p % (FFN_ROW_CHUNKS * 128) == 0
    last = f // tn - 1
    wspec = pl.BlockSpec((None, d, tn), lambda i, j: (layer, 0, j))
    s_col = lambda i, j: jnp.where(i == 0, j, last)
    return pl.pallas_call(
        functools.partial(_ffn_hidden_kernel, n_seq_s=n_seq_s, t_s=t_s),
        grid=(n_seq_p, f // tn),
        in_specs=[
            pl.BlockSpec((t_p, d), lambda i, j: (i, 0), pipeline_mode=pl.Buffered(1)),
            pl.BlockSpec((ms, d), lambda i, j: (0, 0)),
            wspec, wspec,
            pl.BlockSpec((None, n_seq_s, CONV_W - 1, tn), lambda i, j: (layer, 0, 0, s_col(i, j))),
            pl.BlockSpec((None, CONV_W, tn), lambda i, j: (layer, 0, j)),
            pl.BlockSpec((None, 1, tn), lambda i, j: (layer, 0, j)),
        ],
        out_specs=[
            pl.BlockSpec((t_p, tn), lambda i, j: (i, j)),
            pl.BlockSpec((ms, tn), lambda i, j: (0, s_col(i, j))),
            pl.BlockSpec((None, CONV_W - 1, tn), lambda i, j: (i, 0, j)),
            pl.BlockSpec((n_seq_s, CONV_W - 1, tn), lambda i, j: (0, 0, s_col(i, j))),
        ],
        out_shape=[jax.ShapeDtypeStruct((mp, f), BF16), jax.ShapeDtypeStruct((ms, f), BF16),
                   jax.ShapeDtypeStruct((n_seq_p, CONV_W - 1, f), F32),
                   jax.ShapeDtypeStruct((n_seq_s, CONV_W - 1, f), F32)],
        scratch_shapes=[pltpu.VMEM((t_p + CONV_PAD_ROWS, tn), F32)],
        compiler_params=_params("arbitrary", "arbitrary"),
        name="conv_ffn_hidden",
    )(xp, xs, w_up, w_gate, prev_s, w_conv, b_conv.reshape(b_conv.shape[0], 1, f))


def _shift_append_kernel(old_ref, new_ref, o_ref):
    n_keep = old_ref.shape[1]
    o_ref[:n_keep] = old_ref[0]
    o_ref[n_keep:] = new_ref[...]


def shift_append(cache, new):
    b, n_buf, kv, dh = cache.shape
    t = new.shape[1]
    old_block = (pl.Element(1), pl.Element(n_buf - t), pl.Element(kv), pl.Element(dh))
    return pl.pallas_call(
        _shift_append_kernel,
        grid=(b,),
        in_specs=[pl.BlockSpec(old_block, lambda i: (i, t, 0, 0)),
                  pl.BlockSpec((None, t, kv, dh), lambda i: (i, 0, 0, 0))],
        out_specs=pl.BlockSpec((None, n_buf, kv, dh), lambda i: (i, 0, 0, 0)),
        out_shape=jax.ShapeDtypeStruct(cache.shape, cache.dtype),
        compiler_params=_params("parallel"),
        name="shift_append",
    )(cache, new)


def kernel(x_prompt, x_sample, cache_fox_k, cache_fox_v, cache_fox_logf, cache_win_k, cache_win_v, state_ffn_conv,
           page_table, w_in_a, b_f_a, w_o_a, g_attn, g_kv, w_kv_b, w_q_b, w_o_b, rel_bias,
           g_ffn, w_up, w_gate, w_conv, b_conv, w_down, g_final):
    bp, sp, d_model = x_prompt.shape
    bs, ts, _ = x_sample.shape
    n_heads = b_f_a.shape[1]
    hd = n_heads * HEAD_DIM
    n_kv = cache_win_k.shape[2]
    kvw = n_kv * HEAD_DIM
    group_heads = w_q_b.shape[2] // (N_GROUPS * HEAD_DIM)
    d_ff = w_up.shape[2]
    n_pages = page_table.shape[1]
    assert w_in_a.shape[0] == 1 and w_q_b.shape[0] == 1 and g_attn.shape[0] == 2
    mp, ms = bp * sp, bs * ts

    b_f = b_f_a[0][None].astype(F32)
    w_in_t = jnp.swapaxes(w_in_a, 1, 2)
    w_down_b = w_down.astype(BF16)
    xp = x_prompt.reshape(mp, d_model)
    xs = x_sample.reshape(ms, d_model)

    def norm(x, gains, dtype=BF16):
        return rmsnorm(x, gains, dtype)

    def ffn(xp, xs, layer):
        (xnp,) = norm(xp, g_ffn[layer][None])
        (xns,) = norm(xs, g_ffn[layer][None])
        hp, hs, stp, sts = conv_ffn_hidden(xnp, xns, w_up, w_gate, layer, state_ffn_conv, w_conv, b_conv,
                                           bp, sp, bs, ts)
        xp = matmul(hp, w_down_b, layer, res=xp, bm=512, bn=512)
        xs = matmul(hs, w_down_b, layer, res=xs, bm=ms, bn=1024, bk=d_ff // 2)
        return xp, xs, stp, sts

    (xnp,) = norm(xp, g_attn[0][None])
    (xns,) = norm(xs, g_attn[0][None])
    qp, qs = matmul_pair(xnp, xns, w_in_t, 0, 0, hd, out_dtype=BF16, out_scale=ATTN_SCALE, transposed=True)
    kp, ks = matmul_pair(xnp, xns, w_in_t, 0, hd, hd, transposed=True)
    vp, vs = matmul_pair(xnp, xns, w_in_t, 0, 2 * hd, hd, transposed=True)
    lfp = forget_gate(xnp, w_in_t, 0, 3 * hd, b_f)
    lfs = forget_gate(xns, w_in_t, 0, 3 * hd, b_f)
    cp = cumsum_rows(lfp.reshape(bp, sp, n_heads))
    cs = cumsum_rows(lfs.reshape(bs, ts, n_heads))
    ap = fox_prompt_attention(qp, kp, vp, cp.reshape(mp, n_heads), bp, sp, n_heads)
    pt_flat = page_table.reshape(-1)
    d_past = page_suffix_sums(pt_flat, cache_fox_logf[0], bs, n_pages)
    a_s = fox_sample_attention(qs, cs, d_past, ks, vs, cache_fox_k[0], cache_fox_v[0], pt_flat, bs, ts, n_heads,
                               n_pages).astype(BF16)
    xp, xs = matmul_pair(ap, a_s, w_o_a, 0, 0, d_model, res=(xp, xs))
    xp, xs, st0p, st0s = ffn(xp, xs, 0)

    xnp, xkvp = norm(xp, jnp.stack([g_attn[1], g_kv]))
    xns, xkvs = norm(xs, jnp.stack([g_attn[1], g_kv]))
    kvp, kvs = matmul_pair(xkvp, xkvs, w_kv_b[None], 0, 0, 2 * kvw)
    qp, qs = matmul_pair(xnp, xns, w_q_b, 0, 0, w_q_b.shape[2], out_scale=ATTN_SCALE)
    kbp, vbp = kvp[:, :kvw], kvp[:, kvw:]
    kbs, vbs = kvs[:, :kvw], kvs[:, kvw:]
    ap = dilated_prompt_attention(qp, kbp, vbp, rel_bias, bp, sp, n_kv, group_heads)
    a_s = dilated_sample_attention(qs, kbs, vbs, cache_win_k, cache_win_v, rel_bias, bs, ts, n_kv,
                                   group_heads).astype(BF16)
    xp, xs = matmul_pair(ap, a_s, w_o_b, 0, 0, d_model, res=(xp, xs))
    xp, xs, st1p, st1s = ffn(xp, xs, 1)
    (yp,) = norm(xp, g_final[None], F32)
    (ys,) = norm(xs, g_final[None], F32)

    n_pg = sp // PAGE_SIZE
    keep = min(BUCKET_MAX_DIST, sp)
    return (
        yp.reshape(bp, sp, d_model),
        ys.reshape(bs, ts, d_model),
        kp.reshape(1, bp, n_pg, PAGE_SIZE, n_heads, HEAD_DIM),
        vp.reshape(1, bp, n_pg, PAGE_SIZE, n_heads, HEAD_DIM),
        lfp.reshape(1, bp, n_pg, PAGE_SIZE, n_heads),
        ks.reshape(1, bs, ts, n_heads, HEAD_DIM),
        vs.reshape(1, bs, ts, n_heads, HEAD_DIM),
        lfs.reshape(1, bs, ts, n_heads),
        kbp.reshape(bp, sp, n_kv, HEAD_DIM)[:, sp - keep:],
        vbp.reshape(bp, sp, n_kv, HEAD_DIM)[:, sp - keep:],
        shift_append(cache_win_k, kbs.reshape(bs, ts, n_kv, HEAD_DIM)),
        shift_append(cache_win_v, vbs.reshape(bs, ts, n_kv, HEAD_DIM)),
        jnp.stack([st0p, st1p]),
        jnp.stack([st0s, st1s]),
    )
```

```python
import functools
import math

import numpy as np
import jax
import jax.numpy as jnp
from jax import lax
from jax.experimental import pallas as pl
from jax.experimental.pallas import tpu as pltpu

F32 = jnp.float32
BF16 = jnp.bfloat16

HEAD_DIM = 128
PAGE_SIZE = 128
PATTERNS = ((128, 1), (512, 4), (2048, 16))
N_GROUPS = len(PATTERNS)
N_BUCKETS = 32
BUCKET_MAX_DIST = max(w for w, _ in PATTERNS)
CONV_W = 3
Q_BLOCK = 128
NORM_EPS = 1e-6
ATTN_SCALE = HEAD_DIM ** -0.5
NEG = -1e30
SQRT_HALF = 0.7071067811865476

V7X_VMEM_LIMIT_BYTES = 56 * 1024 * 1024


def _params(*semantics):
    return pltpu.CompilerParams(dimension_semantics=semantics, vmem_limit_bytes=V7X_VMEM_LIMIT_BYTES)


def _rmsnorm_kernel(x_ref, g_ref, *o_refs):
    x = x_ref[...]
    y = x * lax.rsqrt(jnp.mean(x * x, axis=-1, keepdims=True) + NORM_EPS)
    for j, o_ref in enumerate(o_refs):
        o_ref[...] = (y * g_ref[j:j + 1, :]).astype(o_ref.dtype)


def rmsnorm(x, gains, out_dtype):
    m, d = x.shape
    n = gains.shape[0]
    tm = min(256, m)
    return pl.pallas_call(
        _rmsnorm_kernel,
        grid=(m // tm,),
        in_specs=[pl.BlockSpec((tm, d), lambda i: (i, 0)), pl.BlockSpec((n, d), lambda i: (0, 0))],
        out_specs=[pl.BlockSpec((tm, d), lambda i: (i, 0))] * n,
        out_shape=[jax.ShapeDtypeStruct((m, d), out_dtype)] * n,
        compiler_params=_params("parallel"),
        name="rmsnorm",
    )(x, gains)


def _matmul_kernel(*refs, has_res, nk, out_scale):
    x_ref, w_ref = refs[0], refs[1]
    res_ref = refs[2] if has_res else None
    o_ref = refs[2 + has_res]

    def finish(acc):
        if out_scale != 1.0:
            acc = acc * out_scale
        if has_res:
            acc = acc + res_ref[...]
        o_ref[...] = acc.astype(o_ref.dtype)

    part = jnp.dot(x_ref[...], w_ref[...], preferred_element_type=F32)
    if nk == 1:
        finish(part)
        return
    acc_ref = refs[3 + has_res]
    k = pl.program_id(2)

    @pl.when(k == 0)
    def _():
        acc_ref[...] = part

    @pl.when(jnp.logical_and(k > 0, k < nk - 1))
    def _():
        acc_ref[...] += part

    @pl.when(k == nk - 1)
    def _():
        finish(acc_ref[...] + part)


def matmul(x, w, layer, *, res=None, out_dtype=F32, out_scale=1.0, bm=1024, bn=1024, bk=None):
    m, kdim = x.shape
    n = w.shape[2]
    bm, bn = min(bm, m), min(bn, n)
    bk = kdim if bk is None else bk
    nk = kdim // bk
    assert m % bm == 0 and n % bn == 0 and kdim % bk == 0 and nk >= 1
    in_specs = [pl.BlockSpec((bm, bk), lambda i, j, k: (i, k)),
                pl.BlockSpec((None, bk, bn), lambda i, j, k: (layer, k, j))]
    args = [x, w]
    if res is not None:
        in_specs.append(pl.BlockSpec((bm, bn), lambda i, j, k: (i, j)))
        args.append(res)
    return pl.pallas_call(
        functools.partial(_matmul_kernel, has_res=res is not None, nk=nk, out_scale=out_scale),
        grid=(m // bm, n // bn, nk),
        in_specs=in_specs,
        out_specs=pl.BlockSpec((bm, bn), lambda i, j, k: (i, j)),
        out_shape=jax.ShapeDtypeStruct((m, n), out_dtype),
        scratch_shapes=[pltpu.VMEM((bm, bn), F32)] if nk > 1 else [],
        compiler_params=_params("parallel", "parallel", "arbitrary"),
        name="matmul",
    )(*args)


def _pair_matmul_kernel(*refs, nm, has_res, out_scale, transposed):
    if has_res:
        xp_ref, xs_ref, w_ref, rp_ref, rs_ref, op_ref, os_ref, wb_ref = refs
    else:
        xp_ref, xs_ref, w_ref, op_ref, os_ref, wb_ref = refs
        rp_ref = rs_ref = None
    i = pl.program_id(1)

    @pl.when(i == 0)
    def _():
        wb_ref[...] = w_ref[...].astype(BF16)

    def emit(x_ref, r_ref, o_ref):
        dims = (((1,), (1 if transposed else 0,)), ((), ()))
        acc = lax.dot_general(x_ref[...], wb_ref[...], dims, preferred_element_type=F32)
        if out_scale != 1.0:
            acc = acc * out_scale
        if r_ref is not None:
            acc = acc + r_ref[...]
        o_ref[...] = acc.astype(o_ref.dtype)

    @pl.when(i < nm)
    def _():
        emit(xp_ref, rp_ref, op_ref)

    @pl.when(i == nm)
    def _():
        emit(xs_ref, rs_ref, os_ref)


def matmul_pair(xp, xs, w, layer, col0, n, *, res=None, out_dtype=F32, out_scale=1.0, transposed=False,
                bm=256, bn=1024):
    mp, kdim = xp.shape
    ms = xs.shape[0]
    assert mp % bm == 0 and n % bn == 0 and col0 % bn == 0 and w.shape[2 if transposed else 1] == kdim
    nm, cb0, last = mp // bm, col0 // bn, mp // bm - 1
    row = lambda j, i: (jnp.minimum(i, last), 0)
    tile = lambda j, i: (jnp.minimum(i, last), j)
    if transposed:
        w_spec = pl.BlockSpec((None, bn, kdim), lambda j, i: (layer, cb0 + j, 0))
    else:
        w_spec = pl.BlockSpec((None, kdim, bn), lambda j, i: (layer, 0, cb0 + j))
    in_specs = [pl.BlockSpec((bm, kdim), row), pl.BlockSpec((ms, kdim), lambda j, i: (0, 0)), w_spec]
    args = [xp, xs, w]
    if res is not None:
        in_specs += [pl.BlockSpec((bm, bn), tile), pl.BlockSpec((ms, bn), lambda j, i: (0, j))]
        args += list(res)
    return pl.pallas_call(
        functools.partial(_pair_matmul_kernel, nm=nm, has_res=res is not None, out_scale=out_scale,
                          transposed=transposed),
        grid=(n // bn, nm + 1),
        in_specs=in_specs,
        out_specs=[pl.BlockSpec((bm, bn), tile), pl.BlockSpec((ms, bn), lambda j, i: (0, j))],
        out_shape=[jax.ShapeDtypeStruct((mp, n), out_dtype), jax.ShapeDtypeStruct((ms, n), out_dtype)],
        scratch_shapes=[pltpu.VMEM((bn, kdim) if transposed else (kdim, bn), BF16)],
        compiler_params=_params("parallel", "arbitrary"),
        name="matmul_pair",
    )(*args)


def _gate_kernel(x_ref, w_ref, b_ref, o_ref):
    z = lax.dot_general(x_ref[...], w_ref[...].astype(BF16), (((1,), (1,)), ((), ())),
                        preferred_element_type=F32) + b_ref[...]
    o_ref[...] = -(jnp.maximum(-z, 0.0) + jnp.log1p(jnp.exp(-jnp.abs(z))))


def forget_gate(xn, w_t, layer, row0, b_f):
    m, d = xn.shape
    nh = b_f.shape[1]
    tm = min(512, m)
    assert row0 % nh == 0 and nh % 8 == 0
    return pl.pallas_call(
        _gate_kernel,
        grid=(m // tm,),
        in_specs=[pl.BlockSpec((tm, d), lambda i: (i, 0)),
                  pl.BlockSpec((None, nh, d), lambda i: (layer, row0 // nh, 0)),
                  pl.BlockSpec((1, nh), lambda i: (0, 0))],
        out_specs=pl.BlockSpec((tm, nh), lambda i: (i, 0)),
        out_shape=jax.ShapeDtypeStruct((m, nh), F32),
        compiler_params=_params("parallel"),
        name="forget_gate",
    )(xn, w_t, b_f)


def _split_dot(tri, x):
    hi = x.astype(BF16)
    r1 = x - hi.astype(F32)
    mid = r1.astype(BF16)
    lo = (r1 - mid.astype(F32)).astype(BF16)
    out = jnp.dot(tri, lo, preferred_element_type=F32)
    out = out + jnp.dot(tri, mid, preferred_element_type=F32)
    return out + jnp.dot(tri, hi, preferred_element_type=F32)


def _cumsum_kernel(x_ref, o_ref, *, chunk):
    s = x_ref.shape[0]
    row = lax.broadcasted_iota(jnp.int32, (chunk, chunk), 0)
    col = lax.broadcasted_iota(jnp.int32, (chunk, chunk), 1)
    tri = jnp.where(col <= row, 1.0, 0.0).astype(BF16)
    carry = jnp.zeros((1, x_ref.shape[1]), F32)
    for i in range(s // chunk):
        y = _split_dot(tri, x_ref[i * chunk:(i + 1) * chunk, :]) + carry
        o_ref[i * chunk:(i + 1) * chunk, :] = y
        carry = y[chunk - 1:chunk, :]


def _cumsum_small_kernel(x_ref, o_ref):
    x = x_ref[...]
    t = x.shape[0]
    row = lax.broadcasted_iota(jnp.int32, x.shape, 0)
    for i in range(t):
        o_ref[i:i + 1, :] = jnp.sum(jnp.where(row <= i, x, 0.0), axis=0, keepdims=True)


def cumsum_rows(x):
    b, s, nh = x.shape
    body = _cumsum_small_kernel if s < 128 else functools.partial(_cumsum_kernel, chunk=min(256, s))
    return pl.pallas_call(
        body,
        grid=(b,),
        in_specs=[pl.BlockSpec((None, s, nh), lambda i: (i, 0, 0))],
        out_specs=pl.BlockSpec((None, s, nh), lambda i: (i, 0, 0)),
        out_shape=jax.ShapeDtypeStruct((b, s, nh), F32),
        compiler_params=_params("parallel"),
        name="cumsum_rows",
    )(x)


SUFFIX_PAGES_PER_STEP = 16


def _page_suffix_kernel(pt_ref, *refs, pps):
    lf_refs, o_ref, carry_ref = refs[:pps], refs[pps], refs[pps + 1]
    j = pl.program_id(1)
    n = lf_refs[0].shape[0]

    @pl.when(j == 0)
    def _():
        carry_ref[...] = jnp.zeros_like(carry_ref)

    row = lax.broadcasted_iota(jnp.int32, (n, n), 0)
    col = lax.broadcasted_iota(jnp.int32, (n, n), 1)
    upper = jnp.where(col > row, 1.0, 0.0).astype(BF16)
    carry = carry_ref[...]
    for pp in reversed(range(pps)):
        x = lf_refs[pp][...]
        o_ref[pp] = _split_dot(upper, x) + carry
        carry = carry + jnp.sum(x, axis=0, keepdims=True)
    carry_ref[...] = carry


def page_suffix_sums(page_table_flat, lf_pool, b, n_pages):
    _, n, nh = lf_pool.shape
    pps = SUFFIX_PAGES_PER_STEP
    assert n_pages % pps == 0
    n_steps = n_pages // pps

    def page_spec(pp):
        return pl.BlockSpec((None, n, nh), lambda i, j, pt: (pt[i * n_pages + n_pages - (j + 1) * pps + pp], 0, 0))

    grid_spec = pltpu.PrefetchScalarGridSpec(
        num_scalar_prefetch=1,
        grid=(b, n_steps),
        in_specs=[page_spec(pp) for pp in range(pps)],
        out_specs=pl.BlockSpec((None, pps, n, nh), lambda i, j, pt: (i, n_steps - 1 - j, 0, 0)),
        scratch_shapes=[pltpu.VMEM((1, nh), F32)],
    )
    return pl.pallas_call(
        functools.partial(_page_suffix_kernel, pps=pps),
        grid_spec=grid_spec,
        out_shape=jax.ShapeDtypeStruct((b, n_pages, n, nh), F32),
        compiler_params=_params("parallel", "arbitrary"),
        name="page_suffix_sums",
    )(page_table_flat, *([lf_pool] * pps))


HEADS_PER_TILE = 8
FOX_PAGES_PER_STEP = 2
FOX_PROMPT_TQ = 512


def _tile_rows(ref, sub):
    rows, n_sub, dh = ref.shape
    return ref.reshape(rows * n_sub, dh)[pl.ds(sub, rows, stride=n_sub), :]


def _fox_prompt_block(q_ref, k_ref, v_ref, c_ref, ct_ref, o_ref, h, ii):
    tq = q_ref.shape[0]
    q = q_ref[...]
    lane = lax.broadcasted_iota(jnp.int32, c_ref.shape, 1)
    cq = jnp.sum(jnp.where(lane == h, c_ref[...], 0.0), axis=-1, keepdims=True)
    dn = (((1,), (1,)), ((), ()))

    def scores(j):
        kb = k_ref[j * tq:(j + 1) * tq, :].astype(BF16)
        return lax.dot_general(q, kb, dn, preferred_element_type=F32) + (cq - ct_ref[j:j + 1, :])

    def update(j, s, carry, masked):
        m, l, acc = carry
        if masked:
            row = lax.broadcasted_iota(jnp.int32, s.shape, 0)
            col = lax.broadcasted_iota(jnp.int32, s.shape, 1)
            s = jnp.where(col <= row, s, NEG)
        m_new = jnp.maximum(m, jnp.max(s, axis=-1, keepdims=True))
        alpha = jnp.exp(m - m_new)
        p = jnp.exp(s - m_new)
        l = l * alpha + jnp.sum(p, axis=-1, keepdims=True)
        vb = v_ref[j * tq:(j + 1) * tq, :].astype(BF16)
        acc = acc * alpha + jnp.dot(p.astype(BF16), vb, preferred_element_type=F32)
        return m_new, l, acc

    carry = (jnp.full((tq, 1), NEG, F32), jnp.zeros((tq, 1), F32), jnp.zeros((tq, HEAD_DIM), F32))
    s_next = scores(0)
    for j in range(ii + 1):
        s_cur = s_next
        if j < ii:
            s_next = scores(j + 1)
        carry = update(j, s_cur, carry, masked=(j == ii))
    _, l, acc = carry
    o_ref[...] = (acc / l).astype(o_ref.dtype)


def _fox_online_update(m_ref, l_ref, acc_ref, s, pv):
    m_old = m_ref[...]
    m_new = jnp.maximum(m_old, jnp.max(s, axis=-1, keepdims=True))
    alpha = jnp.exp(m_old - m_new)
    p = jnp.exp(s - m_new)
    l_ref[...] = l_ref[...] * alpha + jnp.sum(p, axis=-1, keepdims=True)
    acc_ref[...] = acc_ref[...] * alpha + pv(p)
    m_ref[...] = m_new


def _fox_sample_pages(q_ref, cn_ref, dt_ref, k_refs, v_refs, m_ref, l_ref, acc_ref, n_heads, pps):
    ng = n_heads // HEADS_PER_TILE
    t = q_ref.shape[1]
    dn = (((1,), (1,)), ((), ()))

    def head_rows(page_refs, h):
        g, hs = divmod(h, HEADS_PER_TILE)
        tiles = [_tile_rows(page_refs[pp * ng + g], hs).astype(BF16) for pp in range(pps)]
        return tiles[0] if pps == 1 else jnp.concatenate(tiles, axis=0)

    parts = []
    for h in range(n_heads):
        sh = lax.dot_general(q_ref[h], head_rows(k_refs, h), dn, preferred_element_type=F32)
        d_row = [dt_ref[pp, h:h + 1, :] for pp in range(pps)]
        parts.append(sh + (d_row[0] if pps == 1 else jnp.concatenate(d_row, axis=1)))
    cn = cn_ref[...]
    s = jnp.concatenate(parts, axis=0) + (cn if pps == 1 else jnp.concatenate([cn] * pps, axis=1))

    def pv_pages(p):
        return jnp.concatenate(
            [jnp.dot(p[h * t:(h + 1) * t].astype(BF16), head_rows(v_refs, h), preferred_element_type=F32)
             for h in range(n_heads)], axis=0)

    _fox_online_update(m_ref, l_ref, acc_ref, s, pv_pages)


def _fox_sample_new_rows(q_ref, cn_ref, cnt_ref, kn_ref, vn_ref, o_ref, m_ref, l_ref, acc_ref, n_heads):
    t = q_ref.shape[1]
    dn = (((1,), (1,)), ((), ()))
    parts = []
    for h in range(n_heads):
        kh = kn_ref[:, h * HEAD_DIM:(h + 1) * HEAD_DIM].astype(BF16)
        parts.append(lax.dot_general(q_ref[h], kh, dn, preferred_element_type=F32))
    s2 = jnp.concatenate(parts, axis=0) + cn_ref[:, :t] - cnt_ref[...]
    row = lax.broadcasted_iota(jnp.int32, s2.shape, 0)
    col = lax.broadcasted_iota(jnp.int32, s2.shape, 1)
    s2 = jnp.where(col <= lax.rem(row, t), s2, NEG)

    def pv_new(p):
        return jnp.concatenate(
            [jnp.dot(p[h * t:(h + 1) * t], vn_ref[:, h * HEAD_DIM:(h + 1) * HEAD_DIM], preferred_element_type=F32)
             for h in range(n_heads)], axis=0)

    _fox_online_update(m_ref, l_ref, acc_ref, s2, pv_new)
    o_ref[...] = acc_ref[...] / l_ref[...]


def _fox_kernel(pt_ref, qp_ref, kp_ref, vp_ref, c_ref, ct_ref, qs_ref, cn_ref, cnt_ref, dt_ref, kn_ref, vn_ref,
                *refs, nq, n_heads, n_steps, pps):
    nb = n_heads // HEADS_PER_TILE * pps
    k_refs, v_refs = refs[:nb], refs[nb:2 * nb]
    op_ref, os_ref, m_ref, l_ref, acc_ref = refs[2 * nb:]
    step = pl.program_id(0)
    i = lax.rem(step, nq)
    h = lax.rem(step // nq, n_heads)
    js = lax.rem(step, n_steps)

    @pl.when(js == 0)
    def _():
        m_ref[...] = jnp.full_like(m_ref, NEG)
        l_ref[...] = jnp.zeros_like(l_ref)
        acc_ref[...] = jnp.zeros_like(acc_ref)

    for ii in range(nq):

        @pl.when(i == ii)
        def _(ii=ii):
            _fox_sample_pages(qs_ref, cn_ref, dt_ref, k_refs, v_refs, m_ref, l_ref, acc_ref, n_heads, pps)
            _fox_prompt_block(qp_ref, kp_ref, vp_ref, c_ref, ct_ref, op_ref, h, ii)

    @pl.when(js == n_steps - 1)
    def _():
        _fox_sample_new_rows(qs_ref, cn_ref, cnt_ref, kn_ref, vn_ref, os_ref, m_ref, l_ref, acc_ref, n_heads)


def fox_attention(qp, kp, vp, cp, bp, sp, qs, c_new, d_past, k_new, v_new, k_pool, v_pool, page_table_flat,
                  bs, ts, n_heads, n_pages):
    tq = FOX_PROMPT_TQ
    nq = sp // tq
    ng = n_heads // HEADS_PER_TILE
    pps = FOX_PAGES_PER_STEP
    n_steps = n_pages // pps
    assert n_pages % pps == 0 and bp * n_heads * nq == bs * n_steps, "the two groups must have equally many steps"
    d = n_heads * HEAD_DIM
    ht = n_heads * ts
    n_pool = k_pool.shape[0]

    ct = cp.reshape(bp, sp, n_heads).transpose(0, 2, 1).reshape(bp * n_heads, nq, tq)
    q4 = qs.reshape(bs, ts, n_heads, HEAD_DIM).transpose(0, 2, 1, 3)
    c_ht = c_new.transpose(0, 2, 1).reshape(bs, ht, 1)
    cn = jnp.broadcast_to(c_ht, (bs, ht, PAGE_SIZE))
    cnt = jnp.broadcast_to(c_new.transpose(0, 2, 1)[:, :, None, :], (bs, n_heads, ts, ts)).reshape(bs, ht, ts)
    dt = d_past.transpose(0, 1, 3, 2)
    kpool = k_pool.reshape(n_pool, PAGE_SIZE, ng, HEADS_PER_TILE, HEAD_DIM)
    vpool = v_pool.reshape(n_pool, PAGE_SIZE, ng, HEADS_PER_TILE, HEAD_DIM)

    def p_row(s, pt):
        return (s // (n_heads * nq) * nq + lax.rem(s, nq), lax.rem(s // nq, n_heads))

    def p_head(s, pt):
        return (s // (n_heads * nq), lax.rem(s // nq, n_heads))

    def page_spec(pp, g):
        return pl.BlockSpec((None, PAGE_SIZE, None, HEADS_PER_TILE, HEAD_DIM),
                            lambda s, pt: (pt[s // n_steps * n_pages + lax.rem(s, n_steps) * pps + pp], 0, g, 0, 0))

    page_specs = [page_spec(pp, g) for pp in range(pps) for g in range(ng)]
    sb = lambda s, pt: s // n_steps
    grid_spec = pltpu.PrefetchScalarGridSpec(
        num_scalar_prefetch=1,
        grid=(bs * n_steps,),
        in_specs=[
            pl.BlockSpec((tq, HEAD_DIM), p_row),
            pl.BlockSpec((sp, HEAD_DIM), p_head),
            pl.BlockSpec((sp, HEAD_DIM), p_head),
            pl.BlockSpec((tq, n_heads), lambda s, pt: (p_row(s, pt)[0], 0)),
            pl.BlockSpec((None, nq, tq), lambda s, pt: (s // nq, 0, 0)),
            pl.BlockSpec((None, n_heads, ts, HEAD_DIM), lambda s, pt: (sb(s, pt), 0, 0, 0)),
            pl.BlockSpec((None, ht, PAGE_SIZE), lambda s, pt: (sb(s, pt), 0, 0)),
            pl.BlockSpec((None, ht, ts), lambda s, pt: (sb(s, pt), 0, 0)),
            pl.BlockSpec((None, pps, n_heads, PAGE_SIZE), lambda s, pt: (sb(s, pt), lax.rem(s, n_steps), 0, 0)),
            pl.BlockSpec((ts, d), lambda s, pt: (sb(s, pt), 0)),
            pl.BlockSpec((ts, d), lambda s, pt: (sb(s, pt), 0)),
        ] + page_specs * 2,
        out_specs=[pl.BlockSpec((tq, HEAD_DIM), p_row),
                   pl.BlockSpec((None, ht, HEAD_DIM), lambda s, pt: (sb(s, pt), 0, 0))],
        scratch_shapes=[pltpu.VMEM((ht, 1), F32), pltpu.VMEM((ht, 1), F32), pltpu.VMEM((ht, HEAD_DIM), F32)],
    )
    op, os_ = pl.pallas_call(
        functools.partial(_fox_kernel, nq=nq, n_heads=n_heads, n_steps=n_steps, pps=pps),
        grid_spec=grid_spec,
        out_shape=[jax.ShapeDtypeStruct(qp.shape, BF16), jax.ShapeDtypeStruct((bs, ht, HEAD_DIM), F32)],
        compiler_params=_params("arbitrary"),
        name="fox_attention",
    )(page_table_flat, qp, kp, vp, cp, ct, q4, cn, cnt, dt, k_new, v_new,
      *([kpool] * (ng * pps)), *([vpool] * (ng * pps)))
    return op, os_.reshape(bs, n_heads, ts, HEAD_DIM).transpose(0, 2, 1, 3).reshape(bs * ts, d)


def _t5_bucket(dist):
    n = np.asarray(dist, dtype=np.int64)
    exact = N_BUCKETS // 2
    large = exact + (np.log(np.maximum(n, 1) / exact) / np.log(BUCKET_MAX_DIST / exact) * (N_BUCKETS - exact)).astype(np.int64)
    return np.where(n < exact, n, np.minimum(large, N_BUCKETS - 1)).astype(np.int32)


def _band_bias(rel_bias, group_heads):
    a = np.arange(Q_BLOCK)[:, None]
    bcol = np.arange(2 * Q_BLOCK)[None, :]
    j = a - bcol + Q_BLOCK
    out = []
    for g, (window, dil) in enumerate(PATTERNS):
        assert window // dil == Q_BLOCK
        valid = (j >= 0) & (j <= window // dil)
        bucket = _t5_bucket(dil * np.clip(j, 0, window // dil))
        bias = _select_buckets(rel_bias[:, g * group_heads:(g + 1) * group_heads], bucket)
        out.append(jnp.where(valid[None], bias, NEG))
    return jnp.stack(out)


def _select_buckets(rb, bucket):
    onehot = (jnp.asarray(bucket.reshape(-1))[None, :] == jnp.arange(N_BUCKETS)[:, None]).astype(F32)
    sel = jnp.einsum("bh,bn->hn", rb.astype(F32), onehot, precision=lax.Precision.HIGHEST)
    return sel.reshape((rb.shape[1],) + bucket.shape)


DILATED_BLOCKS_PER_STEP = 4


def _dilated_prompt_kernel(*refs, s):
    n_q = N_GROUPS * 2
    q_refs = refs[:n_q]
    k_ref, v_ref, bias_ref, o_ref, m_ref, l_ref, acc_ref = refs[n_q:]
    qb_rows = Q_BLOCK
    dn = (((1,), (1,)), ((), ()))

    def blocks(g, specs, merge):
        bias = bias_ref[g].reshape(2 * qb_rows, 2 * qb_rows)
        scores = []
        for q_idx, k_idx, first in specs:
            q2 = jnp.concatenate([q_refs[2 * g + r][q_idx, :].astype(BF16) for r in range(2)], axis=0)
            sc = lax.dot_general(q2, k_ref[k_idx, :].astype(BF16), dn, preferred_element_type=F32)
            scores.append(sc + (bias[:, qb_rows:] if first else bias))
        stats = []
        for sc in scores:
            m_blk = jnp.max(sc, axis=-1, keepdims=True)
            p = jnp.exp(sc - m_blk)
            stats.append((m_blk, jnp.sum(p, axis=-1, keepdims=True), p.astype(BF16)))
        outs = [jnp.dot(p, v_ref[k_idx, :].astype(BF16), preferred_element_type=F32)
                for (_, _, p), (_, k_idx, _) in zip(stats, specs)]
        for (m_blk, l_blk, _), o_blk, (q_idx, _, _) in zip(stats, outs, specs):
            for r in range(2):
                rs = slice(r * qb_rows, (r + 1) * qb_rows)
                m_b = jnp.broadcast_to(m_blk[rs], (qb_rows, HEAD_DIM))
                l_b = jnp.broadcast_to(l_blk[rs], (qb_rows, HEAD_DIM))
                if merge:
                    m_old = m_ref[r, q_idx, :]
                    m_new = jnp.maximum(m_old, m_b)
                    a_old = jnp.exp(m_old - m_new)
                    a_blk = jnp.exp(m_b - m_new)
                    l_ref[r, q_idx, :] = l_ref[r, q_idx, :] * a_old + l_b * a_blk
                    acc_ref[r, q_idx, :] = acc_ref[r, q_idx, :] * a_old + o_blk[rs] * a_blk
                    m_ref[r, q_idx, :] = m_new
                else:
                    m_ref[r, q_idx, :] = m_b
                    l_ref[r, q_idx, :] = l_b
                    acc_ref[r, q_idx, :] = o_blk[rs]

    per_step = DILATED_BLOCKS_PER_STEP
    for g, (_, dil) in enumerate(PATTERNS):
        n_blocks = s // (dil * qb_rows)
        merge = g > 0

        def idx(start, size, dil=dil):
            return pl.ds(start, size) if dil == 1 else pl.ds(start, size, stride=dil)

        def spec(c, qb, dil=dil, idx=idx):
            q0 = c + dil * qb_rows * qb
            if isinstance(qb, int) and qb == 0:
                return idx(q0, qb_rows), idx(q0, qb_rows), True
            return idx(q0, qb_rows), idx(q0 - dil * qb_rows, 2 * qb_rows), False

        if n_blocks == 1:
            assert dil % per_step == 0

            def classes(it, carry, g=g, merge=merge, spec=spec):
                blocks(g, [spec(it * per_step + u, 0) for u in range(per_step)], merge)
                return carry

            lax.fori_loop(0, dil // per_step, classes, 0)
        elif n_blocks <= per_step:

            def one_class(c, carry, g=g, merge=merge, spec=spec, n_blocks=n_blocks):
                blocks(g, [spec(c, qb) for qb in range(n_blocks)], merge)
                return carry

            lax.fori_loop(0, dil, one_class, 0)
        else:
            assert dil == 1 and (n_blocks - 1) % (per_step - 1) == 0
            blocks(g, [spec(0, 0)], merge)
            grp = per_step - 1

            def later(it, carry, g=g, merge=merge, spec=spec, grp=grp):
                blocks(g, [spec(0, 1 + it * grp + u) for u in range(grp)], merge)
                return carry

            lax.fori_loop(0, (n_blocks - 1) // grp, later, 0)

    for r in range(2):
        o_ref[:, r * HEAD_DIM:(r + 1) * HEAD_DIM] = (acc_ref[r] / l_ref[r]).astype(o_ref.dtype)


def dilated_prompt_attention(q, k, v, rel_bias, b, s, n_kv, group_heads):
    assert group_heads == 2 * n_kv
    bias = _band_bias(rel_bias, group_heads).reshape(N_GROUPS, n_kv, 2, Q_BLOCK, 2 * Q_BLOCK)

    def q_spec(g, r):
        return pl.BlockSpec((s, HEAD_DIM), lambda bi, kv: (bi, g * group_heads + 2 * kv + r))

    return pl.pallas_call(
        functools.partial(_dilated_prompt_kernel, s=s),
        grid=(b, n_kv),
        in_specs=[q_spec(g, r) for g in range(N_GROUPS) for r in range(2)] + [
            pl.BlockSpec((s, HEAD_DIM), lambda bi, kv: (bi, kv)),
            pl.BlockSpec((s, HEAD_DIM), lambda bi, kv: (bi, kv)),
            pl.BlockSpec((N_GROUPS, None, 2, Q_BLOCK, 2 * Q_BLOCK), lambda bi, kv: (0, kv, 0, 0, 0)),
        ],
        out_specs=pl.BlockSpec((s, 2 * HEAD_DIM), lambda bi, kv: (bi, kv)),
        out_shape=jax.ShapeDtypeStruct((b * s, group_heads * HEAD_DIM), BF16),
        scratch_shapes=[pltpu.VMEM((2, s, HEAD_DIM), F32)] * 3,
        compiler_params=_params("parallel", "parallel"),
        name="dilated_prompt_attention",
    )(*([q] * (N_GROUPS * 2)), k, v, bias)


def _sample_bias(rel_bias, group_heads, n_kv, t, n_buf):
    dist = n_buf + np.arange(t)[:, None] - np.arange(n_buf + t)[None, :]
    out = []
    for g, (window, dil) in enumerate(PATTERNS):
        valid = (dist >= 0) & (dist % dil == 0) & (dist <= window)
        bucket = _t5_bucket(np.clip(dist, 0, window))
        bias = _select_buckets(rel_bias[:, g * group_heads:(g + 1) * group_heads], bucket)
        out.append(jnp.where(valid[None], bias, NEG).reshape(n_kv, 2, t, n_buf + t))
    return jnp.stack(out, axis=1).reshape(n_kv, N_GROUPS * 2 * t, n_buf + t)


def _dilated_sample_kernel(q_ref, kc_ref, vc_ref, kn_ref, vn_ref, bc_ref, bn_ref, o_ref, *, n_kv, group_heads):
    t = q_ref.shape[0]
    n_buf = kc_ref.shape[0]
    for kv in range(n_kv):
        heads = [g * group_heads + 2 * kv + r for g in range(N_GROUPS) for r in range(2)]
        q6 = jnp.concatenate([q_ref[:, h * HEAD_DIM:(h + 1) * HEAD_DIM] for h in heads], axis=0).astype(BF16)
        kc = _tile_rows(kc_ref, kv).astype(BF16)
        vc = _tile_rows(vc_ref, kv).astype(BF16)
        sl = slice(kv * HEAD_DIM, (kv + 1) * HEAD_DIM)
        kn = kn_ref[:, sl].astype(BF16)
        vn = vn_ref[:, sl]
        dn = (((1,), (1,)), ((), ()))
        s_c = lax.dot_general(q6, kc, dn, preferred_element_type=F32) + bc_ref[kv]
        s_n = lax.dot_general(q6, kn, dn, preferred_element_type=F32) + bn_ref[kv]
        m_row = jnp.maximum(jnp.max(s_c, axis=-1, keepdims=True), jnp.max(s_n, axis=-1, keepdims=True))
        m = jnp.max(m_row.reshape(N_GROUPS, 2 * t, 1), axis=0)
        m_all = jnp.concatenate([m] * N_GROUPS, axis=0)
        p_c = jnp.exp(s_c - m_all)
        p_n = jnp.exp(s_n - m_all)
        l_row = jnp.sum(p_c, axis=-1, keepdims=True) + jnp.sum(p_n, axis=-1, keepdims=True)
        o_row = jnp.dot(p_c.astype(BF16), vc, preferred_element_type=F32) + jnp.dot(p_n, vn, preferred_element_type=F32)
        l = jnp.sum(l_row.reshape(N_GROUPS, 2 * t, 1), axis=0)
        o = jnp.sum(o_row.reshape(N_GROUPS, 2 * t, HEAD_DIM), axis=0) / l
        for r in range(2):
            o_ref[:, (2 * kv + r) * HEAD_DIM:(2 * kv + r + 1) * HEAD_DIM] = o[r * t:(r + 1) * t]


def dilated_sample_attention(q, k_new, v_new, cache_k, cache_v, rel_bias, b, t, n_kv, group_heads):
    n_buf = cache_k.shape[1]
    bias = _sample_bias(rel_bias, group_heads, n_kv, t, n_buf)
    bias_c, bias_n = bias[:, :, :n_buf], bias[:, :, n_buf:]
    dq = q.shape[1]
    dkv = n_kv * HEAD_DIM
    return pl.pallas_call(
        functools.partial(_dilated_sample_kernel, n_kv=n_kv, group_heads=group_heads),
        grid=(b,),
        in_specs=[
            pl.BlockSpec((t, dq), lambda i: (i, 0)),
            pl.BlockSpec((None, n_buf, n_kv, HEAD_DIM), lambda i: (i, 0, 0, 0)),
            pl.BlockSpec((None, n_buf, n_kv, HEAD_DIM), lambda i: (i, 0, 0, 0)),
            pl.BlockSpec((t, dkv), lambda i: (i, 0)),
            pl.BlockSpec((t, dkv), lambda i: (i, 0)),
            pl.BlockSpec(bias_c.shape, lambda i: (0, 0, 0)),
            pl.BlockSpec(bias_n.shape, lambda i: (0, 0, 0)),
        ],
        out_specs=pl.BlockSpec((t, group_heads * HEAD_DIM), lambda i: (i, 0)),
        out_shape=jax.ShapeDtypeStruct((b * t, group_heads * HEAD_DIM), F32),
        compiler_params=_params("parallel"),
        name="dilated_sample_attention",
    )(q, cache_k, cache_v, k_new, v_new, bias_c, bias_n)


CONV_PAD_ROWS = 8


FFN_COL_TILE = 256
FFN_ROW_CHUNKS = 4


def _ffn_hidden_kernel(xp_ref, xs_ref, wu_ref, wg_ref, prev_ref, wc_ref, bc_ref, hp_ref, hs_ref, stp_ref, sts_ref,
                       pad_ref, *, n_seq_s, t_s):
    i = pl.program_id(0)
    t = xp_ref.shape[0]
    lo = CONV_PAD_ROWS - (CONV_W - 1)
    rc = t // FFN_ROW_CHUNKS
    wu = wu_ref[...].astype(BF16)
    wg = wg_ref[...].astype(BF16)

    def conv_gelu_gate(gate, start, rows):
        uc = bc_ref[...] + pad_ref[pl.ds(lo + start, rows), :] * wc_ref[0:1, :]
        for tap in range(1, CONV_W):
            uc = uc + pad_ref[pl.ds(lo + start + tap, rows), :] * wc_ref[tap:tap + 1, :]
        return 0.5 * uc * (1.0 + lax.erf(uc * SQRT_HALF)) * gate

    @pl.when(i == 0)
    def _():
        x = xs_ref[...]
        u = jnp.dot(x, wu, preferred_element_type=F32)
        gate = jnp.dot(x, wg, preferred_element_type=F32)
        pieces = []
        for sq in range(n_seq_s):
            rs = slice(sq * t_s, (sq + 1) * t_s)
            pad_ref[pl.ds(CONV_PAD_ROWS, t_s), :] = u[rs]
            pad_ref[pl.ds(lo, CONV_W - 1), :] = prev_ref[sq]
            pieces.append(conv_gelu_gate(gate[rs], 0, t_s))
            sts_ref[sq] = pad_ref[pl.ds(lo + t_s, CONV_W - 1), :]
        hs_ref[...] = jnp.concatenate(pieces, axis=0).astype(hs_ref.dtype)

    pad_ref[pl.ds(lo, CONV_W - 1), :] = jnp.zeros((CONV_W - 1, pad_ref.shape[1]), F32)
    gates = []
    for c in range(FFN_ROW_CHUNKS):
        x = xp_ref[c * rc:(c + 1) * rc, :]
        pad_ref[pl.ds(CONV_PAD_ROWS + c * rc, rc), :] = jnp.dot(x, wu, preferred_element_type=F32)
        gates.append(jnp.dot(x, wg, preferred_element_type=F32))
    for c in range(FFN_ROW_CHUNKS):
        hp_ref[c * rc:(c + 1) * rc, :] = conv_gelu_gate(gates[c], c * rc, rc).astype(hp_ref.dtype)
    stp_ref[...] = pad_ref[pl.ds(lo + t, CONV_W - 1), :]


def conv_ffn_hidden(xp, xs, w_up, w_gate, layer, prev_s, w_conv, b_conv, n_seq_p, t_p, n_seq_s, t_s):
    mp, d = xp.shape
    ms = xs.shape[0]
    f = w_up.shape[2]
    tn = FFN_COL_TILE
    assert mp == n_seq_p * t_p and ms == n_seq_s * t_s and f % tn == 0 and t_p % (FFN_ROW_CHUNKS * 128) == 0
    last = f // tn - 1
    wspec = pl.BlockSpec((None, d, tn), lambda i, j: (layer, 0, j))
    s_col = lambda i, j: jnp.where(i == 0, j, last)
    return pl.pallas_call(
        functools.partial(_ffn_hidden_kernel, n_seq_s=n_seq_s, t_s=t_s),
        grid=(n_seq_p, f // tn),
        in_specs=[
            pl.BlockSpec((t_p, d), lambda i, j: (i, 0), pipeline_mode=pl.Buffered(1)),
            pl.BlockSpec((ms, d), lambda i, j: (0, 0)),
            wspec, wspec,
            pl.BlockSpec((None, n_seq_s, CONV_W - 1, tn), lambda i, j: (layer, 0, 0, s_col(i, j))),
            pl.BlockSpec((None, CONV_W, tn), lambda i, j: (layer, 0, j)),
            pl.BlockSpec((None, 1, tn), lambda i, j: (layer, 0, j)),
        ],
        out_specs=[
            pl.BlockSpec((t_p, tn), lambda i, j: (i, j)),
            pl.BlockSpec((ms, tn), lambda i, j: (0, s_col(i, j))),
            pl.BlockSpec((None, CONV_W - 1, tn), lambda i, j: (i, 0, j)),
            pl.BlockSpec((n_seq_s, CONV_W - 1, tn), lambda i, j: (0, 0, s_col(i, j))),
        ],
        out_shape=[jax.ShapeDtypeStruct((mp, f), BF16), jax.ShapeDtypeStruct((ms, f), BF16),
                   jax.ShapeDtypeStruct((n_seq_p, CONV_W - 1, f), F32),
                   jax.ShapeDtypeStruct((n_seq_s, CONV_W - 1, f), F32)],
        scratch_shapes=[pltpu.VMEM((t_p + CONV_PAD_ROWS, tn), F32)],
        compiler_params=_params("arbitrary", "arbitrary"),
        name="conv_ffn_hidden",
    )(xp, xs, w_up, w_gate, prev_s, w_conv, b_conv.reshape(b_conv.shape[0], 1, f))


def _shift_append_kernel(old_ref, new_ref, o_ref):
    n_keep = old_ref.shape[1]
    o_ref[:n_keep] = old_ref[0]
    o_ref[n_keep:] = new_ref[...]


def shift_append(cache, new):
    b, n_buf, kv, dh = cache.shape
    t = new.shape[1]
    old_block = (pl.Element(1), pl.Element(n_buf - t), pl.Element(kv), pl.Element(dh))
    return pl.pallas_call(
        _shift_append_kernel,
        grid=(b,),
        in_specs=[pl.BlockSpec(old_block, lambda i: (i, t, 0, 0)),
                  pl.BlockSpec((None, t, kv, dh), lambda i: (i, 0, 0, 0))],
        out_specs=pl.BlockSpec((None, n_buf, kv, dh), lambda i: (i, 0, 0, 0)),
        out_shape=jax.ShapeDtypeStruct(cache.shape, cache.dtype),
        compiler_params=_params("parallel"),
        name="shift_append",
    )(cache, new)


def kernel(x_prompt, x_sample, cache_fox_k, cache_fox_v, cache_fox_logf, cache_win_k, cache_win_v, state_ffn_conv,
           page_table, w_in_a, b_f_a, w_o_a, g_attn, g_kv, w_kv_b, w_q_b, w_o_b, rel_bias,
           g_ffn, w_up, w_gate, w_conv, b_conv, w_down, g_final):
    bp, sp, d_model = x_prompt.shape
    bs, ts, _ = x_sample.shape
    n_heads = b_f_a.shape[1]
    hd = n_heads * HEAD_DIM
    n_kv = cache_win_k.shape[2]
    kvw = n_kv * HEAD_DIM
    group_heads = w_q_b.shape[2] // (N_GROUPS * HEAD_DIM)
    d_ff = w_up.shape[2]
    n_pages = page_table.shape[1]
    assert w_in_a.shape[0] == 1 and w_q_b.shape[0] == 1 and g_attn.shape[0] == 2
    mp, ms = bp * sp, bs * ts

    b_f = b_f_a[0][None].astype(F32)
    w_in_t = jnp.swapaxes(w_in_a, 1, 2)
    w_down_b = w_down.astype(BF16)
    xp = x_prompt.reshape(mp, d_model)
    xs = x_sample.reshape(ms, d_model)

    def norm(x, gains, dtype=BF16):
        return rmsnorm(x, gains, dtype)

    def ffn(xp, xs, layer):
        (xnp,) = norm(xp, g_ffn[layer][None])
        (xns,) = norm(xs, g_ffn[layer][None])
        hp, hs, stp, sts = conv_ffn_hidden(xnp, xns, w_up, w_gate, layer, state_ffn_conv, w_conv, b_conv,
                                           bp, sp, bs, ts)
        xp = matmul(hp, w_down_b, layer, res=xp, bm=512, bn=512)
        xs = matmul(hs, w_down_b, layer, res=xs, bm=ms, bn=1024, bk=d_ff // 2)
        return xp, xs, stp, sts

    (xnp,) = norm(xp, g_attn[0][None])
    (xns,) = norm(xs, g_attn[0][None])
    qp, qs = matmul_pair(xnp, xns, w_in_t, 0, 0, hd, out_dtype=BF16, out_scale=ATTN_SCALE, transposed=True)
    kp, ks = matmul_pair(xnp, xns, w_in_t, 0, hd, hd, transposed=True)
    vp, vs = matmul_pair(xnp, xns, w_in_t, 0, 2 * hd, hd, transposed=True)
    lfp = forget_gate(xnp, w_in_t, 0, 3 * hd, b_f)
    lfs = forget_gate(xns, w_in_t, 0, 3 * hd, b_f)
    cp = cumsum_rows(lfp.reshape(bp, sp, n_heads))
    cs = cumsum_rows(lfs.reshape(bs, ts, n_heads))
    pt_flat = page_table.reshape(-1)
    d_past = page_suffix_sums(pt_flat, cache_fox_logf[0], bs, n_pages)
    ap, a_s = fox_attention(qp, kp, vp, cp.reshape(mp, n_heads), bp, sp, qs, cs, d_past, ks, vs, cache_fox_k[0],
                            cache_fox_v[0], pt_flat, bs, ts, n_heads, n_pages)
    a_s = a_s.astype(BF16)
    xp, xs = matmul_pair(ap, a_s, w_o_a, 0, 0, d_model, res=(xp, xs))
    xp, xs, st0p, st0s = ffn(xp, xs, 0)

    xnp, xkvp = norm(xp, jnp.stack([g_attn[1], g_kv]))
    xns, xkvs = norm(xs, jnp.stack([g_attn[1], g_kv]))
    kvp, kvs = matmul_pair(xkvp, xkvs, w_kv_b[None], 0, 0, 2 * kvw)
    qp, qs = matmul_pair(xnp, xns, w_q_b, 0, 0, w_q_b.shape[2], out_scale=ATTN_SCALE)
    kbp, vbp = kvp[:, :kvw], kvp[:, kvw:]
    kbs, vbs = kvs[:, :kvw], kvs[:, kvw:]
    ap = dilated_prompt_attention(qp, kbp, vbp, rel_bias, bp, sp, n_kv, group_heads)
    a_s = dilated_sample_attention(qs, kbs, vbs, cache_win_k, cache_win_v, rel_bias, bs, ts, n_kv,
                                   group_heads).astype(BF16)
    xp, xs = matmul_pair(ap, a_s, w_o_b, 0, 0, d_model, res=(xp, xs))
    xp, xs, st1p, st1s = ffn(xp, xs, 1)
    (yp,) = norm(xp, g_final[None], F32)
    (ys,) = norm(xs, g_final[None], F32)

    n_pg = sp // PAGE_SIZE
    keep = min(BUCKET_MAX_DIST, sp)
    return (
        yp.reshape(bp, sp, d_model),
        ys.reshape(bs, ts, d_model),
        kp.reshape(1, bp, n_pg, PAGE_SIZE, n_heads, HEAD_DIM),
        vp.reshape(1, bp, n_pg, PAGE_SIZE, n_heads, HEAD_DIM),
        lfp.reshape(1, bp, n_pg, PAGE_SIZE, n_heads),
        ks.reshape(1, bs, ts, n_heads, HEAD_DIM),
        vs.reshape(1, bs, ts, n_heads, HEAD_DIM),
        lfs.reshape(1, bs, ts, n_heads),
        kbp.reshape(bp, sp, n_kv, HEAD_DIM)[:, sp - keep:],
        vbp.reshape(bp, sp, n_kv, HEAD_DIM)[:, sp - keep:],
        shift_append(cache_win_k, kbs.reshape(bs, ts, n_kv, HEAD_DIM)),
        shift_append(cache_win_v, vbs.reshape(bs, ts, n_kv, HEAD_DIM)),
        jnp.stack([st0p, st1p]),
        jnp.stack([st0s, st1s]),
    )
```

```python
import functools
import math

import numpy as np
import jax
import jax.numpy as jnp
from jax import lax
from jax.experimental import pallas as pl
from jax.experimental.pallas import tpu as pltpu

F32 = jnp.float32
BF16 = jnp.bfloat16

HEAD_DIM = 128
PAGE_SIZE = 128
PATTERNS = ((128, 1), (512, 4), (2048, 16))
N_GROUPS = len(PATTERNS)
N_BUCKETS = 32
BUCKET_MAX_DIST = max(w for w, _ in PATTERNS)
CONV_W = 3
Q_BLOCK = 128
NORM_EPS = 1e-6
ATTN_SCALE = HEAD_DIM ** -0.5
NEG = -1e30
SQRT_HALF = 0.7071067811865476

V7X_VMEM_LIMIT_BYTES = 56 * 1024 * 1024


def _params(*semantics):
    return pltpu.CompilerParams(dimension_semantics=semantics, vmem_limit_bytes=V7X_VMEM_LIMIT_BYTES)


def _rmsnorm_kernel(x_ref, g_ref, *o_refs):
    x = x_ref[...]
    y = x * lax.rsqrt(jnp.mean(x * x, axis=-1, keepdims=True) + NORM_EPS)
    for j, o_ref in enumerate(o_refs):
        o_ref[...] = (y * g_ref[j:j + 1, :]).astype(o_ref.dtype)


def rmsnorm(x, gains, out_dtype):
    m, d = x.shape
    n = gains.shape[0]
    tm = min(256, m)
    return pl.pallas_call(
        _rmsnorm_kernel,
        grid=(m // tm,),
        in_specs=[pl.BlockSpec((tm, d), lambda i: (i, 0)), pl.BlockSpec((n, d), lambda i: (0, 0))],
        out_specs=[pl.BlockSpec((tm, d), lambda i: (i, 0))] * n,
        out_shape=[jax.ShapeDtypeStruct((m, d), out_dtype)] * n,
        compiler_params=_params("parallel"),
        name="rmsnorm",
    )(x, gains)


def _matmul_kernel(*refs, has_res, nk, out_scale):
    x_ref, w_ref = refs[0], refs[1]
    res_ref = refs[2] if has_res else None
    o_ref = refs[2 + has_res]

    def finish(acc):
        if out_scale != 1.0:
            acc = acc * out_scale
        if has_res:
            acc = acc + res_ref[...]
        o_ref[...] = acc.astype(o_ref.dtype)

    part = jnp.dot(x_ref[...], w_ref[...], preferred_element_type=F32)
    if nk == 1:
        finish(part)
        return
    acc_ref = refs[3 + has_res]
    k = pl.program_id(2)

    @pl.when(k == 0)
    def _():
        acc_ref[...] = part

    @pl.when(jnp.logical_and(k > 0, k < nk - 1))
    def _():
        acc_ref[...] += part

    @pl.when(k == nk - 1)
    def _():
        finish(acc_ref[...] + part)


def matmul(x, w, layer, *, res=None, out_dtype=F32, out_scale=1.0, bm=1024, bn=1024, bk=None):
    m, kdim = x.shape
    n = w.shape[2]
    bm, bn = min(bm, m), min(bn, n)
    bk = kdim if bk is None else bk
    nk = kdim // bk
    assert m % bm == 0 and n % bn == 0 and kdim % bk == 0 and nk >= 1
    in_specs = [pl.BlockSpec((bm, bk), lambda i, j, k: (i, k)),
                pl.BlockSpec((None, bk, bn), lambda i, j, k: (layer, k, j))]
    args = [x, w]
    if res is not None:
        in_specs.append(pl.BlockSpec((bm, bn), lambda i, j, k: (i, j)))
        args.append(res)
    return pl.pallas_call(
        functools.partial(_matmul_kernel, has_res=res is not None, nk=nk, out_scale=out_scale),
        grid=(m // bm, n // bn, nk),
        in_specs=in_specs,
        out_specs=pl.BlockSpec((bm, bn), lambda i, j, k: (i, j)),
        out_shape=jax.ShapeDtypeStruct((m, n), out_dtype),
        scratch_shapes=[pltpu.VMEM((bm, bn), F32)] if nk > 1 else [],
        compiler_params=_params("parallel", "parallel", "arbitrary"),
        name="matmul",
    )(*args)


def _pair_matmul_kernel(*refs, nm, has_res, out_scale, transposed):
    if has_res:
        xp_ref, xs_ref, w_ref, rp_ref, rs_ref, op_ref, os_ref, wb_ref = refs
    else:
        xp_ref, xs_ref, w_ref, op_ref, os_ref, wb_ref = refs
        rp_ref = rs_ref = None
    i = pl.program_id(1)

    @pl.when(i == 0)
    def _():
        wb_ref[...] = w_ref[...].astype(BF16)

    def emit(x_ref, r_ref, o_ref):
        dims = (((1,), (1 if transposed else 0,)), ((), ()))
        acc = lax.dot_general(x_ref[...], wb_ref[...], dims, preferred_element_type=F32)
        if out_scale != 1.0:
            acc = acc * out_scale
        if r_ref is not None:
            acc = acc + r_ref[...]
        o_ref[...] = acc.astype(o_ref.dtype)

    @pl.when(i < nm)
    def _():
        emit(xp_ref, rp_ref, op_ref)

    @pl.when(i == nm)
    def _():
        emit(xs_ref, rs_ref, os_ref)


def matmul_pair(xp, xs, w, layer, col0, n, *, res=None, out_dtype=F32, out_scale=1.0, transposed=False,
                bm=256, bn=1024):
    mp, kdim = xp.shape
    ms = xs.shape[0]
    assert mp % bm == 0 and n % bn == 0 and col0 % bn == 0 and w.shape[2 if transposed else 1] == kdim
    nm, cb0, last = mp // bm, col0 // bn, mp // bm - 1
    row = lambda j, i: (jnp.minimum(i, last), 0)
    tile = lambda j, i: (jnp.minimum(i, last), j)
    if transposed:
        w_spec = pl.BlockSpec((None, bn, kdim), lambda j, i: (layer, cb0 + j, 0))
    else:
        w_spec = pl.BlockSpec((None, kdim, bn), lambda j, i: (layer, 0, cb0 + j))
    in_specs = [pl.BlockSpec((bm, kdim), row), pl.BlockSpec((ms, kdim), lambda j, i: (0, 0)), w_spec]
    args = [xp, xs, w]
    if res is not None:
        in_specs += [pl.BlockSpec((bm, bn), tile), pl.BlockSpec((ms, bn), lambda j, i: (0, j))]
        args += list(res)
    return pl.pallas_call(
        functools.partial(_pair_matmul_kernel, nm=nm, has_res=res is not None, out_scale=out_scale,
                          transposed=transposed),
        grid=(n // bn, nm + 1),
        in_specs=in_specs,
        out_specs=[pl.BlockSpec((bm, bn), tile), pl.BlockSpec((ms, bn), lambda j, i: (0, j))],
        out_shape=[jax.ShapeDtypeStruct((mp, n), out_dtype), jax.ShapeDtypeStruct((ms, n), out_dtype)],
        scratch_shapes=[pltpu.VMEM((bn, kdim) if transposed else (kdim, bn), BF16)],
        compiler_params=_params("parallel", "arbitrary"),
        name="matmul_pair",
    )(*args)


def _gate_kernel(x_ref, w_ref, b_ref, o_ref):
    z = lax.dot_general(x_ref[...], w_ref[...].astype(BF16), (((1,), (1,)), ((), ())),
                        preferred_element_type=F32) + b_ref[...]
    o_ref[...] = -(jnp.maximum(-z, 0.0) + jnp.log1p(jnp.exp(-jnp.abs(z))))


def forget_gate(xn, w_t, layer, row0, b_f):
    m, d = xn.shape
    nh = b_f.shape[1]
    tm = min(512, m)
    assert row0 % nh == 0 and nh % 8 == 0
    return pl.pallas_call(
        _gate_kernel,
        grid=(m // tm,),
        in_specs=[pl.BlockSpec((tm, d), lambda i: (i, 0)),
                  pl.BlockSpec((None, nh, d), lambda i: (layer, row0 // nh, 0)),
                  pl.BlockSpec((1, nh), lambda i: (0, 0))],
        out_specs=pl.BlockSpec((tm, nh), lambda i: (i, 0)),
        out_shape=jax.ShapeDtypeStruct((m, nh), F32),
        compiler_params=_params("parallel"),
        name="forget_gate",
    )(xn, w_t, b_f)


def _split_dot(tri, x):
    hi = x.astype(BF16)
    r1 = x - hi.astype(F32)
    mid = r1.astype(BF16)
    lo = (r1 - mid.astype(F32)).astype(BF16)
    out = jnp.dot(tri, lo, preferred_element_type=F32)
    out = out + jnp.dot(tri, mid, preferred_element_type=F32)
    return out + jnp.dot(tri, hi, preferred_element_type=F32)


def _cumsum_kernel(x_ref, o_ref, *, chunk):
    s = x_ref.shape[0]
    row = lax.broadcasted_iota(jnp.int32, (chunk, chunk), 0)
    col = lax.broadcasted_iota(jnp.int32, (chunk, chunk), 1)
    tri = jnp.where(col <= row, 1.0, 0.0).astype(BF16)
    carry = jnp.zeros((1, x_ref.shape[1]), F32)
    for i in range(s // chunk):
        y = _split_dot(tri, x_ref[i * chunk:(i + 1) * chunk, :]) + carry
        o_ref[i * chunk:(i + 1) * chunk, :] = y
        carry = y[chunk - 1:chunk, :]


def _cumsum_small_kernel(x_ref, o_ref):
    x = x_ref[...]
    t = x.shape[0]
    row = lax.broadcasted_iota(jnp.int32, x.shape, 0)
    for i in range(t):
        o_ref[i:i + 1, :] = jnp.sum(jnp.where(row <= i, x, 0.0), axis=0, keepdims=True)


def cumsum_rows(x):
    b, s, nh = x.shape
    body = _cumsum_small_kernel if s < 128 else functools.partial(_cumsum_kernel, chunk=min(256, s))
    return pl.pallas_call(
        body,
        grid=(b,),
        in_specs=[pl.BlockSpec((None, s, nh), lambda i: (i, 0, 0))],
        out_specs=pl.BlockSpec((None, s, nh), lambda i: (i, 0, 0)),
        out_shape=jax.ShapeDtypeStruct((b, s, nh), F32),
        compiler_params=_params("parallel"),
        name="cumsum_rows",
    )(x)


SUFFIX_PAGES_PER_STEP = 16


def _page_suffix_kernel(pt_ref, *refs, pps):
    lf_refs, o_ref, carry_ref = refs[:pps], refs[pps], refs[pps + 1]
    j = pl.program_id(1)
    n = lf_refs[0].shape[0]

    @pl.when(j == 0)
    def _():
        carry_ref[...] = jnp.zeros_like(carry_ref)

    row = lax.broadcasted_iota(jnp.int32, (n, n), 0)
    col = lax.broadcasted_iota(jnp.int32, (n, n), 1)
    upper = jnp.where(col > row, 1.0, 0.0).astype(BF16)
    carry = carry_ref[...]
    for pp in reversed(range(pps)):
        x = lf_refs[pp][...]
        o_ref[pp] = _split_dot(upper, x) + carry
        carry = carry + jnp.sum(x, axis=0, keepdims=True)
    carry_ref[...] = carry


def page_suffix_sums(page_table_flat, lf_pool, b, n_pages):
    _, n, nh = lf_pool.shape
    pps = SUFFIX_PAGES_PER_STEP
    assert n_pages % pps == 0
    n_steps = n_pages // pps

    def page_spec(pp):
        return pl.BlockSpec((None, n, nh), lambda i, j, pt: (pt[i * n_pages + n_pages - (j + 1) * pps + pp], 0, 0))

    grid_spec = pltpu.PrefetchScalarGridSpec(
        num_scalar_prefetch=1,
        grid=(b, n_steps),
        in_specs=[page_spec(pp) for pp in range(pps)],
        out_specs=pl.BlockSpec((None, pps, n, nh), lambda i, j, pt: (i, n_steps - 1 - j, 0, 0)),
        scratch_shapes=[pltpu.VMEM((1, nh), F32)],
    )
    return pl.pallas_call(
        functools.partial(_page_suffix_kernel, pps=pps),
        grid_spec=grid_spec,
        out_shape=jax.ShapeDtypeStruct((b, n_pages, n, nh), F32),
        compiler_params=_params("parallel", "arbitrary"),
        name="page_suffix_sums",
    )(page_table_flat, *([lf_pool] * pps))


HEADS_PER_TILE = 8
FOX_PAGES_PER_STEP = 2
FOX_PROMPT_TQ = 512


def _tile_rows(ref, sub):
    rows, n_sub, dh = ref.shape
    return ref.reshape(rows * n_sub, dh)[pl.ds(sub, rows, stride=n_sub), :]


def _fox_prompt_block(q_ref, k_ref, v_ref, c_ref, ct_ref, o_ref, h, ii):
    tq = q_ref.shape[0]
    q = q_ref[...]
    lane = lax.broadcasted_iota(jnp.int32, c_ref.shape, 1)
    cq = jnp.sum(jnp.where(lane == h, c_ref[...], 0.0), axis=-1, keepdims=True)
    dn = (((1,), (1,)), ((), ()))

    def scores(j):
        kb = k_ref[j * tq:(j + 1) * tq, :].astype(BF16)
        return lax.dot_general(q, kb, dn, preferred_element_type=F32) + (cq - ct_ref[j:j + 1, :])

    def update(j, s, carry, masked):
        m, l, acc = carry
        if masked:
            row = lax.broadcasted_iota(jnp.int32, s.shape, 0)
            col = lax.broadcasted_iota(jnp.int32, s.shape, 1)
            s = jnp.where(col <= row, s, NEG)
        m_new = jnp.maximum(m, jnp.max(s, axis=-1, keepdims=True))
        alpha = jnp.exp(m - m_new)
        p = jnp.exp(s - m_new)
        l = l * alpha + jnp.sum(p, axis=-1, keepdims=True)
        vb = v_ref[j * tq:(j + 1) * tq, :].astype(BF16)
        acc = acc * alpha + jnp.dot(p.astype(BF16), vb, preferred_element_type=F32)
        return m_new, l, acc

    carry = (jnp.full((tq, 1), NEG, F32), jnp.zeros((tq, 1), F32), jnp.zeros((tq, HEAD_DIM), F32))
    s_next = scores(0)
    for j in range(ii + 1):
        s_cur = s_next
        if j < ii:
            s_next = scores(j + 1)
        carry = update(j, s_cur, carry, masked=(j == ii))
    _, l, acc = carry
    o_ref[...] = (acc / l).astype(o_ref.dtype)


def _fox_online_update(m_ref, l_ref, acc_ref, s, pv):
    m_old = m_ref[...]
    m_new = jnp.maximum(m_old, jnp.max(s, axis=-1, keepdims=True))
    alpha = jnp.exp(m_old - m_new)
    p = jnp.exp(s - m_new)
    l_ref[...] = l_ref[...] * alpha + jnp.sum(p, axis=-1, keepdims=True)
    acc_ref[...] = acc_ref[...] * alpha + pv(p)
    m_ref[...] = m_new


def _fox_sample_pages(q_ref, cn_ref, dt_ref, k_refs, v_refs, m_ref, l_ref, acc_ref, n_heads, pps):
    ng = n_heads // HEADS_PER_TILE
    t = q_ref.shape[1]
    dn = (((1,), (1,)), ((), ()))

    def head_rows(page_refs, h):
        g, hs = divmod(h, HEADS_PER_TILE)
        tiles = [_tile_rows(page_refs[pp * ng + g], hs).astype(BF16) for pp in range(pps)]
        return tiles[0] if pps == 1 else jnp.concatenate(tiles, axis=0)

    parts = []
    for h in range(n_heads):
        sh = lax.dot_general(q_ref[h], head_rows(k_refs, h), dn, preferred_element_type=F32)
        d_row = [dt_ref[pp, h:h + 1, :] for pp in range(pps)]
        parts.append(sh + (d_row[0] if pps == 1 else jnp.concatenate(d_row, axis=1)))
    cn = cn_ref[...]
    s = jnp.concatenate(parts, axis=0) + (cn if pps == 1 else jnp.concatenate([cn] * pps, axis=1))

    def pv_pages(p):
        return jnp.concatenate(
            [jnp.dot(p[h * t:(h + 1) * t].astype(BF16), head_rows(v_refs, h), preferred_element_type=F32)
             for h in range(n_heads)], axis=0)

    _fox_online_update(m_ref, l_ref, acc_ref, s, pv_pages)


def _fox_sample_new_rows(q_ref, cn_ref, cnt_ref, kn_ref, vn_ref, o_ref, m_ref, l_ref, acc_ref, n_heads):
    t = q_ref.shape[1]
    dn = (((1,), (1,)), ((), ()))
    parts = []
    for h in range(n_heads):
        kh = kn_ref[:, h * HEAD_DIM:(h + 1) * HEAD_DIM].astype(BF16)
        parts.append(lax.dot_general(q_ref[h], kh, dn, preferred_element_type=F32))
    s2 = jnp.concatenate(parts, axis=0) + cn_ref[:, :t] - cnt_ref[...]
    row = lax.broadcasted_iota(jnp.int32, s2.shape, 0)
    col = lax.broadcasted_iota(jnp.int32, s2.shape, 1)
    s2 = jnp.where(col <= lax.rem(row, t), s2, NEG)

    def pv_new(p):
        return jnp.concatenate(
            [jnp.dot(p[h * t:(h + 1) * t], vn_ref[:, h * HEAD_DIM:(h + 1) * HEAD_DIM], preferred_element_type=F32)
             for h in range(n_heads)], axis=0)

    _fox_online_update(m_ref, l_ref, acc_ref, s2, pv_new)
    o_ref[...] = acc_ref[...] / l_ref[...]


def _fox_kernel(pt_ref, qp_ref, kp_ref, vp_ref, c_ref, ct_ref, qs_ref, cn_ref, cnt_ref, dt_ref, kn_ref, vn_ref,
                *refs, nq, n_heads, n_steps, pps):
    nb = n_heads // HEADS_PER_TILE * pps
    k_refs, v_refs = refs[:nb], refs[nb:2 * nb]
    op_ref, os_ref, m_ref, l_ref, acc_ref = refs[2 * nb:]
    step = pl.program_id(0)
    i = lax.rem(step, nq)
    h = lax.rem(step // nq, n_heads)
    js = lax.rem(step, n_steps)

    @pl.when(js == 0)
    def _():
        m_ref[...] = jnp.full_like(m_ref, NEG)
        l_ref[...] = jnp.zeros_like(l_ref)
        acc_ref[...] = jnp.zeros_like(acc_ref)

    for ii in range(nq):

        @pl.when(i == ii)
        def _(ii=ii):
            _fox_sample_pages(qs_ref, cn_ref, dt_ref, k_refs, v_refs, m_ref, l_ref, acc_ref, n_heads, pps)
            _fox_prompt_block(qp_ref, kp_ref, vp_ref, c_ref, ct_ref, op_ref, h, ii)

    @pl.when(js == n_steps - 1)
    def _():
        _fox_sample_new_rows(qs_ref, cn_ref, cnt_ref, kn_ref, vn_ref, os_ref, m_ref, l_ref, acc_ref, n_heads)


def fox_attention(qp, kp, vp, cp, bp, sp, qs, c_new, d_past, k_new, v_new, k_pool, v_pool, page_table_flat,
                  bs, ts, n_heads, n_pages):
    tq = FOX_PROMPT_TQ
    nq = sp // tq
    ng = n_heads // HEADS_PER_TILE
    pps = FOX_PAGES_PER_STEP
    n_steps = n_pages // pps
    assert n_pages % pps == 0 and bp * n_heads * nq == bs * n_steps, "the two groups must have equally many steps"
    d = n_heads * HEAD_DIM
    ht = n_heads * ts
    n_pool = k_pool.shape[0]

    ct = cp.reshape(bp, sp, n_heads).transpose(0, 2, 1).reshape(bp * n_heads, nq, tq)
    q4 = qs.reshape(bs, ts, n_heads, HEAD_DIM).transpose(0, 2, 1, 3)
    c_ht = c_new.transpose(0, 2, 1).reshape(bs, ht, 1)
    cn = jnp.broadcast_to(c_ht, (bs, ht, PAGE_SIZE))
    cnt = jnp.broadcast_to(c_new.transpose(0, 2, 1)[:, :, None, :], (bs, n_heads, ts, ts)).reshape(bs, ht, ts)
    dt = d_past.transpose(0, 1, 3, 2)
    kpool = k_pool.reshape(n_pool, PAGE_SIZE, ng, HEADS_PER_TILE, HEAD_DIM)
    vpool = v_pool.reshape(n_pool, PAGE_SIZE, ng, HEADS_PER_TILE, HEAD_DIM)

    def p_row(s, pt):
        return (s // (n_heads * nq) * nq + lax.rem(s, nq), lax.rem(s // nq, n_heads))

    def p_head(s, pt):
        return (s // (n_heads * nq), lax.rem(s // nq, n_heads))

    def page_spec(pp, g):
        return pl.BlockSpec((None, PAGE_SIZE, None, HEADS_PER_TILE, HEAD_DIM),
                            lambda s, pt: (pt[s // n_steps * n_pages + lax.rem(s, n_steps) * pps + pp], 0, g, 0, 0))

    page_specs = [page_spec(pp, g) for pp in range(pps) for g in range(ng)]
    sb = lambda s, pt: s // n_steps
    grid_spec = pltpu.PrefetchScalarGridSpec(
        num_scalar_prefetch=1,
        grid=(bs * n_steps,),
        in_specs=[
            pl.BlockSpec((tq, HEAD_DIM), p_row),
            pl.BlockSpec((sp, HEAD_DIM), p_head),
            pl.BlockSpec((sp, HEAD_DIM), p_head),
            pl.BlockSpec((tq, n_heads), lambda s, pt: (p_row(s, pt)[0], 0)),
            pl.BlockSpec((None, nq, tq), lambda s, pt: (s // nq, 0, 0)),
            pl.BlockSpec((None, n_heads, ts, HEAD_DIM), lambda s, pt: (sb(s, pt), 0, 0, 0)),
            pl.BlockSpec((None, ht, PAGE_SIZE), lambda s, pt: (sb(s, pt), 0, 0)),
            pl.BlockSpec((None, ht, ts), lambda s, pt: (sb(s, pt), 0, 0)),
            pl.BlockSpec((None, pps, n_heads, PAGE_SIZE), lambda s, pt: (sb(s, pt), lax.rem(s, n_steps), 0, 0)),
            pl.BlockSpec((ts, d), lambda s, pt: (sb(s, pt), 0)),
            pl.BlockSpec((ts, d), lambda s, pt: (sb(s, pt), 0)),
        ] + page_specs * 2,
        out_specs=[pl.BlockSpec((tq, HEAD_DIM), p_row),
                   pl.BlockSpec((None, ht, HEAD_DIM), lambda s, pt: (sb(s, pt), 0, 0))],
        scratch_shapes=[pltpu.VMEM((ht, 1), F32), pltpu.VMEM((ht, 1), F32), pltpu.VMEM((ht, HEAD_DIM), F32)],
    )
    op, os_ = pl.pallas_call(
        functools.partial(_fox_kernel, nq=nq, n_heads=n_heads, n_steps=n_steps, pps=pps),
        grid_spec=grid_spec,
        out_shape=[jax.ShapeDtypeStruct(qp.shape, BF16), jax.ShapeDtypeStruct((bs, ht, HEAD_DIM), F32)],
        compiler_params=_params("arbitrary"),
        name="fox_attention",
    )(page_table_flat, qp, kp, vp, cp, ct, q4, cn, cnt, dt, k_new, v_new,
      *([kpool] * (ng * pps)), *([vpool] * (ng * pps)))
    return op, os_.reshape(bs, n_heads, ts, HEAD_DIM).transpose(0, 2, 1, 3).reshape(bs * ts, d)


def _t5_bucket(dist):
    n = np.asarray(dist, dtype=np.int64)
    exact = N_BUCKETS // 2
    large = exact + (np.log(np.maximum(n, 1) / exact) / np.log(BUCKET_MAX_DIST / exact) * (N_BUCKETS - exact)).astype(np.int64)
    return np.where(n < exact, n, np.minimum(large, N_BUCKETS - 1)).astype(np.int32)


def _band_bias(rel_bias, group_heads):
    a = np.arange(Q_BLOCK)[:, None]
    bcol = np.arange(2 * Q_BLOCK)[None, :]
    j = a - bcol + Q_BLOCK
    out = []
    for g, (window, dil) in enumerate(PATTERNS):
        assert window // dil == Q_BLOCK
        valid = (j >= 0) & (j <= window // dil)
        bucket = _t5_bucket(dil * np.clip(j, 0, window // dil))
        bias = _select_buckets(rel_bias[:, g * group_heads:(g + 1) * group_heads], bucket)
        out.append(jnp.where(valid[None], bias, NEG))
    return jnp.stack(out)


def _select_buckets(rb, bucket):
    onehot = (jnp.asarray(bucket.reshape(-1))[None, :] == jnp.arange(N_BUCKETS)[:, None]).astype(F32)
    sel = jnp.einsum("bh,bn->hn", rb.astype(F32), onehot, precision=lax.Precision.HIGHEST)
    return sel.reshape((rb.shape[1],) + bucket.shape)


DILATED_BLOCKS_PER_STEP = 4


def _dilated_prompt_kernel(*refs, s):
    n_q = N_GROUPS * 2
    q_refs = refs[:n_q]
    k_ref, v_ref, bias_ref, o_ref, m_ref, l_ref, acc_ref = refs[n_q:]
    qb_rows = Q_BLOCK
    dn = (((1,), (1,)), ((), ()))

    def blocks(g, specs, merge):
        bias = bias_ref[g].reshape(2 * qb_rows, 2 * qb_rows)
        scores = []
        for q_idx, k_idx, first in specs:
            q2 = jnp.concatenate([q_refs[2 * g + r][q_idx, :].astype(BF16) for r in range(2)], axis=0)
            sc = lax.dot_general(q2, k_ref[k_idx, :].astype(BF16), dn, preferred_element_type=F32)
            scores.append(sc + (bias[:, qb_rows:] if first else bias))
        stats = []
        for sc in scores:
            m_blk = jnp.max(sc, axis=-1, keepdims=True)
            p = jnp.exp(sc - m_blk)
            stats.append((m_blk, jnp.sum(p, axis=-1, keepdims=True), p.astype(BF16)))
        outs = [jnp.dot(p, v_ref[k_idx, :].astype(BF16), preferred_element_type=F32)
                for (_, _, p), (_, k_idx, _) in zip(stats, specs)]
        for (m_blk, l_blk, _), o_blk, (q_idx, _, _) in zip(stats, outs, specs):
            for r in range(2):
                rs = slice(r * qb_rows, (r + 1) * qb_rows)
                m_b = jnp.broadcast_to(m_blk[rs], (qb_rows, HEAD_DIM))
                l_b = jnp.broadcast_to(l_blk[rs], (qb_rows, HEAD_DIM))
                if merge:
                    m_old = m_ref[r, q_idx, :]
                    m_new = jnp.maximum(m_old, m_b)
                    a_old = jnp.exp(m_old - m_new)
                    a_blk = jnp.exp(m_b - m_new)
                    l_ref[r, q_idx, :] = l_ref[r, q_idx, :] * a_old + l_b * a_blk
                    acc_ref[r, q_idx, :] = acc_ref[r, q_idx, :] * a_old + o_blk[rs] * a_blk
                    m_ref[r, q_idx, :] = m_new
                else:
                    m_ref[r, q_idx, :] = m_b
                    l_ref[r, q_idx, :] = l_b
                    acc_ref[r, q_idx, :] = o_blk[rs]

    per_step = DILATED_BLOCKS_PER_STEP
    for g, (_, dil) in enumerate(PATTERNS):
        n_blocks = s // (dil * qb_rows)
        merge = g > 0

        def idx(start, size, dil=dil):
            return pl.ds(start, size) if dil == 1 else pl.ds(start, size, stride=dil)

        def spec(c, qb, dil=dil, idx=idx):
            q0 = c + dil * qb_rows * qb
            if isinstance(qb, int) and qb == 0:
                return idx(q0, qb_rows), idx(q0, qb_rows), True
            return idx(q0, qb_rows), idx(q0 - dil * qb_rows, 2 * qb_rows), False

        if n_blocks == 1:
            assert dil % per_step == 0

            def classes(it, carry, g=g, merge=merge, spec=spec):
                blocks(g, [spec(it * per_step + u, 0) for u in range(per_step)], merge)
                return carry

            lax.fori_loop(0, dil // per_step, classes, 0)
        elif n_blocks <= per_step:

            def one_class(c, carry, g=g, merge=merge, spec=spec, n_blocks=n_blocks):
                blocks(g, [spec(c, qb) for qb in range(n_blocks)], merge)
                return carry

            lax.fori_loop(0, dil, one_class, 0)
        else:
            assert dil == 1 and (n_blocks - 1) % (per_step - 1) == 0
            blocks(g, [spec(0, 0)], merge)
            grp = per_step - 1

            def later(it, carry, g=g, merge=merge, spec=spec, grp=grp):
                blocks(g, [spec(0, 1 + it * grp + u) for u in range(grp)], merge)
                return carry

            lax.fori_loop(0, (n_blocks - 1) // grp, later, 0)

    for r in range(2):
        o_ref[:, r * HEAD_DIM:(r + 1) * HEAD_DIM] = (acc_ref[r] / l_ref[r]).astype(o_ref.dtype)


def dilated_prompt_attention(q, k, v, rel_bias, b, s, n_kv, group_heads):
    assert group_heads == 2 * n_kv
    bias = _band_bias(rel_bias, group_heads).reshape(N_GROUPS, n_kv, 2, Q_BLOCK, 2 * Q_BLOCK)

    def q_spec(g, r):
        return pl.BlockSpec((s, HEAD_DIM), lambda bi, kv: (bi, g * group_heads + 2 * kv + r))

    return pl.pallas_call(
        functools.partial(_dilated_prompt_kernel, s=s),
        grid=(b, n_kv),
        in_specs=[q_spec(g, r) for g in range(N_GROUPS) for r in range(2)] + [
            pl.BlockSpec((s, HEAD_DIM), lambda bi, kv: (bi, kv)),
            pl.BlockSpec((s, HEAD_DIM), lambda bi, kv: (bi, kv)),
            pl.BlockSpec((N_GROUPS, None, 2, Q_BLOCK, 2 * Q_BLOCK), lambda bi, kv: (0, kv, 0, 0, 0)),
        ],
        out_specs=pl.BlockSpec((s, 2 * HEAD_DIM), lambda bi, kv: (bi, kv)),
        out_shape=jax.ShapeDtypeStruct((b * s, group_heads * HEAD_DIM), BF16),
        scratch_shapes=[pltpu.VMEM((2, s, HEAD_DIM), F32)] * 3,
        compiler_params=_params("parallel", "parallel"),
        name="dilated_prompt_attention",
    )(*([q] * (N_GROUPS * 2)), k, v, bias)


def _sample_bias(rel_bias, group_heads, n_kv, t, n_buf):
    dist = n_buf + np.arange(t)[:, None] - np.arange(n_buf + t)[None, :]
    out = []
    for g, (window, dil) in enumerate(PATTERNS):
        valid = (dist >= 0) & (dist % dil == 0) & (dist <= window)
        bucket = _t5_bucket(np.clip(dist, 0, window))
        bias = _select_buckets(rel_bias[:, g * group_heads:(g + 1) * group_heads], bucket)
        out.append(jnp.where(valid[None], bias, NEG).reshape(n_kv, 2, t, n_buf + t))
    return jnp.stack(out, axis=1).reshape(n_kv, N_GROUPS * 2 * t, n_buf + t)


def _dilated_sample_kernel(q_ref, kc_ref, vc_ref, kn_ref, vn_ref, bc_ref, bn_ref, o_ref, *, n_kv, group_heads):
    t = q_ref.shape[0]
    n_buf = kc_ref.shape[0]
    for kv in range(n_kv):
        heads = [g * group_heads + 2 * kv + r for g in range(N_GROUPS) for r in range(2)]
        q6 = jnp.concatenate([q_ref[:, h * HEAD_DIM:(h + 1) * HEAD_DIM] for h in heads], axis=0).astype(BF16)
        kc = _tile_rows(kc_ref, kv).astype(BF16)
        vc = _tile_rows(vc_ref, kv).astype(BF16)
        sl = slice(kv * HEAD_DIM, (kv + 1) * HEAD_DIM)
        kn = kn_ref[:, sl].astype(BF16)
        vn = vn_ref[:, sl]
        dn = (((1,), (1,)), ((), ()))
        s_c = lax.dot_general(q6, kc, dn, preferred_element_type=F32) + bc_ref[kv]
        s_n = lax.dot_general(q6, kn, dn, preferred_element_type=F32) + bn_ref[kv]
        m_row = jnp.maximum(jnp.max(s_c, axis=-1, keepdims=True), jnp.max(s_n, axis=-1, keepdims=True))
        m = jnp.max(m_row.reshape(N_GROUPS, 2 * t, 1), axis=0)
        m_all = jnp.concatenate([m] * N_GROUPS, axis=0)
        p_c = jnp.exp(s_c - m_all)
        p_n = jnp.exp(s_n - m_all)
        l_row = jnp.sum(p_c, axis=-1, keepdims=True) + jnp.sum(p_n, axis=-1, keepdims=True)
        o_row = jnp.dot(p_c.astype(BF16), vc, preferred_element_type=F32) + jnp.dot(p_n, vn, preferred_element_type=F32)
        l = jnp.sum(l_row.reshape(N_GROUPS, 2 * t, 1), axis=0)
        o = jnp.sum(o_row.reshape(N_GROUPS, 2 * t, HEAD_DIM), axis=0) / l
        for r in range(2):
            o_ref[:, (2 * kv + r) * HEAD_DIM:(2 * kv + r + 1) * HEAD_DIM] = o[r * t:(r + 1) * t]


def dilated_sample_attention(q, k_new, v_new, cache_k, cache_v, rel_bias, b, t, n_kv, group_heads):
    n_buf = cache_k.shape[1]
    bias = _sample_bias(rel_bias, group_heads, n_kv, t, n_buf)
    bias_c, bias_n = bias[:, :, :n_buf], bias[:, :, n_buf:]
    dq = q.shape[1]
    dkv = n_kv * HEAD_DIM
    return pl.pallas_call(
        functools.partial(_dilated_sample_kernel, n_kv=n_kv, group_heads=group_heads),
        grid=(b,),
        in_specs=[
            pl.BlockSpec((t, dq), lambda i: (i, 0)),
            pl.BlockSpec((None, n_buf, n_kv, HEAD_DIM), lambda i: (i, 0, 0, 0)),
            pl.BlockSpec((None, n_buf, n_kv, HEAD_DIM), lambda i: (i, 0, 0, 0)),
            pl.BlockSpec((t, dkv), lambda i: (i, 0)),
            pl.BlockSpec((t, dkv), lambda i: (i, 0)),
            pl.BlockSpec(bias_c.shape, lambda i: (0, 0, 0)),
            pl.BlockSpec(bias_n.shape, lambda i: (0, 0, 0)),
        ],
        out_specs=pl.BlockSpec((t, group_heads * HEAD_DIM), lambda i: (i, 0)),
        out_shape=jax.ShapeDtypeStruct((b * t, group_heads * HEAD_DIM), F32),
        compiler_params=_params("parallel"),
        name="dilated_sample_attention",
    )(q, cache_k, cache_v, k_new, v_new, bias_c, bias_n)


CONV_PAD_ROWS = 8


FFN_COL_TILE = 256
FFN_ROW_CHUNKS = 4


def _ffn_hidden_kernel(xp_ref, xs_ref, wu_ref, wg_ref, prev_ref, wc_ref, bc_ref, wd_ref, hp_ref, hs_ref, stp_ref,
                       sts_ref, wdb_ref, pad_ref, *, n_seq_s, t_s):
    i = pl.program_id(0)
    wdb_ref[...] = wd_ref[...].astype(BF16)
    t = xp_ref.shape[0]
    lo = CONV_PAD_ROWS - (CONV_W - 1)
    rc = t // FFN_ROW_CHUNKS
    wu = wu_ref[...].astype(BF16)
    wg = wg_ref[...].astype(BF16)

    def conv_gelu_gate(gate, start, rows):
        uc = bc_ref[...] + pad_ref[pl.ds(lo + start, rows), :] * wc_ref[0:1, :]
        for tap in range(1, CONV_W):
            uc = uc + pad_ref[pl.ds(lo + start + tap, rows), :] * wc_ref[tap:tap + 1, :]
        return 0.5 * uc * (1.0 + lax.erf(uc * SQRT_HALF)) * gate

    @pl.when(i == 0)
    def _():
        x = xs_ref[...]
        u = jnp.dot(x, wu, preferred_element_type=F32)
        gate = jnp.dot(x, wg, preferred_element_type=F32)
        pieces = []
        for sq in range(n_seq_s):
            rs = slice(sq * t_s, (sq + 1) * t_s)
            pad_ref[pl.ds(CONV_PAD_ROWS, t_s), :] = u[rs]
            pad_ref[pl.ds(lo, CONV_W - 1), :] = prev_ref[sq]
            pieces.append(conv_gelu_gate(gate[rs], 0, t_s))
            sts_ref[sq] = pad_ref[pl.ds(lo + t_s, CONV_W - 1), :]
        hs_ref[...] = jnp.concatenate(pieces, axis=0).astype(hs_ref.dtype)

    pad_ref[pl.ds(lo, CONV_W - 1), :] = jnp.zeros((CONV_W - 1, pad_ref.shape[1]), F32)
    gates = []
    for c in range(FFN_ROW_CHUNKS):
        x = xp_ref[c * rc:(c + 1) * rc, :]
        pad_ref[pl.ds(CONV_PAD_ROWS + c * rc, rc), :] = jnp.dot(x, wu, preferred_element_type=F32)
        gates.append(jnp.dot(x, wg, preferred_element_type=F32))
    for c in range(FFN_ROW_CHUNKS):
        hp_ref[c * rc:(c + 1) * rc, :] = conv_gelu_gate(gates[c], c * rc, rc).astype(hp_ref.dtype)
    stp_ref[...] = pad_ref[pl.ds(lo + t, CONV_W - 1), :]


def conv_ffn_hidden(xp, xs, w_up, w_gate, w_down, layer, prev_s, w_conv, b_conv, n_seq_p, t_p, n_seq_s, t_s):
    mp, d = xp.shape
    ms = xs.shape[0]
    f = w_up.shape[2]
    tn = FFN_COL_TILE
    assert mp == n_seq_p * t_p and ms == n_seq_s * t_s and f % tn == 0 and t_p % (FFN_ROW_CHUNKS * 128) == 0
    last = f // tn - 1
    n_col = f // tn
    assert f % (n_seq_p * n_col) == 0
    slab = f // (n_seq_p * n_col)
    wd_spec = lambda lead: pl.BlockSpec((None, slab, d), lambda i, j: (lead, i * n_col + j, 0))
    wspec = pl.BlockSpec((None, d, tn), lambda i, j: (layer, 0, j))
    s_col = lambda i, j: jnp.where(i == 0, j, last)
    return pl.pallas_call(
        functools.partial(_ffn_hidden_kernel, n_seq_s=n_seq_s, t_s=t_s),
        grid=(n_seq_p, f // tn),
        in_specs=[
            pl.BlockSpec((t_p, d), lambda i, j: (i, 0), pipeline_mode=pl.Buffered(1)),
            pl.BlockSpec((ms, d), lambda i, j: (0, 0)),
            wspec, wspec,
            pl.BlockSpec((None, n_seq_s, CONV_W - 1, tn), lambda i, j: (layer, 0, 0, s_col(i, j))),
            pl.BlockSpec((None, CONV_W, tn), lambda i, j: (layer, 0, j)),
            pl.BlockSpec((None, 1, tn), lambda i, j: (layer, 0, j)),
            wd_spec(layer),
        ],
        out_specs=[
            pl.BlockSpec((t_p, tn), lambda i, j: (i, j)),
            pl.BlockSpec((ms, tn), lambda i, j: (0, s_col(i, j))),
            pl.BlockSpec((None, CONV_W - 1, tn), lambda i, j: (i, 0, j)),
            pl.BlockSpec((n_seq_s, CONV_W - 1, tn), lambda i, j: (0, 0, s_col(i, j))),
            wd_spec(0),
        ],
        out_shape=[jax.ShapeDtypeStruct((mp, f), BF16), jax.ShapeDtypeStruct((ms, f), BF16),
                   jax.ShapeDtypeStruct((n_seq_p, CONV_W - 1, f), F32),
                   jax.ShapeDtypeStruct((n_seq_s, CONV_W - 1, f), F32),
                   jax.ShapeDtypeStruct((1, f, d), BF16)],
        scratch_shapes=[pltpu.VMEM((t_p + CONV_PAD_ROWS, tn), F32)],
        compiler_params=_params("arbitrary", "arbitrary"),
        name="conv_ffn_hidden",
    )(xp, xs, w_up, w_gate, prev_s, w_conv, b_conv.reshape(b_conv.shape[0], 1, f), w_down)


def _shift_append_kernel(old_ref, new_ref, o_ref):
    n_keep = old_ref.shape[1]
    o_ref[:n_keep] = old_ref[0]
    o_ref[n_keep:] = new_ref[...]


def shift_append(cache, new):
    b, n_buf, kv, dh = cache.shape
    t = new.shape[1]
    old_block = (pl.Element(1), pl.Element(n_buf - t), pl.Element(kv), pl.Element(dh))
    return pl.pallas_call(
        _shift_append_kernel,
        grid=(b,),
        in_specs=[pl.BlockSpec(old_block, lambda i: (i, t, 0, 0)),
                  pl.BlockSpec((None, t, kv, dh), lambda i: (i, 0, 0, 0))],
        out_specs=pl.BlockSpec((None, n_buf, kv, dh), lambda i: (i, 0, 0, 0)),
        out_shape=jax.ShapeDtypeStruct(cache.shape, cache.dtype),
        compiler_params=_params("parallel"),
        name="shift_append",
    )(cache, new)


def kernel(x_prompt, x_sample, cache_fox_k, cache_fox_v, cache_fox_logf, cache_win_k, cache_win_v, state_ffn_conv,
           page_table, w_in_a, b_f_a, w_o_a, g_attn, g_kv, w_kv_b, w_q_b, w_o_b, rel_bias,
           g_ffn, w_up, w_gate, w_conv, b_conv, w_down, g_final):
    bp, sp, d_model = x_prompt.shape
    bs, ts, _ = x_sample.shape
    n_heads = b_f_a.shape[1]
    hd = n_heads * HEAD_DIM
    n_kv = cache_win_k.shape[2]
    kvw = n_kv * HEAD_DIM
    group_heads = w_q_b.shape[2] // (N_GROUPS * HEAD_DIM)
    d_ff = w_up.shape[2]
    n_pages = page_table.shape[1]
    assert w_in_a.shape[0] == 1 and w_q_b.shape[0] == 1 and g_attn.shape[0] == 2
    mp, ms = bp * sp, bs * ts

    b_f = b_f_a[0][None].astype(F32)
    w_in_t = jnp.swapaxes(w_in_a, 1, 2)
    xp = x_prompt.reshape(mp, d_model)
    xs = x_sample.reshape(ms, d_model)

    def norm(x, gains, dtype=BF16):
        return rmsnorm(x, gains, dtype)

    def ffn(xp, xs, layer):
        (xnp,) = norm(xp, g_ffn[layer][None])
        (xns,) = norm(xs, g_ffn[layer][None])
        hp, hs, stp, sts, w_down_b = conv_ffn_hidden(xnp, xns, w_up, w_gate, w_down, layer, state_ffn_conv, w_conv,
                                                     b_conv, bp, sp, bs, ts)
        xp = matmul(hp, w_down_b, 0, res=xp, bm=512, bn=512)
        xs = matmul(hs, w_down_b, 0, res=xs, bm=ms, bn=1024, bk=d_ff // 2)
        return xp, xs, stp, sts

    (xnp,) = norm(xp, g_attn[0][None])
    (xns,) = norm(xs, g_attn[0][None])
    qp, qs = matmul_pair(xnp, xns, w_in_t, 0, 0, hd, out_dtype=BF16, out_scale=ATTN_SCALE, transposed=True)
    kp, ks = matmul_pair(xnp, xns, w_in_t, 0, hd, hd, transposed=True)
    vp, vs = matmul_pair(xnp, xns, w_in_t, 0, 2 * hd, hd, transposed=True)
    lfp = forget_gate(xnp, w_in_t, 0, 3 * hd, b_f)
    lfs = forget_gate(xns, w_in_t, 0, 3 * hd, b_f)
    cp = cumsum_rows(lfp.reshape(bp, sp, n_heads))
    cs = cumsum_rows(lfs.reshape(bs, ts, n_heads))
    pt_flat = page_table.reshape(-1)
    d_past = page_suffix_sums(pt_flat, cache_fox_logf[0], bs, n_pages)
    ap, a_s = fox_attention(qp, kp, vp, cp.reshape(mp, n_heads), bp, sp, qs, cs, d_past, ks, vs, cache_fox_k[0],
                            cache_fox_v[0], pt_flat, bs, ts, n_heads, n_pages)
    a_s = a_s.astype(BF16)
    xp, xs = matmul_pair(ap, a_s, w_o_a, 0, 0, d_model, res=(xp, xs))
    xp, xs, st0p, st0s = ffn(xp, xs, 0)

    xnp, xkvp = norm(xp, jnp.stack([g_attn[1], g_kv]))
    xns, xkvs = norm(xs, jnp.stack([g_attn[1], g_kv]))
    kvp, kvs = matmul_pair(xkvp, xkvs, w_kv_b[None], 0, 0, 2 * kvw)
    qp, qs = matmul_pair(xnp, xns, w_q_b, 0, 0, w_q_b.shape[2], out_scale=ATTN_SCALE)
    kbp, vbp = kvp[:, :kvw], kvp[:, kvw:]
    kbs, vbs = kvs[:, :kvw], kvs[:, kvw:]
    ap = dilated_prompt_attention(qp, kbp, vbp, rel_bias, bp, sp, n_kv, group_heads)
    a_s = dilated_sample_attention(qs, kbs, vbs, cache_win_k, cache_win_v, rel_bias, bs, ts, n_kv,
                                   group_heads).astype(BF16)
    xp, xs = matmul_pair(ap, a_s, w_o_b, 0, 0, d_model, res=(xp, xs))
    xp, xs, st1p, st1s = ffn(xp, xs, 1)
    (yp,) = norm(xp, g_final[None], F32)
    (ys,) = norm(xs, g_final[None], F32)

    n_pg = sp // PAGE_SIZE
    keep = min(BUCKET_MAX_DIST, sp)
    return (
        yp.reshape(bp, sp, d_model),
        ys.reshape(bs, ts, d_model),
        kp.reshape(1, bp, n_pg, PAGE_SIZE, n_heads, HEAD_DIM),
        vp.reshape(1, bp, n_pg, PAGE_SIZE, n_heads, HEAD_DIM),
        lfp.reshape(1, bp, n_pg, PAGE_SIZE, n_heads),
        ks.reshape(1, bs, ts, n_heads, HEAD_DIM),
        vs.reshape(1, bs, ts, n_heads, HEAD_DIM),
        lfs.reshape(1, bs, ts, n_heads),
        kbp.reshape(bp, sp, n_kv, HEAD_DIM)[:, sp - keep:],
        vbp.reshape(bp, sp, n_kv, HEAD_DIM)[:, sp - keep:],
        shift_append(cache_win_k, kbs.reshape(bs, ts, n_kv, HEAD_DIM)),
        shift_append(cache_win_v, vbs.reshape(bs, ts, n_kv, HEAD_DIM)),
        jnp.stack([st0p, st1p]),
        jnp.stack([st0s, st1s]),
    )
```

```python
import functools
import math

import numpy as np
import jax
import jax.numpy as jnp
from jax import lax
from jax.experimental import pallas as pl
from jax.experimental.pallas import tpu as pltpu

F32 = jnp.float32
BF16 = jnp.bfloat16

HEAD_DIM = 128
PAGE_SIZE = 128
PATTERNS = ((128, 1), (512, 4), (2048, 16))
N_GROUPS = len(PATTERNS)
N_BUCKETS = 32
BUCKET_MAX_DIST = max(w for w, _ in PATTERNS)
CONV_W = 3
Q_BLOCK = 128
NORM_EPS = 1e-6
ATTN_SCALE = HEAD_DIM ** -0.5
NEG = -1e30
SQRT_HALF = 0.7071067811865476

V7X_VMEM_LIMIT_BYTES = 56 * 1024 * 1024


def _params(*semantics):
    return pltpu.CompilerParams(dimension_semantics=semantics, vmem_limit_bytes=V7X_VMEM_LIMIT_BYTES)


def _rmsnorm_kernel(x_ref, g_ref, *o_refs):
    x = x_ref[...]
    y = x * lax.rsqrt(jnp.mean(x * x, axis=-1, keepdims=True) + NORM_EPS)
    for j, o_ref in enumerate(o_refs):
        o_ref[...] = (y * g_ref[j:j + 1, :]).astype(o_ref.dtype)


def rmsnorm(x, gains, out_dtype):
    m, d = x.shape
    n = gains.shape[0]
    tm = min(256, m)
    return pl.pallas_call(
        _rmsnorm_kernel,
        grid=(m // tm,),
        in_specs=[pl.BlockSpec((tm, d), lambda i: (i, 0)), pl.BlockSpec((n, d), lambda i: (0, 0))],
        out_specs=[pl.BlockSpec((tm, d), lambda i: (i, 0))] * n,
        out_shape=[jax.ShapeDtypeStruct((m, d), out_dtype)] * n,
        compiler_params=_params("parallel"),
        name="rmsnorm",
    )(x, gains)


def _matmul_kernel(*refs, has_res, nk, out_scale):
    x_ref, w_ref = refs[0], refs[1]
    res_ref = refs[2] if has_res else None
    o_ref = refs[2 + has_res]

    def finish(acc):
        if out_scale != 1.0:
            acc = acc * out_scale
        if has_res:
            acc = acc + res_ref[...]
        o_ref[...] = acc.astype(o_ref.dtype)

    part = jnp.dot(x_ref[...], w_ref[...], preferred_element_type=F32)
    if nk == 1:
        finish(part)
        return
    acc_ref = refs[3 + has_res]
    k = pl.program_id(2)

    @pl.when(k == 0)
    def _():
        acc_ref[...] = part

    @pl.when(jnp.logical_and(k > 0, k < nk - 1))
    def _():
        acc_ref[...] += part

    @pl.when(k == nk - 1)
    def _():
        finish(acc_ref[...] + part)


def matmul(x, w, layer, *, res=None, out_dtype=F32, out_scale=1.0, bm=1024, bn=1024, bk=None):
    m, kdim = x.shape
    n = w.shape[2]
    bm, bn = min(bm, m), min(bn, n)
    bk = kdim if bk is None else bk
    nk = kdim // bk
    assert m % bm == 0 and n % bn == 0 and kdim % bk == 0 and nk >= 1
    in_specs = [pl.BlockSpec((bm, bk), lambda i, j, k: (i, k)),
                pl.BlockSpec((None, bk, bn), lambda i, j, k: (layer, k, j))]
    args = [x, w]
    if res is not None:
        in_specs.append(pl.BlockSpec((bm, bn), lambda i, j, k: (i, j)))
        args.append(res)
    return pl.pallas_call(
        functools.partial(_matmul_kernel, has_res=res is not None, nk=nk, out_scale=out_scale),
        grid=(m // bm, n // bn, nk),
        in_specs=in_specs,
        out_specs=pl.BlockSpec((bm, bn), lambda i, j, k: (i, j)),
        out_shape=jax.ShapeDtypeStruct((m, n), out_dtype),
        scratch_shapes=[pltpu.VMEM((bm, bn), F32)] if nk > 1 else [],
        compiler_params=_params("parallel", "parallel", "arbitrary"),
        name="matmul",
    )(*args)


def _pair_matmul_kernel(*refs, nm, has_res, out_scale, transposed):
    if has_res:
        xp_ref, xs_ref, w_ref, rp_ref, rs_ref, op_ref, os_ref, wb_ref = refs
    else:
        xp_ref, xs_ref, w_ref, op_ref, os_ref, wb_ref = refs
        rp_ref = rs_ref = None
    i = pl.program_id(1)

    @pl.when(i == 0)
    def _():
        wb_ref[...] = w_ref[...].astype(BF16)

    def emit(x_ref, r_ref, o_ref):
        dims = (((1,), (1 if transposed else 0,)), ((), ()))
        acc = lax.dot_general(x_ref[...], wb_ref[...], dims, preferred_element_type=F32)
        if out_scale != 1.0:
            acc = acc * out_scale
        if r_ref is not None:
            acc = acc + r_ref[...]
        o_ref[...] = acc.astype(o_ref.dtype)

    @pl.when(i < nm)
    def _():
        emit(xp_ref, rp_ref, op_ref)

    @pl.when(i == nm)
    def _():
        emit(xs_ref, rs_ref, os_ref)


def matmul_pair(xp, xs, w, layer, col0, n, *, res=None, out_dtype=F32, out_scale=1.0, transposed=False,
                bm=256, bn=1024):
    mp, kdim = xp.shape
    ms = xs.shape[0]
    assert mp % bm == 0 and n % bn == 0 and col0 % bn == 0 and w.shape[2 if transposed else 1] == kdim
    nm, cb0, last = mp // bm, col0 // bn, mp // bm - 1
    row = lambda j, i: (jnp.minimum(i, last), 0)
    tile = lambda j, i: (jnp.minimum(i, last), j)
    if transposed:
        w_spec = pl.BlockSpec((None, bn, kdim), lambda j, i: (layer, cb0 + j, 0))
    else:
        w_spec = pl.BlockSpec((None, kdim, bn), lambda j, i: (layer, 0, cb0 + j))
    in_specs = [pl.BlockSpec((bm, kdim), row), pl.BlockSpec((ms, kdim), lambda j, i: (0, 0)), w_spec]
    args = [xp, xs, w]
    if res is not None:
        in_specs += [pl.BlockSpec((bm, bn), tile), pl.BlockSpec((ms, bn), lambda j, i: (0, j))]
        args += list(res)
    return pl.pallas_call(
        functools.partial(_pair_matmul_kernel, nm=nm, has_res=res is not None, out_scale=out_scale,
                          transposed=transposed),
        grid=(n // bn, nm + 1),
        in_specs=in_specs,
        out_specs=[pl.BlockSpec((bm, bn), tile), pl.BlockSpec((ms, bn), lambda j, i: (0, j))],
        out_shape=[jax.ShapeDtypeStruct((mp, n), out_dtype), jax.ShapeDtypeStruct((ms, n), out_dtype)],
        scratch_shapes=[pltpu.VMEM((bn, kdim) if transposed else (kdim, bn), BF16)],
        compiler_params=_params("parallel", "arbitrary"),
        name="matmul_pair",
    )(*args)


def _gate_kernel(x_ref, w_ref, b_ref, o_ref):
    z = lax.dot_general(x_ref[...], w_ref[...].astype(BF16), (((1,), (1,)), ((), ())),
                        preferred_element_type=F32) + b_ref[...]
    o_ref[...] = -(jnp.maximum(-z, 0.0) + jnp.log1p(jnp.exp(-jnp.abs(z))))


def forget_gate(xn, w_t, layer, row0, b_f):
    m, d = xn.shape
    nh = b_f.shape[1]
    tm = min(512, m)
    assert row0 % nh == 0 and nh % 8 == 0
    return pl.pallas_call(
        _gate_kernel,
        grid=(m // tm,),
        in_specs=[pl.BlockSpec((tm, d), lambda i: (i, 0)),
                  pl.BlockSpec((None, nh, d), lambda i: (layer, row0 // nh, 0)),
                  pl.BlockSpec((1, nh), lambda i: (0, 0))],
        out_specs=pl.BlockSpec((tm, nh), lambda i: (i, 0)),
        out_shape=jax.ShapeDtypeStruct((m, nh), F32),
        compiler_params=_params("parallel"),
        name="forget_gate",
    )(xn, w_t, b_f)


def _split_dot(tri, x):
    hi = x.astype(BF16)
    r1 = x - hi.astype(F32)
    mid = r1.astype(BF16)
    lo = (r1 - mid.astype(F32)).astype(BF16)
    out = jnp.dot(tri, lo, preferred_element_type=F32)
    out = out + jnp.dot(tri, mid, preferred_element_type=F32)
    return out + jnp.dot(tri, hi, preferred_element_type=F32)


def _cumsum_kernel(x_ref, o_ref, *, chunk):
    s = x_ref.shape[0]
    row = lax.broadcasted_iota(jnp.int32, (chunk, chunk), 0)
    col = lax.broadcasted_iota(jnp.int32, (chunk, chunk), 1)
    tri = jnp.where(col <= row, 1.0, 0.0).astype(BF16)
    carry = jnp.zeros((1, x_ref.shape[1]), F32)
    for i in range(s // chunk):
        y = _split_dot(tri, x_ref[i * chunk:(i + 1) * chunk, :]) + carry
        o_ref[i * chunk:(i + 1) * chunk, :] = y
        carry = y[chunk - 1:chunk, :]


def _cumsum_small_kernel(x_ref, o_ref):
    x = x_ref[...]
    t = x.shape[0]
    row = lax.broadcasted_iota(jnp.int32, x.shape, 0)
    for i in range(t):
        o_ref[i:i + 1, :] = jnp.sum(jnp.where(row <= i, x, 0.0), axis=0, keepdims=True)


def cumsum_rows(x):
    b, s, nh = x.shape
    body = _cumsum_small_kernel if s < 128 else functools.partial(_cumsum_kernel, chunk=min(256, s))
    return pl.pallas_call(
        body,
        grid=(b,),
        in_specs=[pl.BlockSpec((None, s, nh), lambda i: (i, 0, 0))],
        out_specs=pl.BlockSpec((None, s, nh), lambda i: (i, 0, 0)),
        out_shape=jax.ShapeDtypeStruct((b, s, nh), F32),
        compiler_params=_params("parallel"),
        name="cumsum_rows",
    )(x)


SUFFIX_PAGES_PER_STEP = 16


def _page_suffix_kernel(pt_ref, *refs, pps):
    lf_refs, o_ref, carry_ref = refs[:pps], refs[pps], refs[pps + 1]
    j = pl.program_id(1)
    n = lf_refs[0].shape[0]

    @pl.when(j == 0)
    def _():
        carry_ref[...] = jnp.zeros_like(carry_ref)

    row = lax.broadcasted_iota(jnp.int32, (n, n), 0)
    col = lax.broadcasted_iota(jnp.int32, (n, n), 1)
    upper = jnp.where(col > row, 1.0, 0.0).astype(BF16)
    carry = carry_ref[...]
    for pp in reversed(range(pps)):
        x = lf_refs[pp][...]
        o_ref[pp] = _split_dot(upper, x) + carry
        carry = carry + jnp.sum(x, axis=0, keepdims=True)
    carry_ref[...] = carry


def page_suffix_sums(page_table_flat, lf_pool, b, n_pages):
    _, n, nh = lf_pool.shape
    pps = SUFFIX_PAGES_PER_STEP
    assert n_pages % pps == 0
    n_steps = n_pages // pps

    def page_spec(pp):
        return pl.BlockSpec((None, n, nh), lambda i, j, pt: (pt[i * n_pages + n_pages - (j + 1) * pps + pp], 0, 0))

    grid_spec = pltpu.PrefetchScalarGridSpec(
        num_scalar_prefetch=1,
        grid=(b, n_steps),
        in_specs=[page_spec(pp) for pp in range(pps)],
        out_specs=pl.BlockSpec((None, pps, n, nh), lambda i, j, pt: (i, n_steps - 1 - j, 0, 0)),
        scratch_shapes=[pltpu.VMEM((1, nh), F32)],
    )
    return pl.pallas_call(
        functools.partial(_page_suffix_kernel, pps=pps),
        grid_spec=grid_spec,
        out_shape=jax.ShapeDtypeStruct((b, n_pages, n, nh), F32),
        compiler_params=_params("parallel", "arbitrary"),
        name="page_suffix_sums",
    )(page_table_flat, *([lf_pool] * pps))


HEADS_PER_TILE = 8
FOX_PAGES_PER_STEP = 2
FOX_PROMPT_TQ = 512


def _tile_rows(ref, sub):
    rows, n_sub, dh = ref.shape
    return ref.reshape(rows * n_sub, dh)[pl.ds(sub, rows, stride=n_sub), :]


def _fox_prompt_block(q_ref, k_ref, v_ref, c_ref, ct_ref, o_ref, h, ii):
    tq = q_ref.shape[0]
    q = q_ref[...]
    lane = lax.broadcasted_iota(jnp.int32, c_ref.shape, 1)
    cq = jnp.sum(jnp.where(lane == h, c_ref[...], 0.0), axis=-1, keepdims=True)
    dn = (((1,), (1,)), ((), ()))

    def scores(j):
        kb = k_ref[j * tq:(j + 1) * tq, :].astype(BF16)
        return lax.dot_general(q, kb, dn, preferred_element_type=F32) + (cq - ct_ref[j:j + 1, :])

    def update(j, s, carry, masked):
        m, l, acc = carry
        if masked:
            row = lax.broadcasted_iota(jnp.int32, s.shape, 0)
            col = lax.broadcasted_iota(jnp.int32, s.shape, 1)
            s = jnp.where(col <= row, s, NEG)
        m_new = jnp.maximum(m, jnp.max(s, axis=-1, keepdims=True))
        alpha = jnp.exp(m - m_new)
        p = jnp.exp(s - m_new)
        l = l * alpha + jnp.sum(p, axis=-1, keepdims=True)
        vb = v_ref[j * tq:(j + 1) * tq, :].astype(BF16)
        acc = acc * alpha + jnp.dot(p.astype(BF16), vb, preferred_element_type=F32)
        return m_new, l, acc

    carry = (jnp.full((tq, 1), NEG, F32), jnp.zeros((tq, 1), F32), jnp.zeros((tq, HEAD_DIM), F32))
    s_next = scores(0)
    for j in range(ii + 1):
        s_cur = s_next
        if j < ii:
            s_next = scores(j + 1)
        carry = update(j, s_cur, carry, masked=(j == ii))
    _, l, acc = carry
    o_ref[...] = (acc / l).astype(o_ref.dtype)


def _fox_online_update(m_ref, l_ref, acc_ref, s, pv):
    m_old = m_ref[...]
    m_new = jnp.maximum(m_old, jnp.max(s, axis=-1, keepdims=True))
    alpha = jnp.exp(m_old - m_new)
    p = jnp.exp(s - m_new)
    l_ref[...] = l_ref[...] * alpha + jnp.sum(p, axis=-1, keepdims=True)
    acc_ref[...] = acc_ref[...] * alpha + pv(p)
    m_ref[...] = m_new


def _fox_sample_pages(q_ref, cn_ref, dt_ref, k_refs, v_refs, m_ref, l_ref, acc_ref, n_heads, pps):
    ng = n_heads // HEADS_PER_TILE
    t = q_ref.shape[1]
    dn = (((1,), (1,)), ((), ()))

    def head_rows(page_refs, h):
        g, hs = divmod(h, HEADS_PER_TILE)
        tiles = [_tile_rows(page_refs[pp * ng + g], hs).astype(BF16) for pp in range(pps)]
        return tiles[0] if pps == 1 else jnp.concatenate(tiles, axis=0)

    parts = []
    for h in range(n_heads):
        sh = lax.dot_general(q_ref[h], head_rows(k_refs, h), dn, preferred_element_type=F32)
        d_row = [dt_ref[pp, h:h + 1, :] for pp in range(pps)]
        parts.append(sh + (d_row[0] if pps == 1 else jnp.concatenate(d_row, axis=1)))
    cn = cn_ref[...]
    s = jnp.concatenate(parts, axis=0) + (cn if pps == 1 else jnp.concatenate([cn] * pps, axis=1))

    def pv_pages(p):
        return jnp.concatenate(
            [jnp.dot(p[h * t:(h + 1) * t].astype(BF16), head_rows(v_refs, h), preferred_element_type=F32)
             for h in range(n_heads)], axis=0)

    _fox_online_update(m_ref, l_ref, acc_ref, s, pv_pages)


def _fox_sample_new_rows(q_ref, cn_ref, cnt_ref, kn_ref, vn_ref, o_ref, m_ref, l_ref, acc_ref, n_heads):
    t = q_ref.shape[1]
    dn = (((1,), (1,)), ((), ()))
    parts = []
    for h in range(n_heads):
        kh = kn_ref[:, h * HEAD_DIM:(h + 1) * HEAD_DIM].astype(BF16)
        parts.append(lax.dot_general(q_ref[h], kh, dn, preferred_element_type=F32))
    s2 = jnp.concatenate(parts, axis=0) + cn_ref[:, :t] - cnt_ref[...]
    row = lax.broadcasted_iota(jnp.int32, s2.shape, 0)
    col = lax.broadcasted_iota(jnp.int32, s2.shape, 1)
    s2 = jnp.where(col <= lax.rem(row, t), s2, NEG)

    def pv_new(p):
        return jnp.concatenate(
            [jnp.dot(p[h * t:(h + 1) * t], vn_ref[:, h * HEAD_DIM:(h + 1) * HEAD_DIM], preferred_element_type=F32)
             for h in range(n_heads)], axis=0)

    _fox_online_update(m_ref, l_ref, acc_ref, s2, pv_new)
    o_ref[...] = acc_ref[...] / l_ref[...]


def _fox_kernel(pt_ref, qp_ref, kp_ref, vp_ref, c_ref, ct_ref, qs_ref, cn_ref, cnt_ref, dt_ref, kn_ref, vn_ref,
                *refs, nq, n_heads, n_steps, pps):
    nb = n_heads // HEADS_PER_TILE * pps
    k_refs, v_refs = refs[:nb], refs[nb:2 * nb]
    op_ref, os_ref, m_ref, l_ref, acc_ref = refs[2 * nb:]
    step = pl.program_id(0)
    i = lax.rem(step, nq)
    h = lax.rem(step // nq, n_heads)
    js = lax.rem(step, n_steps)

    @pl.when(js == 0)
    def _():
        m_ref[...] = jnp.full_like(m_ref, NEG)
        l_ref[...] = jnp.zeros_like(l_ref)
        acc_ref[...] = jnp.zeros_like(acc_ref)

    for ii in range(nq):

        @pl.when(i == ii)
        def _(ii=ii):
            _fox_sample_pages(qs_ref, cn_ref, dt_ref, k_refs, v_refs, m_ref, l_ref, acc_ref, n_heads, pps)
            _fox_prompt_block(qp_ref, kp_ref, vp_ref, c_ref, ct_ref, op_ref, h, ii)

    @pl.when(js == n_steps - 1)
    def _():
        _fox_sample_new_rows(qs_ref, cn_ref, cnt_ref, kn_ref, vn_ref, os_ref, m_ref, l_ref, acc_ref, n_heads)


def fox_attention(qp, kp, vp, cp, bp, sp, qs, c_new, d_past, k_new, v_new, k_pool, v_pool, page_table_flat,
                  bs, ts, n_heads, n_pages):
    tq = FOX_PROMPT_TQ
    nq = sp // tq
    ng = n_heads // HEADS_PER_TILE
    pps = FOX_PAGES_PER_STEP
    n_steps = n_pages // pps
    assert n_pages % pps == 0 and bp * n_heads * nq == bs * n_steps, "the two groups must have equally many steps"
    d = n_heads * HEAD_DIM
    ht = n_heads * ts
    n_pool = k_pool.shape[0]

    ct = cp.reshape(bp, sp, n_heads).transpose(0, 2, 1).reshape(bp * n_heads, nq, tq)
    q4 = qs.reshape(bs, ts, n_heads, HEAD_DIM).transpose(0, 2, 1, 3)
    c_ht = c_new.transpose(0, 2, 1).reshape(bs, ht, 1)
    cn = jnp.broadcast_to(c_ht, (bs, ht, PAGE_SIZE))
    cnt = jnp.broadcast_to(c_new.transpose(0, 2, 1)[:, :, None, :], (bs, n_heads, ts, ts)).reshape(bs, ht, ts)
    dt = d_past.transpose(0, 1, 3, 2)
    kpool = k_pool.reshape(n_pool, PAGE_SIZE, ng, HEADS_PER_TILE, HEAD_DIM)
    vpool = v_pool.reshape(n_pool, PAGE_SIZE, ng, HEADS_PER_TILE, HEAD_DIM)

    def p_row(s, pt):
        return (s // (n_heads * nq) * nq + lax.rem(s, nq), lax.rem(s // nq, n_heads))

    def p_head(s, pt):
        return (s // (n_heads * nq), lax.rem(s // nq, n_heads))

    def page_spec(pp, g):
        return pl.BlockSpec((None, PAGE_SIZE, None, HEADS_PER_TILE, HEAD_DIM),
                            lambda s, pt: (pt[s // n_steps * n_pages + lax.rem(s, n_steps) * pps + pp], 0, g, 0, 0))

    page_specs = [page_spec(pp, g) for pp in range(pps) for g in range(ng)]
    sb = lambda s, pt: s // n_steps
    grid_spec = pltpu.PrefetchScalarGridSpec(
        num_scalar_prefetch=1,
        grid=(bs * n_steps,),
        in_specs=[
            pl.BlockSpec((tq, HEAD_DIM), p_row),
            pl.BlockSpec((sp, HEAD_DIM), p_head),
            pl.BlockSpec((sp, HEAD_DIM), p_head),
            pl.BlockSpec((tq, n_heads), lambda s, pt: (p_row(s, pt)[0], 0)),
            pl.BlockSpec((None, nq, tq), lambda s, pt: (s // nq, 0, 0)),
            pl.BlockSpec((None, n_heads, ts, HEAD_DIM), lambda s, pt: (sb(s, pt), 0, 0, 0)),
            pl.BlockSpec((None, ht, PAGE_SIZE), lambda s, pt: (sb(s, pt), 0, 0)),
            pl.BlockSpec((None, ht, ts), lambda s, pt: (sb(s, pt), 0, 0)),
            pl.BlockSpec((None, pps, n_heads, PAGE_SIZE), lambda s, pt: (sb(s, pt), lax.rem(s, n_steps), 0, 0)),
            pl.BlockSpec((ts, d), lambda s, pt: (sb(s, pt), 0)),
            pl.BlockSpec((ts, d), lambda s, pt: (sb(s, pt), 0)),
        ] + page_specs * 2,
        out_specs=[pl.BlockSpec((tq, HEAD_DIM), p_row),
                   pl.BlockSpec((None, ht, HEAD_DIM), lambda s, pt: (sb(s, pt), 0, 0))],
        scratch_shapes=[pltpu.VMEM((ht, 1), F32), pltpu.VMEM((ht, 1), F32), pltpu.VMEM((ht, HEAD_DIM), F32)],
    )
    op, os_ = pl.pallas_call(
        functools.partial(_fox_kernel, nq=nq, n_heads=n_heads, n_steps=n_steps, pps=pps),
        grid_spec=grid_spec,
        out_shape=[jax.ShapeDtypeStruct(qp.shape, BF16), jax.ShapeDtypeStruct((bs, ht, HEAD_DIM), F32)],
        compiler_params=_params("arbitrary"),
        name="fox_attention",
    )(page_table_flat, qp, kp, vp, cp, ct, q4, cn, cnt, dt, k_new, v_new,
      *([kpool] * (ng * pps)), *([vpool] * (ng * pps)))
    return op, os_.reshape(bs, n_heads, ts, HEAD_DIM).transpose(0, 2, 1, 3).reshape(bs * ts, d)


def _t5_bucket(dist):
    n = np.asarray(dist, dtype=np.int64)
    exact = N_BUCKETS // 2
    large = exact + (np.log(np.maximum(n, 1) / exact) / np.log(BUCKET_MAX_DIST / exact) * (N_BUCKETS - exact)).astype(np.int64)
    return np.where(n < exact, n, np.minimum(large, N_BUCKETS - 1)).astype(np.int32)


def _band_bias(rel_bias, group_heads):
    a = np.arange(Q_BLOCK)[:, None]
    bcol = np.arange(2 * Q_BLOCK)[None, :]
    j = a - bcol + Q_BLOCK
    out = []
    for g, (window, dil) in enumerate(PATTERNS):
        assert window // dil == Q_BLOCK
        valid = (j >= 0) & (j <= window // dil)
        bucket = _t5_bucket(dil * np.clip(j, 0, window // dil))
        bias = _select_buckets(rel_bias[:, g * group_heads:(g + 1) * group_heads], bucket)
        out.append(jnp.where(valid[None], bias, NEG))
    return jnp.stack(out)


def _select_buckets(rb, bucket):
    onehot = (jnp.asarray(bucket.reshape(-1))[None, :] == jnp.arange(N_BUCKETS)[:, None]).astype(F32)
    sel = jnp.einsum("bh,bn->hn", rb.astype(F32), onehot, precision=lax.Precision.HIGHEST)
    return sel.reshape((rb.shape[1],) + bucket.shape)


DILATED_BLOCKS_PER_STEP = 4


def _dilated_prompt_kernel(*refs, s):
    n_q = N_GROUPS * 2
    q_refs = refs[:n_q]
    k_ref, v_ref, bias_ref, o_ref, m_ref, l_ref, acc_ref = refs[n_q:]
    qb_rows = Q_BLOCK
    dn = (((1,), (1,)), ((), ()))

    def blocks(g, specs, merge):
        bias = bias_ref[g].reshape(2 * qb_rows, 2 * qb_rows)
        scores = []
        for q_idx, k_idx, first in specs:
            q2 = jnp.concatenate([q_refs[2 * g + r][q_idx, :].astype(BF16) for r in range(2)], axis=0)
            sc = lax.dot_general(q2, k_ref[k_idx, :].astype(BF16), dn, preferred_element_type=F32)
            scores.append(sc + (bias[:, qb_rows:] if first else bias))
        stats = []
        for sc in scores:
            m_blk = jnp.max(sc, axis=-1, keepdims=True)
            p = jnp.exp(sc - m_blk)
            stats.append((m_blk, jnp.sum(p, axis=-1, keepdims=True), p.astype(BF16)))
        outs = [jnp.dot(p, v_ref[k_idx, :].astype(BF16), preferred_element_type=F32)
                for (_, _, p), (_, k_idx, _) in zip(stats, specs)]
        for (m_blk, l_blk, _), o_blk, (q_idx, _, _) in zip(stats, outs, specs):
            for r in range(2):
                rs = slice(r * qb_rows, (r + 1) * qb_rows)
                m_b = jnp.broadcast_to(m_blk[rs], (qb_rows, HEAD_DIM))
                l_b = jnp.broadcast_to(l_blk[rs], (qb_rows, HEAD_DIM))
                if merge:
                    m_old = m_ref[r, q_idx, :]
                    m_new = jnp.maximum(m_old, m_b)
                    a_old = jnp.exp(m_old - m_new)
                    a_blk = jnp.exp(m_b - m_new)
                    l_ref[r, q_idx, :] = l_ref[r, q_idx, :] * a_old + l_b * a_blk
                    acc_ref[r, q_idx, :] = acc_ref[r, q_idx, :] * a_old + o_blk[rs] * a_blk
                    m_ref[r, q_idx, :] = m_new
                else:
                    m_ref[r, q_idx, :] = m_b
                    l_ref[r, q_idx, :] = l_b
                    acc_ref[r, q_idx, :] = o_blk[rs]

    per_step = DILATED_BLOCKS_PER_STEP
    order = sorted(range(N_GROUPS), key=lambda g: -PATTERNS[g][1])
    for g in order:
        dil = PATTERNS[g][1]
        n_blocks = s // (dil * qb_rows)
        merge = g != order[0]

        def idx(start, size, dil=dil):
            return pl.ds(start, size) if dil == 1 else pl.ds(start, size, stride=dil)

        def spec(c, qb, dil=dil, idx=idx):
            q0 = c + dil * qb_rows * qb
            if isinstance(qb, int) and qb == 0:
                return idx(q0, qb_rows), idx(q0, qb_rows), True
            return idx(q0, qb_rows), idx(q0 - dil * qb_rows, 2 * qb_rows), False

        if n_blocks == 1:
            assert dil % per_step == 0

            def classes(it, carry, g=g, merge=merge, spec=spec):
                blocks(g, [spec(it * per_step + u, 0) for u in range(per_step)], merge)
                return carry

            lax.fori_loop(0, dil // per_step, classes, 0)
        elif n_blocks <= per_step:

            def one_class(c, carry, g=g, merge=merge, spec=spec, n_blocks=n_blocks):
                blocks(g, [spec(c, qb) for qb in range(n_blocks)], merge)
                return carry

            lax.fori_loop(0, dil, one_class, 0)
        else:
            assert dil == 1 and (n_blocks - 1) % (per_step - 1) == 0
            blocks(g, [spec(0, 0)], merge)
            grp = per_step - 1

            def later(it, carry, g=g, merge=merge, spec=spec, grp=grp):
                blocks(g, [spec(0, 1 + it * grp + u) for u in range(grp)], merge)
                return carry

            lax.fori_loop(0, (n_blocks - 1) // grp, later, 0)

    for r in range(2):
        o_ref[:, r * HEAD_DIM:(r + 1) * HEAD_DIM] = (acc_ref[r] / l_ref[r]).astype(o_ref.dtype)


def dilated_prompt_attention(q, k, v, rel_bias, b, s, n_kv, group_heads):
    assert group_heads == 2 * n_kv
    bias = _band_bias(rel_bias, group_heads).reshape(N_GROUPS, n_kv, 2, Q_BLOCK, 2 * Q_BLOCK)

    def q_spec(g, r):
        return pl.BlockSpec((s, HEAD_DIM), lambda bi, kv: (bi, g * group_heads + 2 * kv + r))

    return pl.pallas_call(
        functools.partial(_dilated_prompt_kernel, s=s),
        grid=(b, n_kv),
        in_specs=[q_spec(g, r) for g in range(N_GROUPS) for r in range(2)] + [
            pl.BlockSpec((s, HEAD_DIM), lambda bi, kv: (bi, kv)),
            pl.BlockSpec((s, HEAD_DIM), lambda bi, kv: (bi, kv)),
            pl.BlockSpec((N_GROUPS, None, 2, Q_BLOCK, 2 * Q_BLOCK), lambda bi, kv: (0, kv, 0, 0, 0)),
        ],
        out_specs=pl.BlockSpec((s, 2 * HEAD_DIM), lambda bi, kv: (bi, kv)),
        out_shape=jax.ShapeDtypeStruct((b * s, group_heads * HEAD_DIM), BF16),
        scratch_shapes=[pltpu.VMEM((2, s, HEAD_DIM), F32)] * 3,
        compiler_params=_params("parallel", "parallel"),
        name="dilated_prompt_attention",
    )(*([q] * (N_GROUPS * 2)), k, v, bias)


def _sample_bias(rel_bias, group_heads, n_kv, t, n_buf):
    dist = n_buf + np.arange(t)[:, None] - np.arange(n_buf + t)[None, :]
    out = []
    for g, (window, dil) in enumerate(PATTERNS):
        valid = (dist >= 0) & (dist % dil == 0) & (dist <= window)
        bucket = _t5_bucket(np.clip(dist, 0, window))
        bias = _select_buckets(rel_bias[:, g * group_heads:(g + 1) * group_heads], bucket)
        out.append(jnp.where(valid[None], bias, NEG).reshape(n_kv, 2, t, n_buf + t))
    return jnp.stack(out, axis=1).reshape(n_kv, N_GROUPS * 2 * t, n_buf + t)


def _dilated_sample_kernel(q_ref, kc_ref, vc_ref, kn_ref, vn_ref, bc_ref, bn_ref, o_ref, *, n_kv, group_heads):
    t = q_ref.shape[0]
    n_buf = kc_ref.shape[0]
    for kv in range(n_kv):
        heads = [g * group_heads + 2 * kv + r for g in range(N_GROUPS) for r in range(2)]
        q6 = jnp.concatenate([q_ref[:, h * HEAD_DIM:(h + 1) * HEAD_DIM] for h in heads], axis=0).astype(BF16)
        kc = _tile_rows(kc_ref, kv).astype(BF16)
        vc = _tile_rows(vc_ref, kv).astype(BF16)
        sl = slice(kv * HEAD_DIM, (kv + 1) * HEAD_DIM)
        kn = kn_ref[:, sl].astype(BF16)
        vn = vn_ref[:, sl]
        dn = (((1,), (1,)), ((), ()))
        s_c = lax.dot_general(q6, kc, dn, preferred_element_type=F32) + bc_ref[kv]
        s_n = lax.dot_general(q6, kn, dn, preferred_element_type=F32) + bn_ref[kv]
        m_row = jnp.maximum(jnp.max(s_c, axis=-1, keepdims=True), jnp.max(s_n, axis=-1, keepdims=True))
        m = jnp.max(m_row.reshape(N_GROUPS, 2 * t, 1), axis=0)
        m_all = jnp.concatenate([m] * N_GROUPS, axis=0)
        p_c = jnp.exp(s_c - m_all)
        p_n = jnp.exp(s_n - m_all)
        l_row = jnp.sum(p_c, axis=-1, keepdims=True) + jnp.sum(p_n, axis=-1, keepdims=True)
        o_row = jnp.dot(p_c.astype(BF16), vc, preferred_element_type=F32) + jnp.dot(p_n, vn, preferred_element_type=F32)
        l = jnp.sum(l_row.reshape(N_GROUPS, 2 * t, 1), axis=0)
        o = jnp.sum(o_row.reshape(N_GROUPS, 2 * t, HEAD_DIM), axis=0) / l
        for r in range(2):
            o_ref[:, (2 * kv + r) * HEAD_DIM:(2 * kv + r + 1) * HEAD_DIM] = o[r * t:(r + 1) * t]


def dilated_sample_attention(q, k_new, v_new, cache_k, cache_v, rel_bias, b, t, n_kv, group_heads):
    n_buf = cache_k.shape[1]
    bias = _sample_bias(rel_bias, group_heads, n_kv, t, n_buf)
    bias_c, bias_n = bias[:, :, :n_buf], bias[:, :, n_buf:]
    dq = q.shape[1]
    dkv = n_kv * HEAD_DIM
    return pl.pallas_call(
        functools.partial(_dilated_sample_kernel, n_kv=n_kv, group_heads=group_heads),
        grid=(b,),
        in_specs=[
            pl.BlockSpec((t, dq), lambda i: (i, 0)),
            pl.BlockSpec((None, n_buf, n_kv, HEAD_DIM), lambda i: (i, 0, 0, 0)),
            pl.BlockSpec((None, n_buf, n_kv, HEAD_DIM), lambda i: (i, 0, 0, 0)),
            pl.BlockSpec((t, dkv), lambda i: (i, 0)),
            pl.BlockSpec((t, dkv), lambda i: (i, 0)),
            pl.BlockSpec(bias_c.shape, lambda i: (0, 0, 0)),
            pl.BlockSpec(bias_n.shape, lambda i: (0, 0, 0)),
        ],
        out_specs=pl.BlockSpec((t, group_heads * HEAD_DIM), lambda i: (i, 0)),
        out_shape=jax.ShapeDtypeStruct((b * t, group_heads * HEAD_DIM), F32),
        compiler_params=_params("parallel"),
        name="dilated_sample_attention",
    )(q, cache_k, cache_v, k_new, v_new, bias_c, bias_n)


CONV_PAD_ROWS = 8


FFN_COL_TILE = 256
FFN_ROW_CHUNKS = 4


def _ffn_hidden_kernel(xp_ref, xs_ref, wu_ref, wg_ref, prev_ref, wc_ref, bc_ref, wd_ref, hp_ref, hs_ref, stp_ref,
                       sts_ref, wdb_ref, pad_ref, *, n_seq_s, t_s):
    i = pl.program_id(0)
    wdb_ref[...] = wd_ref[...].astype(BF16)
    t = xp_ref.shape[0]
    lo = CONV_PAD_ROWS - (CONV_W - 1)
    rc = t // FFN_ROW_CHUNKS
    wu = wu_ref[...].astype(BF16)
    wg = wg_ref[...].astype(BF16)

    def conv_gelu_gate(gate, start, rows):
        uc = bc_ref[...] + pad_ref[pl.ds(lo + start, rows), :] * wc_ref[0:1, :]
        for tap in range(1, CONV_W):
            uc = uc + pad_ref[pl.ds(lo + start + tap, rows), :] * wc_ref[tap:tap + 1, :]
        return 0.5 * uc * (1.0 + lax.erf(uc * SQRT_HALF)) * gate

    @pl.when(i == 0)
    def _():
        x = xs_ref[...]
        u = jnp.dot(x, wu, preferred_element_type=F32)
        gate = jnp.dot(x, wg, preferred_element_type=F32)
        pieces = []
        for sq in range(n_seq_s):
            rs = slice(sq * t_s, (sq + 1) * t_s)
            pad_ref[pl.ds(CONV_PAD_ROWS, t_s), :] = u[rs]
            pad_ref[pl.ds(lo, CONV_W - 1), :] = prev_ref[sq]
            pieces.append(conv_gelu_gate(gate[rs], 0, t_s))
            sts_ref[sq] = pad_ref[pl.ds(lo + t_s, CONV_W - 1), :]
        hs_ref[...] = jnp.concatenate(pieces, axis=0).astype(hs_ref.dtype)

    pad_ref[pl.ds(lo, CONV_W - 1), :] = jnp.zeros((CONV_W - 1, pad_ref.shape[1]), F32)
    gates = []
    for c in range(FFN_ROW_CHUNKS):
        x = xp_ref[c * rc:(c + 1) * rc, :]
        pad_ref[pl.ds(CONV_PAD_ROWS + c * rc, rc), :] = jnp.dot(x, wu, preferred_element_type=F32)
        gates.append(jnp.dot(x, wg, preferred_element_type=F32))
    for c in range(FFN_ROW_CHUNKS):
        hp_ref[c * rc:(c + 1) * rc, :] = conv_gelu_gate(gates[c], c * rc, rc).astype(hp_ref.dtype)
    stp_ref[...] = pad_ref[pl.ds(lo + t, CONV_W - 1), :]


def conv_ffn_hidden(xp, xs, w_up, w_gate, w_down, layer, prev_s, w_conv, b_conv, n_seq_p, t_p, n_seq_s, t_s):
    mp, d = xp.shape
    ms = xs.shape[0]
    f = w_up.shape[2]
    tn = FFN_COL_TILE
    assert mp == n_seq_p * t_p and ms == n_seq_s * t_s and f % tn == 0 and t_p % (FFN_ROW_CHUNKS * 128) == 0
    last = f // tn - 1
    n_col = f // tn
    assert f % (n_seq_p * n_col) == 0
    slab = f // (n_seq_p * n_col)
    wd_spec = lambda lead: pl.BlockSpec((None, slab, d), lambda i, j: (lead, i * n_col + j, 0))
    wspec = pl.BlockSpec((None, d, tn), lambda i, j: (layer, 0, j))
    s_col = lambda i, j: jnp.where(i == 0, j, last)
    return pl.pallas_call(
        functools.partial(_ffn_hidden_kernel, n_seq_s=n_seq_s, t_s=t_s),
        grid=(n_seq_p, f // tn),
        in_specs=[
            pl.BlockSpec((t_p, d), lambda i, j: (i, 0), pipeline_mode=pl.Buffered(1)),
            pl.BlockSpec((ms, d), lambda i, j: (0, 0)),
            wspec, wspec,
            pl.BlockSpec((None, n_seq_s, CONV_W - 1, tn), lambda i, j: (layer, 0, 0, s_col(i, j))),
            pl.BlockSpec((None, CONV_W, tn), lambda i, j: (layer, 0, j)),
            pl.BlockSpec((None, 1, tn), lambda i, j: (layer, 0, j)),
            wd_spec(layer),
        ],
        out_specs=[
            pl.BlockSpec((t_p, tn), lambda i, j: (i, j)),
            pl.BlockSpec((ms, tn), lambda i, j: (0, s_col(i, j))),
            pl.BlockSpec((None, CONV_W - 1, tn), lambda i, j: (i, 0, j)),
            pl.BlockSpec((n_seq_s, CONV_W - 1, tn), lambda i, j: (0, 0, s_col(i, j))),
            wd_spec(0),
        ],
        out_shape=[jax.ShapeDtypeStruct((mp, f), BF16), jax.ShapeDtypeStruct((ms, f), BF16),
                   jax.ShapeDtypeStruct((n_seq_p, CONV_W - 1, f), F32),
                   jax.ShapeDtypeStruct((n_seq_s, CONV_W - 1, f), F32),
                   jax.ShapeDtypeStruct((1, f, d), BF16)],
        scratch_shapes=[pltpu.VMEM((t_p + CONV_PAD_ROWS, tn), F32)],
        compiler_params=_params("arbitrary", "arbitrary"),
        name="conv_ffn_hidden",
    )(xp, xs, w_up, w_gate, prev_s, w_conv, b_conv.reshape(b_conv.shape[0], 1, f), w_down)


def _shift_append_kernel(old_ref, new_ref, o_ref):
    n_keep = old_ref.shape[1]
    o_ref[:n_keep] = old_ref[0]
    o_ref[n_keep:] = new_ref[...]


def shift_append(cache, new):
    b, n_buf, kv, dh = cache.shape
    t = new.shape[1]
    old_block = (pl.Element(1), pl.Element(n_buf - t), pl.Element(kv), pl.Element(dh))
    return pl.pallas_call(
        _shift_append_kernel,
        grid=(b,),
        in_specs=[pl.BlockSpec(old_block, lambda i: (i, t, 0, 0)),
                  pl.BlockSpec((None, t, kv, dh), lambda i: (i, 0, 0, 0))],
        out_specs=pl.BlockSpec((None, n_buf, kv, dh), lambda i: (i, 0, 0, 0)),
        out_shape=jax.ShapeDtypeStruct(cache.shape, cache.dtype),
        compiler_params=_params("parallel"),
        name="shift_append",
    )(cache, new)


def kernel(x_prompt, x_sample, cache_fox_k, cache_fox_v, cache_fox_logf, cache_win_k, cache_win_v, state_ffn_conv,
           page_table, w_in_a, b_f_a, w_o_a, g_attn, g_kv, w_kv_b, w_q_b, w_o_b, rel_bias,
           g_ffn, w_up, w_gate, w_conv, b_conv, w_down, g_final):
    bp, sp, d_model = x_prompt.shape
    bs, ts, _ = x_sample.shape
    n_heads = b_f_a.shape[1]
    hd = n_heads * HEAD_DIM
    n_kv = cache_win_k.shape[2]
    kvw = n_kv * HEAD_DIM
    group_heads = w_q_b.shape[2] // (N_GROUPS * HEAD_DIM)
    d_ff = w_up.shape[2]
    n_pages = page_table.shape[1]
    assert w_in_a.shape[0] == 1 and w_q_b.shape[0] == 1 and g_attn.shape[0] == 2
    mp, ms = bp * sp, bs * ts

    b_f = b_f_a[0][None].astype(F32)
    w_in_t = jnp.swapaxes(w_in_a, 1, 2)
    xp = x_prompt.reshape(mp, d_model)
    xs = x_sample.reshape(ms, d_model)

    def norm(x, gains, dtype=BF16):
        return rmsnorm(x, gains, dtype)

    def ffn(xp, xs, layer):
        (xnp,) = norm(xp, g_ffn[layer][None])
        (xns,) = norm(xs, g_ffn[layer][None])
        hp, hs, stp, sts, w_down_b = conv_ffn_hidden(xnp, xns, w_up, w_gate, w_down, layer, state_ffn_conv, w_conv,
                                                     b_conv, bp, sp, bs, ts)
        xp = matmul(hp, w_down_b, 0, res=xp, bm=512, bn=512)
        xs = matmul(hs, w_down_b, 0, res=xs, bm=ms, bn=1024, bk=d_ff // 2)
        return xp, xs, stp, sts

    (xnp,) = norm(xp, g_attn[0][None])
    (xns,) = norm(xs, g_attn[0][None])
    qp, qs = matmul_pair(xnp, xns, w_in_t, 0, 0, hd, out_dtype=BF16, out_scale=ATTN_SCALE, transposed=True)
    kp, ks = matmul_pair(xnp, xns, w_in_t, 0, hd, hd, transposed=True)
    vp, vs = matmul_pair(xnp, xns, w_in_t, 0, 2 * hd, hd, transposed=True)
    lfp = forget_gate(xnp, w_in_t, 0, 3 * hd, b_f)
    lfs = forget_gate(xns, w_in_t, 0, 3 * hd, b_f)
    cp = cumsum_rows(lfp.reshape(bp, sp, n_heads))
    cs = cumsum_rows(lfs.reshape(bs, ts, n_heads))
    pt_flat = page_table.reshape(-1)
    d_past = page_suffix_sums(pt_flat, cache_fox_logf[0], bs, n_pages)
    ap, a_s = fox_attention(qp, kp, vp, cp.reshape(mp, n_heads), bp, sp, qs, cs, d_past, ks, vs, cache_fox_k[0],
                            cache_fox_v[0], pt_flat, bs, ts, n_heads, n_pages)
    a_s = a_s.astype(BF16)
    xp, xs = matmul_pair(ap, a_s, w_o_a, 0, 0, d_model, res=(xp, xs))
    xp, xs, st0p, st0s = ffn(xp, xs, 0)

    xnp, xkvp = norm(xp, jnp.stack([g_attn[1], g_kv]))
    xns, xkvs = norm(xs, jnp.stack([g_attn[1], g_kv]))
    kvp, kvs = matmul_pair(xkvp, xkvs, w_kv_b[None], 0, 0, 2 * kvw)
    qp, qs = matmul_pair(xnp, xns, w_q_b, 0, 0, w_q_b.shape[2], out_scale=ATTN_SCALE)
    kbp, vbp = kvp[:, :kvw], kvp[:, kvw:]
    kbs, vbs = kvs[:, :kvw], kvs[:, kvw:]
    ap = dilated_prompt_attention(qp, kbp, vbp, rel_bias, bp, sp, n_kv, group_heads)
    a_s = dilated_sample_attention(qs, kbs, vbs, cache_win_k, cache_win_v, rel_bias, bs, ts, n_kv,
                                   group_heads).astype(BF16)
    xp, xs = matmul_pair(ap, a_s, w_o_b, 0, 0, d_model, res=(xp, xs))
    xp, xs, st1p, st1s = ffn(xp, xs, 1)
    (yp,) = norm(xp, g_final[None], F32)
    (ys,) = norm(xs, g_final[None], F32)

    n_pg = sp // PAGE_SIZE
    keep = min(BUCKET_MAX_DIST, sp)
    return (
        yp.reshape(bp, sp, d_model),
        ys.reshape(bs, ts, d_model),
        kp.reshape(1, bp, n_pg, PAGE_SIZE, n_heads, HEAD_DIM),
        vp.reshape(1, bp, n_pg, PAGE_SIZE, n_heads, HEAD_DIM),
        lfp.reshape(1, bp, n_pg, PAGE_SIZE, n_heads),
        ks.reshape(1, bs, ts, n_heads, HEAD_DIM),
        vs.reshape(1, bs, ts, n_heads, HEAD_DIM),
        lfs.reshape(1, bs, ts, n_heads),
        kbp.reshape(bp, sp, n_kv, HEAD_DIM)[:, sp - keep:],
        vbp.reshape(bp, sp, n_kv, HEAD_DIM)[:, sp - keep:],
        shift_append(cache_win_k, kbs.reshape(bs, ts, n_kv, HEAD_DIM)),
        shift_append(cache_win_v, vbs.reshape(bs, ts, n_kv, HEAD_DIM)),
        jnp.stack([st0p, st1p]),
        jnp.stack([st0s, st1s]),
    )
```

```python
import functools
import math

import numpy as np
import jax
import jax.numpy as jnp
from jax import lax
from jax.experimental import pallas as pl
from jax.experimental.pallas import tpu as pltpu

F32 = jnp.float32
BF16 = jnp.bfloat16

HEAD_DIM = 128
PAGE_SIZE = 128
PATTERNS = ((128, 1), (512, 4), (2048, 16))
N_GROUPS = len(PATTERNS)
N_BUCKETS = 32
BUCKET_MAX_DIST = max(w for w, _ in PATTERNS)
CONV_W = 3
Q_BLOCK = 128
NORM_EPS = 1e-6
ATTN_SCALE = HEAD_DIM ** -0.5
NEG = -1e30
SQRT_HALF = 0.7071067811865476

V7X_VMEM_LIMIT_BYTES = 56 * 1024 * 1024


def _params(*semantics):
    return pltpu.CompilerParams(dimension_semantics=semantics, vmem_limit_bytes=V7X_VMEM_LIMIT_BYTES)


def _rmsnorm_kernel(x_ref, g_ref, *o_refs):
    x = x_ref[...]
    y = x * lax.rsqrt(jnp.mean(x * x, axis=-1, keepdims=True) + NORM_EPS)
    for j, o_ref in enumerate(o_refs):
        o_ref[...] = (y * g_ref[j:j + 1, :]).astype(o_ref.dtype)


def rmsnorm(x, gains, out_dtype):
    m, d = x.shape
    n = gains.shape[0]
    tm = min(256, m)
    return pl.pallas_call(
        _rmsnorm_kernel,
        grid=(m // tm,),
        in_specs=[pl.BlockSpec((tm, d), lambda i: (i, 0)), pl.BlockSpec((n, d), lambda i: (0, 0))],
        out_specs=[pl.BlockSpec((tm, d), lambda i: (i, 0))] * n,
        out_shape=[jax.ShapeDtypeStruct((m, d), out_dtype)] * n,
        compiler_params=_params("parallel"),
        name="rmsnorm",
    )(x, gains)


def _matmul_kernel(*refs, has_res, nk, out_scale):
    x_ref, w_ref = refs[0], refs[1]
    res_ref = refs[2] if has_res else None
    o_ref = refs[2 + has_res]

    def finish(acc):
        if out_scale != 1.0:
            acc = acc * out_scale
        if has_res:
            acc = acc + res_ref[...]
        o_ref[...] = acc.astype(o_ref.dtype)

    part = jnp.dot(x_ref[...], w_ref[...], preferred_element_type=F32)
    if nk == 1:
        finish(part)
        return
    acc_ref = refs[3 + has_res]
    k = pl.program_id(2)

    @pl.when(k == 0)
    def _():
        acc_ref[...] = part

    @pl.when(jnp.logical_and(k > 0, k < nk - 1))
    def _():
        acc_ref[...] += part

    @pl.when(k == nk - 1)
    def _():
        finish(acc_ref[...] + part)


def matmul(x, w, layer, *, res=None, out_dtype=F32, out_scale=1.0, bm=1024, bn=1024, bk=None):
    m, kdim = x.shape
    n = w.shape[2]
    bm, bn = min(bm, m), min(bn, n)
    bk = kdim if bk is None else bk
    nk = kdim // bk
    assert m % bm == 0 and n % bn == 0 and kdim % bk == 0 and nk >= 1
    in_specs = [pl.BlockSpec((bm, bk), lambda i, j, k: (i, k)),
                pl.BlockSpec((None, bk, bn), lambda i, j, k: (layer, k, j))]
    args = [x, w]
    if res is not None:
        in_specs.append(pl.BlockSpec((bm, bn), lambda i, j, k: (i, j)))
        args.append(res)
    return pl.pallas_call(
        functools.partial(_matmul_kernel, has_res=res is not None, nk=nk, out_scale=out_scale),
        grid=(m // bm, n // bn, nk),
        in_specs=in_specs,
        out_specs=pl.BlockSpec((bm, bn), lambda i, j, k: (i, j)),
        out_shape=jax.ShapeDtypeStruct((m, n), out_dtype),
        scratch_shapes=[pltpu.VMEM((bm, bn), F32)] if nk > 1 else [],
        compiler_params=_params("parallel", "parallel", "arbitrary"),
        name="matmul",
    )(*args)


def _pair_matmul_kernel(*refs, nm, has_res, out_scale, transposed):
    if has_res:
        xp_ref, xs_ref, w_ref, rp_ref, rs_ref, op_ref, os_ref, wb_ref = refs
    else:
        xp_ref, xs_ref, w_ref, op_ref, os_ref, wb_ref = refs
        rp_ref = rs_ref = None
    i = pl.program_id(1)

    @pl.when(i == 0)
    def _():
        wb_ref[...] = w_ref[...].astype(BF16)

    def emit(x_ref, r_ref, o_ref):
        dims = (((1,), (1 if transposed else 0,)), ((), ()))
        acc = lax.dot_general(x_ref[...], wb_ref[...], dims, preferred_element_type=F32)
        if out_scale != 1.0:
            acc = acc * out_scale
        if r_ref is not None:
            acc = acc + r_ref[...]
        o_ref[...] = acc.astype(o_ref.dtype)

    @pl.when(i < nm)
    def _():
        emit(xp_ref, rp_ref, op_ref)

    @pl.when(i == nm)
    def _():
        emit(xs_ref, rs_ref, os_ref)


def matmul_pair(xp, xs, w, layer, col0, n, *, res=None, out_dtype=F32, out_scale=1.0, transposed=False,
                bm=256, bn=1024):
    mp, kdim = xp.shape
    ms = xs.shape[0]
    assert mp % bm == 0 and n % bn == 0 and col0 % bn == 0 and w.shape[2 if transposed else 1] == kdim
    nm, cb0, last = mp // bm, col0 // bn, mp // bm - 1
    row = lambda j, i: (jnp.minimum(i, last), 0)
    tile = lambda j, i: (jnp.minimum(i, last), j)
    if transposed:
        w_spec = pl.BlockSpec((None, bn, kdim), lambda j, i: (layer, cb0 + j, 0))
    else:
        w_spec = pl.BlockSpec((None, kdim, bn), lambda j, i: (layer, 0, cb0 + j))
    in_specs = [pl.BlockSpec((bm, kdim), row), pl.BlockSpec((ms, kdim), lambda j, i: (0, 0)), w_spec]
    args = [xp, xs, w]
    if res is not None:
        in_specs += [pl.BlockSpec((bm, bn), tile), pl.BlockSpec((ms, bn), lambda j, i: (0, j))]
        args += list(res)
    return pl.pallas_call(
        functools.partial(_pair_matmul_kernel, nm=nm, has_res=res is not None, out_scale=out_scale,
                          transposed=transposed),
        grid=(n // bn, nm + 1),
        in_specs=in_specs,
        out_specs=[pl.BlockSpec((bm, bn), tile), pl.BlockSpec((ms, bn), lambda j, i: (0, j))],
        out_shape=[jax.ShapeDtypeStruct((mp, n), out_dtype), jax.ShapeDtypeStruct((ms, n), out_dtype)],
        scratch_shapes=[pltpu.VMEM((bn, kdim) if transposed else (kdim, bn), BF16)],
        compiler_params=_params("parallel", "arbitrary"),
        name="matmul_pair",
    )(*args)


def _gate_kernel(x_ref, w_ref, b_ref, o_ref):
    z = lax.dot_general(x_ref[...], w_ref[...].astype(BF16), (((1,), (1,)), ((), ())),
                        preferred_element_type=F32) + b_ref[...]
    o_ref[...] = -(jnp.maximum(-z, 0.0) + jnp.log1p(jnp.exp(-jnp.abs(z))))


def forget_gate(xn, w_t, layer, row0, b_f):
    m, d = xn.shape
    nh = b_f.shape[1]
    tm = min(512, m)
    assert row0 % nh == 0 and nh % 8 == 0
    return pl.pallas_call(
        _gate_kernel,
        grid=(m // tm,),
        in_specs=[pl.BlockSpec((tm, d), lambda i: (i, 0)),
                  pl.BlockSpec((None, nh, d), lambda i: (layer, row0 // nh, 0)),
                  pl.BlockSpec((1, nh), lambda i: (0, 0))],
        out_specs=pl.BlockSpec((tm, nh), lambda i: (i, 0)),
        out_shape=jax.ShapeDtypeStruct((m, nh), F32),
        compiler_params=_params("parallel"),
        name="forget_gate",
    )(xn, w_t, b_f)


def _split_dot(tri, x):
    hi = x.astype(BF16)
    r1 = x - hi.astype(F32)
    mid = r1.astype(BF16)
    lo = (r1 - mid.astype(F32)).astype(BF16)
    out = jnp.dot(tri, lo, preferred_element_type=F32)
    out = out + jnp.dot(tri, mid, preferred_element_type=F32)
    return out + jnp.dot(tri, hi, preferred_element_type=F32)


def _cumsum_kernel(x_ref, o_ref, *, chunk):
    s = x_ref.shape[0]
    row = lax.broadcasted_iota(jnp.int32, (chunk, chunk), 0)
    col = lax.broadcasted_iota(jnp.int32, (chunk, chunk), 1)
    tri = jnp.where(col <= row, 1.0, 0.0).astype(BF16)
    carry = jnp.zeros((1, x_ref.shape[1]), F32)
    for i in range(s // chunk):
        y = _split_dot(tri, x_ref[i * chunk:(i + 1) * chunk, :]) + carry
        o_ref[i * chunk:(i + 1) * chunk, :] = y
        carry = y[chunk - 1:chunk, :]


def _cumsum_small_kernel(x_ref, o_ref):
    x = x_ref[...]
    t = x.shape[0]
    row = lax.broadcasted_iota(jnp.int32, x.shape, 0)
    for i in range(t):
        o_ref[i:i + 1, :] = jnp.sum(jnp.where(row <= i, x, 0.0), axis=0, keepdims=True)


def cumsum_rows(x):
    b, s, nh = x.shape
    body = _cumsum_small_kernel if s < 128 else functools.partial(_cumsum_kernel, chunk=min(256, s))
    return pl.pallas_call(
        body,
        grid=(b,),
        in_specs=[pl.BlockSpec((None, s, nh), lambda i: (i, 0, 0))],
        out_specs=pl.BlockSpec((None, s, nh), lambda i: (i, 0, 0)),
        out_shape=jax.ShapeDtypeStruct((b, s, nh), F32),
        compiler_params=_params("parallel"),
        name="cumsum_rows",
    )(x)


SUFFIX_PAGES_PER_STEP = 16


def _page_suffix_kernel(pt_ref, *refs, pps):
    lf_refs, o_ref, carry_ref = refs[:pps], refs[pps], refs[pps + 1]
    j = pl.program_id(1)
    n = lf_refs[0].shape[0]

    @pl.when(j == 0)
    def _():
        carry_ref[...] = jnp.zeros_like(carry_ref)

    row = lax.broadcasted_iota(jnp.int32, (n, n), 0)
    col = lax.broadcasted_iota(jnp.int32, (n, n), 1)
    upper = jnp.where(col > row, 1.0, 0.0).astype(BF16)
    carry = carry_ref[...]
    for pp in reversed(range(pps)):
        x = lf_refs[pp][...]
        o_ref[pp] = _split_dot(upper, x) + carry
        carry = carry + jnp.sum(x, axis=0, keepdims=True)
    carry_ref[...] = carry


def page_suffix_sums(page_table_flat, lf_pool, b, n_pages):
    _, n, nh = lf_pool.shape
    pps = SUFFIX_PAGES_PER_STEP
    assert n_pages % pps == 0
    n_steps = n_pages // pps

    def page_spec(pp):
        return pl.BlockSpec((None, n, nh), lambda i, j, pt: (pt[i * n_pages + n_pages - (j + 1) * pps + pp], 0, 0))

    grid_spec = pltpu.PrefetchScalarGridSpec(
        num_scalar_prefetch=1,
        grid=(b, n_steps),
        in_specs=[page_spec(pp) for pp in range(pps)],
        out_specs=pl.BlockSpec((None, pps, n, nh), lambda i, j, pt: (i, n_steps - 1 - j, 0, 0)),
        scratch_shapes=[pltpu.VMEM((1, nh), F32)],
    )
    return pl.pallas_call(
        functools.partial(_page_suffix_kernel, pps=pps),
        grid_spec=grid_spec,
        out_shape=jax.ShapeDtypeStruct((b, n_pages, n, nh), F32),
        compiler_params=_params("parallel", "arbitrary"),
        name="page_suffix_sums",
    )(page_table_flat, *([lf_pool] * pps))


HEADS_PER_TILE = 8
FOX_PAGES_PER_STEP = 2
FOX_PROMPT_TQ = 512


def _tile_rows(ref, sub):
    rows, n_sub, dh = ref.shape
    return ref.reshape(rows * n_sub, dh)[pl.ds(sub, rows, stride=n_sub), :]


def _fox_prompt_block(q_ref, k_ref, v_ref, c_ref, ct_ref, o_ref, h, ii):
    tq = q_ref.shape[0]
    q = q_ref[...]
    lane = lax.broadcasted_iota(jnp.int32, c_ref.shape, 1)
    cq = jnp.sum(jnp.where(lane == h, c_ref[...], 0.0), axis=-1, keepdims=True)
    dn = (((1,), (1,)), ((), ()))

    def scores(j):
        kb = k_ref[j * tq:(j + 1) * tq, :].astype(BF16)
        return lax.dot_general(q, kb, dn, preferred_element_type=F32) + (cq - ct_ref[j:j + 1, :])

    def update(j, s, carry, masked):
        m, l, acc = carry
        if masked:
            row = lax.broadcasted_iota(jnp.int32, s.shape, 0)
            col = lax.broadcasted_iota(jnp.int32, s.shape, 1)
            s = jnp.where(col <= row, s, NEG)
        m_new = jnp.maximum(m, jnp.max(s, axis=-1, keepdims=True))
        alpha = jnp.exp(m - m_new)
        p = jnp.exp(s - m_new)
        l = l * alpha + jnp.sum(p, axis=-1, keepdims=True)
        vb = v_ref[j * tq:(j + 1) * tq, :].astype(BF16)
        acc = acc * alpha + jnp.dot(p.astype(BF16), vb, preferred_element_type=F32)
        return m_new, l, acc

    carry = (jnp.full((tq, 1), NEG, F32), jnp.zeros((tq, 1), F32), jnp.zeros((tq, HEAD_DIM), F32))
    s_next = scores(0)
    for j in range(ii + 1):
        s_cur = s_next
        if j < ii:
            s_next = scores(j + 1)
        carry = update(j, s_cur, carry, masked=(j == ii))
    _, l, acc = carry
    o_ref[...] = (acc / l).astype(o_ref.dtype)


def _fox_online_update(m_ref, l_ref, acc_ref, s, pv):
    m_old = m_ref[...]
    m_new = jnp.maximum(m_old, jnp.max(s, axis=-1, keepdims=True))
    alpha = jnp.exp(m_old - m_new)
    p = jnp.exp(s - m_new)
    l_ref[...] = l_ref[...] * alpha + jnp.sum(p, axis=-1, keepdims=True)
    acc_ref[...] = acc_ref[...] * alpha + pv(p)
    m_ref[...] = m_new


def _fox_sample_pages(q_ref, cn_ref, dt_ref, k_refs, v_refs, m_ref, l_ref, acc_ref, n_heads, pps):
    ng = n_heads // HEADS_PER_TILE
    t = q_ref.shape[1]
    dn = (((1,), (1,)), ((), ()))

    def head_rows(page_refs, h):
        g, hs = divmod(h, HEADS_PER_TILE)
        tiles = [_tile_rows(page_refs[pp * ng + g], hs).astype(BF16) for pp in range(pps)]
        return tiles[0] if pps == 1 else jnp.concatenate(tiles, axis=0)

    parts = []
    for h in range(n_heads):
        sh = lax.dot_general(q_ref[h], head_rows(k_refs, h), dn, preferred_element_type=F32)
        d_row = [dt_ref[pp, h:h + 1, :] for pp in range(pps)]
        parts.append(sh + (d_row[0] if pps == 1 else jnp.concatenate(d_row, axis=1)))
    cn = cn_ref[...]
    s = jnp.concatenate(parts, axis=0) + (cn if pps == 1 else jnp.concatenate([cn] * pps, axis=1))

    def pv_pages(p):
        return jnp.concatenate(
            [jnp.dot(p[h * t:(h + 1) * t].astype(BF16), head_rows(v_refs, h), preferred_element_type=F32)
             for h in range(n_heads)], axis=0)

    _fox_online_update(m_ref, l_ref, acc_ref, s, pv_pages)


def _fox_sample_new_rows(q_ref, cn_ref, cnt_ref, kn_ref, vn_ref, o_ref, m_ref, l_ref, acc_ref, n_heads):
    t = q_ref.shape[1]
    dn = (((1,), (1,)), ((), ()))
    parts = []
    for h in range(n_heads):
        kh = kn_ref[:, h * HEAD_DIM:(h + 1) * HEAD_DIM].astype(BF16)
        parts.append(lax.dot_general(q_ref[h], kh, dn, preferred_element_type=F32))
    s2 = jnp.concatenate(parts, axis=0) + cn_ref[:, :t] - cnt_ref[...]
    row = lax.broadcasted_iota(jnp.int32, s2.shape, 0)
    col = lax.broadcasted_iota(jnp.int32, s2.shape, 1)
    s2 = jnp.where(col <= lax.rem(row, t), s2, NEG)

    def pv_new(p):
        return jnp.concatenate(
            [jnp.dot(p[h * t:(h + 1) * t], vn_ref[:, h * HEAD_DIM:(h + 1) * HEAD_DIM], preferred_element_type=F32)
             for h in range(n_heads)], axis=0)

    _fox_online_update(m_ref, l_ref, acc_ref, s2, pv_new)
    o_ref[...] = acc_ref[...] / l_ref[...]


def _fox_kernel(pt_ref, qp_ref, kp_ref, vp_ref, c_ref, ct_ref, qs_ref, cn_ref, cnt_ref, dt_ref, kn_ref, vn_ref,
                *refs, nq, n_heads, n_steps, pps):
    nb = n_heads // HEADS_PER_TILE * pps
    k_refs, v_refs = refs[:nb], refs[nb:2 * nb]
    op_ref, os_ref, m_ref, l_ref, acc_ref = refs[2 * nb:]
    step = pl.program_id(0)
    i = lax.rem(step, nq)
    h = lax.rem(step // nq, n_heads)
    js = lax.rem(step, n_steps)

    @pl.when(js == 0)
    def _():
        m_ref[...] = jnp.full_like(m_ref, NEG)
        l_ref[...] = jnp.zeros_like(l_ref)
        acc_ref[...] = jnp.zeros_like(acc_ref)

    for ii in range(nq):

        @pl.when(i == ii)
        def _(ii=ii):
            _fox_sample_pages(qs_ref, cn_ref, dt_ref, k_refs, v_refs, m_ref, l_ref, acc_ref, n_heads, pps)
            _fox_prompt_block(qp_ref, kp_ref, vp_ref, c_ref, ct_ref, op_ref, h, ii)

    @pl.when(js == n_steps - 1)
    def _():
        _fox_sample_new_rows(qs_ref, cn_ref, cnt_ref, kn_ref, vn_ref, os_ref, m_ref, l_ref, acc_ref, n_heads)


def fox_attention(qp, kp, vp, cp, bp, sp, qs, c_new, d_past, k_new, v_new, k_pool, v_pool, page_table_flat,
                  bs, ts, n_heads, n_pages):
    tq = FOX_PROMPT_TQ
    nq = sp // tq
    ng = n_heads // HEADS_PER_TILE
    pps = FOX_PAGES_PER_STEP
    n_steps = n_pages // pps
    assert n_pages % pps == 0 and bp * n_heads * nq == bs * n_steps, "the two groups must have equally many steps"
    d = n_heads * HEAD_DIM
    ht = n_heads * ts
    n_pool = k_pool.shape[0]

    ct = cp.reshape(bp, sp, n_heads).transpose(0, 2, 1).reshape(bp * n_heads, nq, tq)
    q4 = qs.reshape(bs, ts, n_heads, HEAD_DIM).transpose(0, 2, 1, 3)
    c_ht = c_new.transpose(0, 2, 1).reshape(bs, ht, 1)
    cn = jnp.broadcast_to(c_ht, (bs, ht, PAGE_SIZE))
    cnt = jnp.broadcast_to(c_new.transpose(0, 2, 1)[:, :, None, :], (bs, n_heads, ts, ts)).reshape(bs, ht, ts)
    dt = d_past.transpose(0, 1, 3, 2)
    kpool = k_pool.reshape(n_pool, PAGE_SIZE, ng, HEADS_PER_TILE, HEAD_DIM)
    vpool = v_pool.reshape(n_pool, PAGE_SIZE, ng, HEADS_PER_TILE, HEAD_DIM)

    def p_row(s, pt):
        return (s // (n_heads * nq) * nq + lax.rem(s, nq), lax.rem(s // nq, n_heads))

    def p_head(s, pt):
        return (s // (n_heads * nq), lax.rem(s // nq, n_heads))

    def page_spec(pp, g):
        return pl.BlockSpec((None, PAGE_SIZE, None, HEADS_PER_TILE, HEAD_DIM),
                            lambda s, pt: (pt[s // n_steps * n_pages + lax.rem(s, n_steps) * pps + pp], 0, g, 0, 0))

    page_specs = [page_spec(pp, g) for pp in range(pps) for g in range(ng)]
    sb = lambda s, pt: s // n_steps
    grid_spec = pltpu.PrefetchScalarGridSpec(
        num_scalar_prefetch=1,
        grid=(bs * n_steps,),
        in_specs=[
            pl.BlockSpec((tq, HEAD_DIM), p_row),
            pl.BlockSpec((sp, HEAD_DIM), p_head),
            pl.BlockSpec((sp, HEAD_DIM), p_head),
            pl.BlockSpec((tq, n_heads), lambda s, pt: (p_row(s, pt)[0], 0)),
            pl.BlockSpec((None, nq, tq), lambda s, pt: (s // nq, 0, 0)),
            pl.BlockSpec((None, n_heads, ts, HEAD_DIM), lambda s, pt: (sb(s, pt), 0, 0, 0)),
            pl.BlockSpec((None, ht, PAGE_SIZE), lambda s, pt: (sb(s, pt), 0, 0)),
            pl.BlockSpec((None, ht, ts), lambda s, pt: (sb(s, pt), 0, 0)),
            pl.BlockSpec((None, pps, n_heads, PAGE_SIZE), lambda s, pt: (sb(s, pt), lax.rem(s, n_steps), 0, 0)),
            pl.BlockSpec((ts, d), lambda s, pt: (sb(s, pt), 0)),
            pl.BlockSpec((ts, d), lambda s, pt: (sb(s, pt), 0)),
        ] + page_specs * 2,
        out_specs=[pl.BlockSpec((tq, HEAD_DIM), p_row),
                   pl.BlockSpec((None, ht, HEAD_DIM), lambda s, pt: (sb(s, pt), 0, 0))],
        scratch_shapes=[pltpu.VMEM((ht, 1), F32), pltpu.VMEM((ht, 1), F32), pltpu.VMEM((ht, HEAD_DIM), F32)],
    )
    op, os_ = pl.pallas_call(
        functools.partial(_fox_kernel, nq=nq, n_heads=n_heads, n_steps=n_steps, pps=pps),
        grid_spec=grid_spec,
        out_shape=[jax.ShapeDtypeStruct(qp.shape, BF16), jax.ShapeDtypeStruct((bs, ht, HEAD_DIM), F32)],
        compiler_params=_params("arbitrary"),
        name="fox_attention",
    )(page_table_flat, qp, kp, vp, cp, ct, q4, cn, cnt, dt, k_new, v_new,
      *([kpool] * (ng * pps)), *([vpool] * (ng * pps)))
    return op, os_.reshape(bs, n_heads, ts, HEAD_DIM).transpose(0, 2, 1, 3).reshape(bs * ts, d)


def _t5_bucket(dist):
    n = np.asarray(dist, dtype=np.int64)
    exact = N_BUCKETS // 2
    large = exact + (np.log(np.maximum(n, 1) / exact) / np.log(BUCKET_MAX_DIST / exact) * (N_BUCKETS - exact)).astype(np.int64)
    return np.where(n < exact, n, np.minimum(large, N_BUCKETS - 1)).astype(np.int32)


def _band_bias(rel_bias, group_heads):
    a = np.arange(Q_BLOCK)[:, None]
    bcol = np.arange(2 * Q_BLOCK)[None, :]
    j = a - bcol + Q_BLOCK
    out = []
    for g, (window, dil) in enumerate(PATTERNS):
        assert window // dil == Q_BLOCK
        valid = (j >= 0) & (j <= window // dil)
        bucket = _t5_bucket(dil * np.clip(j, 0, window // dil))
        bias = _select_buckets(rel_bias[:, g * group_heads:(g + 1) * group_heads], bucket)
        out.append(jnp.where(valid[None], bias, NEG))
    return jnp.stack(out)


def _select_buckets(rb, bucket):
    onehot = (jnp.asarray(bucket.reshape(-1))[None, :] == jnp.arange(N_BUCKETS)[:, None]).astype(F32)
    sel = jnp.einsum("bh,bn->hn", rb.astype(F32), onehot, precision=lax.Precision.HIGHEST)
    return sel.reshape((rb.shape[1],) + bucket.shape)


DILATED_BLOCKS_PER_STEP = 4


def _dilated_prompt_kernel(*refs, s):
    n_q = N_GROUPS * 2
    q_refs = refs[:n_q]
    k_ref, v_ref, bias_ref, o_ref, m_ref, l_ref, acc_ref = refs[n_q:]
    qb_rows = Q_BLOCK
    dn = (((1,), (1,)), ((), ()))

    def blocks(g, specs, merge):
        bias = bias_ref[g].reshape(2 * qb_rows, 2 * qb_rows)
        scores = []
        for q_idx, k_idx, first in specs:
            q2 = jnp.concatenate([q_refs[2 * g + r][q_idx, :].astype(BF16) for r in range(2)], axis=0)
            sc = lax.dot_general(q2, k_ref[k_idx, :].astype(BF16), dn, preferred_element_type=F32)
            scores.append(sc + (bias[:, qb_rows:] if first else bias))
        stats = []
        for sc in scores:
            m_blk = jnp.max(sc, axis=-1, keepdims=True)
            p = jnp.exp(sc - m_blk)
            stats.append((m_blk, jnp.sum(p, axis=-1, keepdims=True), p.astype(BF16)))
        outs = [jnp.dot(p, v_ref[k_idx, :].astype(BF16), preferred_element_type=F32)
                for (_, _, p), (_, k_idx, _) in zip(stats, specs)]
        for (m_blk, l_blk, _), o_blk, (q_idx, _, _) in zip(stats, outs, specs):
            for r in range(2):
                rs = slice(r * qb_rows, (r + 1) * qb_rows)
                m_b = jnp.broadcast_to(m_blk[rs], (qb_rows, HEAD_DIM))
                l_b = jnp.broadcast_to(l_blk[rs], (qb_rows, HEAD_DIM))
                if merge:
                    m_old = m_ref[r, q_idx, :]
                    m_new = jnp.maximum(m_old, m_b)
                    a_old = jnp.exp(m_old - m_new)
                    a_blk = jnp.exp(m_b - m_new)
                    l_ref[r, q_idx, :] = l_ref[r, q_idx, :] * a_old + l_b * a_blk
                    acc_ref[r, q_idx, :] = acc_ref[r, q_idx, :] * a_old + o_blk[rs] * a_blk
                    m_ref[r, q_idx, :] = m_new
                else:
                    m_ref[r, q_idx, :] = m_b
                    l_ref[r, q_idx, :] = l_b
                    acc_ref[r, q_idx, :] = o_blk[rs]

    per_step = DILATED_BLOCKS_PER_STEP
    order = sorted(range(N_GROUPS), key=lambda g: -PATTERNS[g][1])
    for g in order:
        dil = PATTERNS[g][1]
        n_blocks = s // (dil * qb_rows)
        merge = g != order[0]

        def idx(start, size, dil=dil):
            return pl.ds(start, size) if dil == 1 else pl.ds(start, size, stride=dil)

        def spec(c, qb, dil=dil, idx=idx):
            q0 = c + dil * qb_rows * qb
            if isinstance(qb, int) and qb == 0:
                return idx(q0, qb_rows), idx(q0, qb_rows), True
            return idx(q0, qb_rows), idx(q0 - dil * qb_rows, 2 * qb_rows), False

        if n_blocks == 1:
            assert dil % per_step == 0

            def classes(it, carry, g=g, merge=merge, spec=spec):
                blocks(g, [spec(it * per_step + u, 0) for u in range(per_step)], merge)
                return carry

            lax.fori_loop(0, dil // per_step, classes, 0)
        elif n_blocks <= per_step:

            def one_class(c, carry, g=g, merge=merge, spec=spec, n_blocks=n_blocks):
                blocks(g, [spec(c, qb) for qb in range(n_blocks)], merge)
                return carry

            lax.fori_loop(0, dil, one_class, 0)
        else:
            assert dil == 1 and (n_blocks - 1) % (per_step - 1) == 0
            blocks(g, [spec(0, 0)], merge)
            grp = per_step - 1

            def later(it, carry, g=g, merge=merge, spec=spec, grp=grp):
                blocks(g, [spec(0, 1 + it * grp + u) for u in range(grp)], merge)
                return carry

            lax.fori_loop(0, (n_blocks - 1) // grp, later, 0)

    for r in range(2):
        o_ref[:, r * HEAD_DIM:(r + 1) * HEAD_DIM] = (acc_ref[r] / l_ref[r]).astype(o_ref.dtype)


def dilated_prompt_attention(q, k, v, rel_bias, b, s, n_kv, group_heads):
    assert group_heads == 2 * n_kv
    bias = _band_bias(rel_bias, group_heads).reshape(N_GROUPS, n_kv, 2, Q_BLOCK, 2 * Q_BLOCK)

    def q_spec(g, r):
        return pl.BlockSpec((s, HEAD_DIM), lambda bi, kv: (bi, g * group_heads + 2 * kv + r))

    return pl.pallas_call(
        functools.partial(_dilated_prompt_kernel, s=s),
        grid=(b, n_kv),
        in_specs=[q_spec(g, r) for g in range(N_GROUPS) for r in range(2)] + [
            pl.BlockSpec((s, HEAD_DIM), lambda bi, kv: (bi, kv)),
            pl.BlockSpec((s, HEAD_DIM), lambda bi, kv: (bi, kv)),
            pl.BlockSpec((N_GROUPS, None, 2, Q_BLOCK, 2 * Q_BLOCK), lambda bi, kv: (0, kv, 0, 0, 0)),
        ],
        out_specs=pl.BlockSpec((s, 2 * HEAD_DIM), lambda bi, kv: (bi, kv)),
        out_shape=jax.ShapeDtypeStruct((b * s, group_heads * HEAD_DIM), BF16),
        scratch_shapes=[pltpu.VMEM((2, s, HEAD_DIM), F32)] * 3,
        compiler_params=_params("parallel", "parallel"),
        name="dilated_prompt_attention",
    )(*([q] * (N_GROUPS * 2)), k, v, bias)


def _sample_bias(rel_bias, group_heads, n_kv, t, n_buf):
    dist = n_buf + np.arange(t)[:, None] - np.arange(n_buf + t)[None, :]
    out = []
    for g, (window, dil) in enumerate(PATTERNS):
        valid = (dist >= 0) & (dist % dil == 0) & (dist <= window)
        bucket = _t5_bucket(np.clip(dist, 0, window))
        bias = _select_buckets(rel_bias[:, g * group_heads:(g + 1) * group_heads], bucket)
        out.append(jnp.where(valid[None], bias, NEG).reshape(n_kv, 2, t, n_buf + t))
    return jnp.stack(out, axis=1).reshape(n_kv, N_GROUPS * 2 * t, n_buf + t)


def _dilated_sample_kernel(q_ref, kc_ref, vc_ref, kn_ref, vn_ref, bc_ref, bn_ref, o_ref, *, n_kv, group_heads):
    t = q_ref.shape[0]
    n_buf = kc_ref.shape[0]
    for kv in range(n_kv):
        heads = [g * group_heads + 2 * kv + r for g in range(N_GROUPS) for r in range(2)]
        q6 = jnp.concatenate([q_ref[:, h * HEAD_DIM:(h + 1) * HEAD_DIM] for h in heads], axis=0).astype(BF16)
        kc = _tile_rows(kc_ref, kv).astype(BF16)
        vc = _tile_rows(vc_ref, kv).astype(BF16)
        sl = slice(kv * HEAD_DIM, (kv + 1) * HEAD_DIM)
        kn = kn_ref[:, sl].astype(BF16)
        vn = vn_ref[:, sl]
        dn = (((1,), (1,)), ((), ()))
        s_c = lax.dot_general(q6, kc, dn, preferred_element_type=F32) + bc_ref[kv]
        s_n = lax.dot_general(q6, kn, dn, preferred_element_type=F32) + bn_ref[kv]
        m_row = jnp.maximum(jnp.max(s_c, axis=-1, keepdims=True), jnp.max(s_n, axis=-1, keepdims=True))
        m = jnp.max(m_row.reshape(N_GROUPS, 2 * t, 1), axis=0)
        m_all = jnp.concatenate([m] * N_GROUPS, axis=0)
        p_c = jnp.exp(s_c - m_all)
        p_n = jnp.exp(s_n - m_all)
        l_row = jnp.sum(p_c, axis=-1, keepdims=True) + jnp.sum(p_n, axis=-1, keepdims=True)
        o_row = jnp.dot(p_c.astype(BF16), vc, preferred_element_type=F32) + jnp.dot(p_n, vn, preferred_element_type=F32)
        l = jnp.sum(l_row.reshape(N_GROUPS, 2 * t, 1), axis=0)
        o = jnp.sum(o_row.reshape(N_GROUPS, 2 * t, HEAD_DIM), axis=0) / l
        for r in range(2):
            o_ref[:, (2 * kv + r) * HEAD_DIM:(2 * kv + r + 1) * HEAD_DIM] = o[r * t:(r + 1) * t]


def dilated_sample_attention(q, k_new, v_new, cache_k, cache_v, rel_bias, b, t, n_kv, group_heads):
    n_buf = cache_k.shape[1]
    bias = _sample_bias(rel_bias, group_heads, n_kv, t, n_buf)
    bias_c, bias_n = bias[:, :, :n_buf], bias[:, :, n_buf:]
    dq = q.shape[1]
    dkv = n_kv * HEAD_DIM
    return pl.pallas_call(
        functools.partial(_dilated_sample_kernel, n_kv=n_kv, group_heads=group_heads),
        grid=(b,),
        in_specs=[
            pl.BlockSpec((t, dq), lambda i: (i, 0)),
            pl.BlockSpec((None, n_buf, n_kv, HEAD_DIM), lambda i: (i, 0, 0, 0)),
            pl.BlockSpec((None, n_buf, n_kv, HEAD_DIM), lambda i: (i, 0, 0, 0)),
            pl.BlockSpec((t, dkv), lambda i: (i, 0)),
            pl.BlockSpec((t, dkv), lambda i: (i, 0)),
            pl.BlockSpec(bias_c.shape, lambda i: (0, 0, 0)),
            pl.BlockSpec(bias_n.shape, lambda i: (0, 0, 0)),
        ],
        out_specs=pl.BlockSpec((t, group_heads * HEAD_DIM), lambda i: (i, 0)),
        out_shape=jax.ShapeDtypeStruct((b * t, group_heads * HEAD_DIM), F32),
        compiler_params=_params("parallel"),
        name="dilated_sample_attention",
    )(q, cache_k, cache_v, k_new, v_new, bias_c, bias_n)


CONV_PAD_ROWS = 8


FFN_COL_TILE = 256
FFN_ROW_CHUNKS = 4


def _ffn_hidden_kernel(xp_ref, xs_ref, wu_ref, wg_ref, prev_ref, wc_ref, bc_ref, wd_ref, hp_ref, hs_ref, stp_ref,
                       sts_ref, wdb_ref, pad_ref, *, n_seq_s, t_s):
    i = pl.program_id(0)
    wdb_ref[...] = wd_ref[...].astype(BF16)
    t = xp_ref.shape[0]
    lo = CONV_PAD_ROWS - (CONV_W - 1)
    rc = t // FFN_ROW_CHUNKS
    wu = wu_ref[...].astype(BF16)
    wg = wg_ref[...].astype(BF16)

    def conv_gelu_gate(gate, start, rows):
        uc = bc_ref[...] + pad_ref[pl.ds(lo + start, rows), :] * wc_ref[0:1, :]
        for tap in range(1, CONV_W):
            uc = uc + pad_ref[pl.ds(lo + start + tap, rows), :] * wc_ref[tap:tap + 1, :]
        return 0.5 * uc * (1.0 + lax.erf(uc * SQRT_HALF)) * gate

    @pl.when(i == 0)
    def _():
        x = xs_ref[...]
        u = jnp.dot(x, wu, preferred_element_type=F32)
        gate = jnp.dot(x, wg, preferred_element_type=F32)
        pieces = []
        for sq in range(n_seq_s):
            rs = slice(sq * t_s, (sq + 1) * t_s)
            pad_ref[pl.ds(CONV_PAD_ROWS, t_s), :] = u[rs]
            pad_ref[pl.ds(lo, CONV_W - 1), :] = prev_ref[sq]
            pieces.append(conv_gelu_gate(gate[rs], 0, t_s))
            sts_ref[sq] = pad_ref[pl.ds(lo + t_s, CONV_W - 1), :]
        hs_ref[...] = jnp.concatenate(pieces, axis=0).astype(hs_ref.dtype)

    pad_ref[pl.ds(lo, CONV_W - 1), :] = jnp.zeros((CONV_W - 1, pad_ref.shape[1]), F32)
    gates = []
    for c in range(FFN_ROW_CHUNKS):
        x = xp_ref[c * rc:(c + 1) * rc, :]
        pad_ref[pl.ds(CONV_PAD_ROWS + c * rc, rc), :] = jnp.dot(x, wu, preferred_element_type=F32)
        gates.append(jnp.dot(x, wg, preferred_element_type=F32))
    for c in range(FFN_ROW_CHUNKS):
        hp_ref[c * rc:(c + 1) * rc, :] = conv_gelu_gate(gates[c], c * rc, rc).astype(hp_ref.dtype)
    stp_ref[...] = pad_ref[pl.ds(lo + t, CONV_W - 1), :]


def conv_ffn_hidden(xp, xs, w_up, w_gate, w_down, layer, prev_s, w_conv, b_conv, n_seq_p, t_p, n_seq_s, t_s):
    mp, d = xp.shape
    ms = xs.shape[0]
    f = w_up.shape[2]
    tn = FFN_COL_TILE
    assert mp == n_seq_p * t_p and ms == n_seq_s * t_s and f % tn == 0 and t_p % (FFN_ROW_CHUNKS * 128) == 0
    last = f // tn - 1
    n_col = f // tn
    assert f % (n_seq_p * n_col) == 0
    slab = f // (n_seq_p * n_col)
    wd_spec = lambda lead: pl.BlockSpec((None, slab, d), lambda i, j: (lead, i * n_col + j, 0))
    wspec = pl.BlockSpec((None, d, tn), lambda i, j: (layer, 0, j))
    s_col = lambda i, j: jnp.where(i == 0, j, last)
    return pl.pallas_call(
        functools.partial(_ffn_hidden_kernel, n_seq_s=n_seq_s, t_s=t_s),
        grid=(n_seq_p, f // tn),
        in_specs=[
            pl.BlockSpec((t_p, d), lambda i, j: (i, 0), pipeline_mode=pl.Buffered(1)),
            pl.BlockSpec((ms, d), lambda i, j: (0, 0)),
            wspec, wspec,
            pl.BlockSpec((None, n_seq_s, CONV_W - 1, tn), lambda i, j: (layer, 0, 0, s_col(i, j))),
            pl.BlockSpec((None, CONV_W, tn), lambda i, j: (layer, 0, j)),
            pl.BlockSpec((None, 1, tn), lambda i, j: (layer, 0, j)),
            wd_spec(layer),
        ],
        out_specs=[
            pl.BlockSpec((t_p, tn), lambda i, j: (i, j)),
            pl.BlockSpec((ms, tn), lambda i, j: (0, s_col(i, j))),
            pl.BlockSpec((None, CONV_W - 1, tn), lambda i, j: (i, 0, j)),
            pl.BlockSpec((n_seq_s, CONV_W - 1, tn), lambda i, j: (0, 0, s_col(i, j))),
            wd_spec(0),
        ],
        out_shape=[jax.ShapeDtypeStruct((mp, f), BF16), jax.ShapeDtypeStruct((ms, f), BF16),
                   jax.ShapeDtypeStruct((n_seq_p, CONV_W - 1, f), F32),
                   jax.ShapeDtypeStruct((n_seq_s, CONV_W - 1, f), F32),
                   jax.ShapeDtypeStruct((1, f, d), BF16)],
        scratch_shapes=[pltpu.VMEM((t_p + CONV_PAD_ROWS, tn), F32)],
        compiler_params=_params("arbitrary", "arbitrary"),
        name="conv_ffn_hidden",
    )(xp, xs, w_up, w_gate, prev_s, w_conv, b_conv.reshape(b_conv.shape[0], 1, f), w_down)


def _shift_append_kernel(old_ref, new_ref, o_ref):
    n_keep = old_ref.shape[1]
    o_ref[:n_keep] = old_ref[0]
    o_ref[n_keep:] = new_ref[...]


def shift_append(cache, new):
    b, n_buf, kv, dh = cache.shape
    t = new.shape[1]
    old_block = (pl.Element(1), pl.Element(n_buf - t), pl.Element(kv), pl.Element(dh))
    return pl.pallas_call(
        _shift_append_kernel,
        grid=(b,),
        in_specs=[pl.BlockSpec(old_block, lambda i: (i, t, 0, 0)),
                  pl.BlockSpec((None, t, kv, dh), lambda i: (i, 0, 0, 0))],
        out_specs=pl.BlockSpec((None, n_buf, kv, dh), lambda i: (i, 0, 0, 0)),
        out_shape=jax.ShapeDtypeStruct(cache.shape, cache.dtype),
        compiler_params=_params("parallel"),
        name="shift_append",
    )(cache, new)


def kernel(x_prompt, x_sample, cache_fox_k, cache_fox_v, cache_fox_logf, cache_win_k, cache_win_v, state_ffn_conv,
           page_table, w_in_a, b_f_a, w_o_a, g_attn, g_kv, w_kv_b, w_q_b, w_o_b, rel_bias,
           g_ffn, w_up, w_gate, w_conv, b_conv, w_down, g_final):
    bp, sp, d_model = x_prompt.shape
    bs, ts, _ = x_sample.shape
    n_heads = b_f_a.shape[1]
    hd = n_heads * HEAD_DIM
    n_kv = cache_win_k.shape[2]
    kvw = n_kv * HEAD_DIM
    group_heads = w_q_b.shape[2] // (N_GROUPS * HEAD_DIM)
    d_ff = w_up.shape[2]
    n_pages = page_table.shape[1]
    assert w_in_a.shape[0] == 1 and w_q_b.shape[0] == 1 and g_attn.shape[0] == 2
    mp, ms = bp * sp, bs * ts

    b_f = b_f_a[0][None].astype(F32)
    w_in_t = jnp.swapaxes(w_in_a, 1, 2)
    xp = x_prompt.reshape(mp, d_model)
    xs = x_sample.reshape(ms, d_model)

    def norm(x, gains, dtype=BF16):
        return rmsnorm(x, gains, dtype)

    def ffn(xp, xs, layer):
        (xnp,) = norm(xp, g_ffn[layer][None])
        (xns,) = norm(xs, g_ffn[layer][None])
        hp, hs, stp, sts, w_down_b = conv_ffn_hidden(xnp, xns, w_up, w_gate, w_down, layer, state_ffn_conv, w_conv,
                                                     b_conv, bp, sp, bs, ts)
        xp = matmul(hp, w_down_b, 0, res=xp, bm=512, bn=512)
        xs = matmul(hs, w_down_b, 0, res=xs, bm=ms, bn=1024, bk=d_ff // 2)
        return xp, xs, stp, sts

    (xnp,) = norm(xp, g_attn[0][None])
    (xns,) = norm(xs, g_attn[0][None])
    qp, qs = matmul_pair(xnp, xns, w_in_t, 0, 0, hd, out_dtype=BF16, out_scale=ATTN_SCALE, transposed=True)
    kp, ks = matmul_pair(xnp, xns, w_in_t, 0, hd, hd, transposed=True)
    vp, vs = matmul_pair(xnp, xns, w_in_t, 0, 2 * hd, hd, transposed=True)
    lfp = forget_gate(xnp, w_in_t, 0, 3 * hd, b_f)
    lfs = forget_gate(xns, w_in_t, 0, 3 * hd, b_f)
    cp = cumsum_rows(lfp.reshape(bp, sp, n_heads))
    cs = cumsum_rows(lfs.reshape(bs, ts, n_heads))
    pt_flat = page_table.reshape(-1)
    d_past = page_suffix_sums(pt_flat, cache_fox_logf[0], bs, n_pages)
    ap, a_s = fox_attention(qp, kp, vp, cp.reshape(mp, n_heads), bp, sp, qs, cs, d_past, ks, vs, cache_fox_k[0],
                            cache_fox_v[0], pt_flat, bs, ts, n_heads, n_pages)
    a_s = a_s.astype(BF16)
    xp, xs = matmul_pair(ap, a_s, w_o_a, 0, 0, d_model, res=(xp, xs))
    xp, xs, st0p, st0s = ffn(xp, xs, 0)

    xnp, xkvp = norm(xp, jnp.stack([g_attn[1], g_kv]))
    xns, xkvs = norm(xs, jnp.stack([g_attn[1], g_kv]))
    kbp, kbs = matmul_pair(xkvp, xkvs, w_kv_b[None], 0, 0, kvw)
    vbp, vbs = matmul_pair(xkvp, xkvs, w_kv_b[None], 0, kvw, kvw)
    qp, qs = matmul_pair(xnp, xns, w_q_b, 0, 0, w_q_b.shape[2], out_scale=ATTN_SCALE)
    ap = dilated_prompt_attention(qp, kbp, vbp, rel_bias, bp, sp, n_kv, group_heads)
    a_s = dilated_sample_attention(qs, kbs, vbs, cache_win_k, cache_win_v, rel_bias, bs, ts, n_kv,
                                   group_heads).astype(BF16)
    xp, xs = matmul_pair(ap, a_s, w_o_b, 0, 0, d_model, res=(xp, xs))
    xp, xs, st1p, st1s = ffn(xp, xs, 1)
    (yp,) = norm(xp, g_final[None], F32)
    (ys,) = norm(xs, g_final[None], F32)

    n_pg = sp // PAGE_SIZE
    keep = min(BUCKET_MAX_DIST, sp)
    return (
        yp.reshape(bp, sp, d_model),
        ys.reshape(bs, ts, d_model),
        kp.reshape(1, bp, n_pg, PAGE_SIZE, n_heads, HEAD_DIM),
        vp.reshape(1, bp, n_pg, PAGE_SIZE, n_heads, HEAD_DIM),
        lfp.reshape(1, bp, n_pg, PAGE_SIZE, n_heads),
        ks.reshape(1, bs, ts, n_heads, HEAD_DIM),
        vs.reshape(1, bs, ts, n_heads, HEAD_DIM),
        lfs.reshape(1, bs, ts, n_heads),
        kbp.reshape(bp, sp, n_kv, HEAD_DIM)[:, sp - keep:],
        vbp.reshape(bp, sp, n_kv, HEAD_DIM)[:, sp - keep:],
        shift_append(cache_win_k, kbs.reshape(bs, ts, n_kv, HEAD_DIM)),
        shift_append(cache_win_v, vbs.reshape(bs, ts, n_kv, HEAD_DIM)),
        jnp.stack([st0p, st1p]),
        jnp.stack([st0s, st1s]),
    )
```
